```python
import math
import jax, jax.numpy as jnp
from jax import lax
import numpy as np

D_MODEL = 2048
BATCH = 8
SEQ = 4096
DEPTH = 4

HEAD_DIM = 128
EPS = 1e-6
NEG = -1e30

NUM_BUCKETS = 32
MAX_DISTANCE = 2048

NSA_HEADS = 8
NSA_GROUPS = 2
NSA_HPG = NSA_HEADS // NSA_GROUPS
NSA_BLOCK = 64
NSA_TOPK = 16
NSA_WINDOW = 512
NSA_QCHUNK = 32

DIL_PAIRS = ((128, 1), (512, 4), (2048, 16))
DIL_GROUPS = 3
DIL_HPG = 4
DIL_HEADS = DIL_GROUPS * DIL_HPG
DIL_QCHUNK = 32

MLA_HEADS = 16
MLA_Q_RANK = 1536
MLA_KV_RANK = 512
MLA_NOPE = 128
MLA_ROPE = 64
MLA_V = 128
MLA_QBLOCK = 128
ROPE_THETA = 10000.0

A_Q = NSA_HEADS * HEAD_DIM
A_KV = 6 * NSA_GROUPS * HEAD_DIM
A_G3 = 3 * NSA_HEADS
A_GATE = NSA_HEADS * HEAD_DIM
B_QKV = 3 * DIL_HEADS * HEAD_DIM
B_GATE = DIL_HPG * HEAD_DIM
EVEN_IN = A_Q + A_KV + A_G3 + A_GATE + B_QKV + B_GATE
EVEN_MIX = A_GATE + B_GATE
ODD_MIX = MLA_HEADS * MLA_V
ODD_IN = MLA_Q_RANK + MLA_KV_RANK + MLA_ROPE + ODD_MIX

kernel_name = "hybrid_nsa_dilated_mla_trunk"


def _split_points(widths):
    pts, acc = [], 0
    for w in widths[:-1]:
        acc += w
        pts.append(acc)
    return pts


def rmsnorm(x, w):
    xf = x.astype(jnp.float32)
    y = xf * lax.rsqrt(jnp.mean(xf * xf, axis=-1, keepdims=True) + EPS)
    return (y * w.astype(jnp.float32)).astype(x.dtype)


def t5_bucket(dist):
    max_exact = NUM_BUCKETS // 2
    d = jnp.maximum(dist, 0)
    df = jnp.maximum(d, 1).astype(jnp.float32)
    large = max_exact + (jnp.log(df / max_exact) / math.log(MAX_DISTANCE / max_exact)
                         * (NUM_BUCKETS - max_exact)).astype(jnp.int32)
    large = jnp.minimum(large, NUM_BUCKETS - 1)
    return jnp.where(d < max_exact, d, large)


def masked_softmax(s, mask):
    p = jax.nn.softmax(jnp.where(mask, s, NEG), axis=-1)
    return jnp.where(mask, p, 0.0)


def rope(x, pos):
    half = x.shape[-1] // 2
    freqs = 1.0 / (ROPE_THETA ** (jnp.arange(half, dtype=jnp.float32) / half))
    ang = pos.astype(jnp.float32)[:, None] * freqs[None, :]
    cos = jnp.cos(ang)[:, None, :]
    sin = jnp.sin(ang)[:, None, :]
    xf = x.astype(jnp.float32)
    x1, x2 = xf[..., :half], xf[..., half:]
    return jnp.concatenate([x1 * cos - x2 * sin, x1 * sin + x2 * cos], axis=-1).astype(x.dtype)


def nsa_compress(kv, pos, w1, w2):
    b_, t_ = kv.shape[:2]
    nblk = t_ // NSA_BLOCK
    blk = kv.reshape(b_, nblk, NSA_BLOCK, NSA_GROUPS, HEAD_DIM) + pos[None, None, :, None, :]
    blk = blk.transpose(0, 1, 3, 2, 4).reshape(b_, nblk, NSA_GROUPS, NSA_BLOCK * HEAD_DIM)
    return jax.nn.silu(blk @ w1) @ w2


def nsa_attention(q, kv_all, gate_logits, cmp_pos, cmp_w1, cmp_w2, table):
    b_, t_ = q.shape[:2]
    L, W, QC, G, HPG = NSA_BLOCK, NSA_WINDOW, NSA_QCHUNK, NSA_GROUPS, NSA_HPG
    nblk = t_ // L
    nsel = min(NSA_TOPK, nblk)
    kc, vc, ks, vs, kw, vw = [a.reshape(b_, t_, G, HEAD_DIM) for a in jnp.split(kv_all, 6, axis=-1)]
    k_cmp = nsa_compress(kc, cmp_pos[0], cmp_w1[0], cmp_w2[0])
    v_cmp = nsa_compress(vc, cmp_pos[1], cmp_w1[1], cmp_w2[1])
    ks_blk = ks.reshape(b_, nblk, L, G, HEAD_DIM).transpose(0, 3, 1, 2, 4)
    vs_blk = vs.reshape(b_, nblk, L, G, HEAD_DIM).transpose(0, 3, 1, 2, 4)
    kw_pad = jnp.pad(kw, ((0, 0), (W, 0), (0, 0), (0, 0)))
    vw_pad = jnp.pad(vw, ((0, 0), (W, 0), (0, 0), (0, 0)))
    tab = table[:, :NSA_HEADS].reshape(NUM_BUCKETS, G, HPG)
    cmp_end = jnp.arange(nblk) * L + (L - 1)
    blk_ids = jnp.arange(nblk)
    scale = HEAD_DIM ** -0.5
    nc = t_ // QC
    q_chunks = q.reshape(b_, nc, QC, G, HPG, HEAD_DIM).transpose(1, 0, 2, 3, 4, 5)
    bidx = jnp.arange(b_)[:, None, None, None]
    gidx = jnp.arange(G)[None, :, None, None]

    def chunk(args):
        qb, c = args
        t = c * QC + jnp.arange(QC)
        s = jnp.einsum('bqghd,bngd->bghqn', qb, k_cmp).astype(jnp.float32) * scale
        dist = t[:, None] - cmp_end[None, :]
        s = s + tab[t5_bucket(dist)].transpose(2, 3, 0, 1)[None]
        p_cmp = masked_softmax(s, dist >= 0)
        o_cmp = jnp.einsum('bghqn,bngd->bqghd', p_cmp.astype(v_cmp.dtype), v_cmp)
        imp = p_cmp.sum(axis=2)
        cur = t // L
        forced = (blk_ids[None, :] == 0) | (blk_ids[None, :] == cur[:, None]) | (blk_ids[None, :] == cur[:, None] - 1)
        imp = jnp.where(forced, jnp.inf, jnp.where(blk_ids[None, :] > cur[:, None], -jnp.inf, imp))
        _, sel = lax.top_k(imp, nsel)
        k_sel = ks_blk[bidx, gidx, sel]
        v_sel = vs_blk[bidx, gidx, sel]
        s = jnp.einsum('bqghd,bgqskd->bghqsk', qb, k_sel).astype(jnp.float32) * scale
        kpos = sel[..., None] * L + jnp.arange(L)
        dist = t[None, None, :, None, None] - kpos
        bias = tab[t5_bucket(dist), jnp.arange(G)[None, :, None, None, None]]
        s = (s + bias.transpose(0, 1, 5, 2, 3, 4)).reshape(b_, G, HPG, QC, nsel * L)
        mask = (dist >= 0)[:, :, None].reshape(b_, G, 1, QC, nsel * L)
        p = masked_softmax(s, mask).reshape(b_, G, HPG, QC, nsel, L)
        o_slc = jnp.einsum('bghqsk,bgqskd->bqghd', p.astype(v_sel.dtype), v_sel)
        kwin = lax.dynamic_slice_in_dim(kw_pad, c * QC, W + QC, axis=1)
        vwin = lax.dynamic_slice_in_dim(vw_pad, c * QC, W + QC, axis=1)
        kp = c * QC - W + jnp.arange(W + QC)
        dist = t[:, None] - kp[None, :]
        mask = (dist >= 0) & (dist < W) & (kp[None, :] >= 0)
        s = jnp.einsum('bqghd,bkgd->bghqk', qb, kwin).astype(jnp.float32) * scale
        s = s + tab[t5_bucket(dist)].transpose(2, 3, 0, 1)[None]
        p = masked_softmax(s, mask)
        o_win = jnp.einsum('bghqk,bkgd->bqghd', p.astype(vwin.dtype), vwin)
        return o_cmp, o_slc, o_win

    o_cmp, o_slc, o_win = lax.map(chunk, (q_chunks, jnp.arange(nc)))

    def unchunk(a):
        return a.transpose(1, 0, 2, 3, 4, 5).reshape(b_, t_, NSA_HEADS, HEAD_DIM)

    g = jax.nn.sigmoid(gate_logits.astype(jnp.float32)).reshape(b_, t_, NSA_HEADS, 3, 1).astype(q.dtype)
    out = g[:, :, :, 0] * unchunk(o_cmp) + g[:, :, :, 1] * unchunk(o_slc) + g[:, :, :, 2] * unchunk(o_win)
    return out.reshape(b_, t_, NSA_HEADS * HEAD_DIM)


def dilated_attention(q, k, v, table):
    b_, t_ = q.shape[:2]
    QC = DIL_QCHUNK
    nc = t_ // QC
    scale = HEAD_DIM ** -0.5
    tab = table[:, NSA_HEADS:].reshape(NUM_BUCKETS, DIL_GROUPS, DIL_HPG)
    k_groups = [k[:, :, g] for g in range(DIL_GROUPS)]
    v_groups = [v[:, :, g] for g in range(DIL_GROUPS)]
    biases = []
    for g, (w, d) in enumerate(DIL_PAIRS):
        offs = jnp.arange(w // d + 1) * d
        biases.append(tab[t5_bucket(offs), g])
    q_chunks = q.reshape(b_, nc, QC, DIL_GROUPS, DIL_HPG, HEAD_DIM).transpose(1, 0, 2, 3, 4, 5)

    def chunk(args):
        qb, c = args
        t = c * QC + jnp.arange(QC)
        outs, lses = [], []
        for g, (w, d) in enumerate(DIL_PAIRS):
            m = w // d + 1
            idx = t[:, None] - jnp.arange(m)[None, :] * d
            valid = idx >= 0
            idxc = jnp.maximum(idx, 0)
            kg = k_groups[g][:, idxc]
            vg = v_groups[g][:, idxc]
            s = jnp.einsum('bqhd,bqmhd->bhqm', qb[:, :, g], kg).astype(jnp.float32) * scale
            s = jnp.where(valid[None, None], s + biases[g].T[None, :, None, :], NEG)
            lse = jax.nn.logsumexp(s, axis=-1)
            p = jnp.exp(s - lse[..., None])
            outs.append(jnp.einsum('bhqm,bqmhd->bqhd', p.astype(vg.dtype), vg))
            lses.append(lse)
        alpha = jax.nn.softmax(jnp.stack(lses, 0), axis=0)
        o = jnp.einsum('gbhq,gbqhd->bqhd', alpha, jnp.stack(outs, 0).astype(jnp.float32))
        return o.astype(q.dtype)

    o = lax.map(chunk, (q_chunks, jnp.arange(nc)))
    return o.transpose(1, 0, 2, 3, 4).reshape(b_, t_, DIL_HPG * HEAD_DIM)


def mla_attention(cq, ckv, kpe, q_norm, w_qb, kv_norm, w_kvb):
    b_, t_ = cq.shape[:2]
    H, QB = MLA_HEADS, MLA_QBLOCK
    pos = jnp.arange(t_)
    q = (rmsnorm(cq, q_norm) @ w_qb).reshape(b_, t_, H, MLA_NOPE + MLA_ROPE)
    q_nope, q_pe = q[..., :MLA_NOPE], rope(q[..., MLA_NOPE:], pos)
    kv = (rmsnorm(ckv, kv_norm) @ w_kvb).reshape(b_, t_, H, MLA_NOPE + MLA_V)
    k_nope, v = kv[..., :MLA_NOPE], kv[..., MLA_NOPE:]
    k_pe = rope(kpe[:, :, None, :], pos)[:, :, 0]
    scale = (MLA_NOPE + MLA_ROPE) ** -0.5
    nb = t_ // QB
    qn = q_nope.reshape(b_, nb, QB, H, MLA_NOPE).transpose(1, 0, 2, 3, 4)
    qp = q_pe.reshape(b_, nb, QB, H, MLA_ROPE).transpose(1, 0, 2, 3, 4)

    def block(args):
        qnb, qpb, c = args
        t = c * QB + jnp.arange(QB)
        s = (jnp.einsum('bqhd,bkhd->bhqk', qnb, k_nope)
             + jnp.einsum('bqhd,bkd->bhqk', qpb, k_pe)).astype(jnp.float32) * scale
        p = masked_softmax(s, pos[None, :] <= t[:, None])
        return jnp.einsum('bhqk,bkhd->bqhd', p.astype(v.dtype), v)

    o = lax.map(block, (qn, qp, jnp.arange(nb)))
    return o.transpose(1, 0, 2, 3, 4).reshape(b_, t_, H * MLA_V)


def setup_inputs(seed: int = 0) -> dict:
    key = jax.random.key(seed)
    ks = jax.random.split(key, 16)
    ne = (DEPTH + 1) // 2
    no = DEPTH // 2

    def nrm(k, shape, scale):
        return jax.random.normal(k, shape, jnp.float32) * scale

    return {
        "x": nrm(ks[0], (BATCH, SEQ, D_MODEL), 1.0),
        "rel_bias": nrm(ks[1], (NUM_BUCKETS, NSA_HEADS + DIL_HEADS), 0.2),
        "norm_w": 1.0 + nrm(ks[2], (DEPTH, D_MODEL), 0.02),
        "final_norm_w": 1.0 + nrm(ks[3], (D_MODEL,), 0.02),
        "ev_w_in": nrm(ks[4], (ne, D_MODEL, EVEN_IN), D_MODEL ** -0.5),
        "nsa_cmp_pos": nrm(ks[5], (ne, 2, NSA_BLOCK, HEAD_DIM), 0.1),
        "nsa_cmp_w1": nrm(ks[6], (ne, 2, NSA_BLOCK * HEAD_DIM, HEAD_DIM), (NSA_BLOCK * HEAD_DIM) ** -0.5),
        "nsa_cmp_w2": nrm(ks[7], (ne, 2, HEAD_DIM, HEAD_DIM), HEAD_DIM ** -0.5),
        "ev_w_out": nrm(ks[8], (ne, EVEN_MIX, D_MODEL), 0.5 * EVEN_MIX ** -0.5),
        "od_w_in": nrm(ks[9], (no, D_MODEL, ODD_IN), D_MODEL ** -0.5),
        "mla_q_norm": 1.0 + nrm(ks[10], (no, MLA_Q_RANK), 0.02),
        "mla_w_qb": nrm(ks[11], (no, MLA_Q_RANK, MLA_HEADS * (MLA_NOPE + MLA_ROPE)), MLA_Q_RANK ** -0.5),
        "mla_kv_norm": 1.0 + nrm(ks[12], (no, MLA_KV_RANK), 0.02),
        "mla_w_kvb": nrm(ks[13], (no, MLA_KV_RANK, MLA_HEADS * (MLA_NOPE + MLA_V)), MLA_KV_RANK ** -0.5),
        "od_w_out": nrm(ks[14], (no, ODD_MIX, D_MODEL), 0.5 * ODD_MIX ** -0.5),
    }


def reference(x, rel_bias, norm_w, final_norm_w, ev_w_in, nsa_cmp_pos, nsa_cmp_w1, nsa_cmp_w2,
              ev_w_out, od_w_in, mla_q_norm, mla_w_qb, mla_kv_norm, mla_w_kvb, od_w_out):
    b_, t_ = x.shape[:2]
    even_pts = _split_points([A_Q, A_KV, A_G3, A_GATE, B_QKV, B_GATE])
    odd_pts = _split_points([MLA_Q_RANK, MLA_KV_RANK, MLA_ROPE, ODD_MIX])
    for l in range(DEPTH):
        h = rmsnorm(x, norm_w[l])
        i = l // 2
        if l % 2 == 0:
            z = h @ ev_w_in[i]
            q_a, kv_a, g3_a, gate_a, qkv_b, gate_b = jnp.split(z, even_pts, axis=-1)
            o_a = nsa_attention(q_a.reshape(b_, t_, NSA_HEADS, HEAD_DIM), kv_a, g3_a,
                                nsa_cmp_pos[i], nsa_cmp_w1[i], nsa_cmp_w2[i], rel_bias)
            q_b, k_b, v_b = [a.reshape(b_, t_, DIL_GROUPS, DIL_HPG, HEAD_DIM)
                             for a in jnp.split(qkv_b, 3, axis=-1)]
            o_b = dilated_attention(q_b, k_b, v_b, rel_bias)
            mixed = jnp.concatenate([o_a * jax.nn.silu(gate_a), o_b * jax.nn.silu(gate_b)], axis=-1)
            x = x + mixed @ ev_w_out[i]
        else:
            z = h @ od_w_in[i]
            cq, ckv, kpe, gate_c = jnp.split(z, odd_pts, axis=-1)
            o_c = mla_attention(cq, ckv, kpe, mla_q_norm[i], mla_w_qb[i], mla_kv_norm[i], mla_w_kvb[i])
            x = x + (o_c * jax.nn.silu(gate_c)) @ od_w_out[i]
    return rmsnorm(x, final_norm_w)
```

```python
import functools
import math

import numpy as np
import jax
import jax.numpy as jnp
from jax import lax
from jax.experimental import pallas as pl
from jax.experimental.pallas import tpu as pltpu

HEAD_DIM = 128
EPS = 1e-6
NEG = -1e30
NUM_BUCKETS = 32
MAX_DISTANCE = 2048
NSA_HEADS = 8
NSA_GROUPS = 2
NSA_HPG = 4
NSA_BLOCK = 64
NSA_TOPK = 16
NSA_WINDOW = 512
DIL_PAIRS = ((128, 1), (512, 4), (2048, 16))
DIL_GROUPS = 3
DIL_HPG = 4
DIL_HEADS = 12
MLA_HEADS = 16
MLA_Q_RANK = 1536
MLA_KV_RANK = 512
MLA_NOPE = 128
MLA_ROPE = 64
MLA_V = 128
ROPE_THETA = 10000.0

A_Q = NSA_HEADS * HEAD_DIM
A_KV = 6 * NSA_GROUPS * HEAD_DIM
A_G3 = 3 * NSA_HEADS
A_GATE = NSA_HEADS * HEAD_DIM
B_QKV = 3 * DIL_HEADS * HEAD_DIM
B_GATE = DIL_HPG * HEAD_DIM

LANES = 128
VMEM_LIMIT = 48 * 1024 * 1024

NSA_TQ = 256
BAND_T = 128
MLA_TQ = 512

_BF = jnp.bfloat16
_F32 = jnp.float32


def _cparams(sem):
    return pltpu.CompilerParams(dimension_semantics=sem, vmem_limit_bytes=VMEM_LIMIT)


def _dot(a, b):
    return jnp.dot(a, b, preferred_element_type=_F32)


def _dot_nt(a, b):
    return lax.dot_general(a, b, (((1,), (1,)), ((), ())), preferred_element_type=_F32)


def _silu(x):
    return x * (1.0 / (1.0 + jnp.exp(-x)))


def _sigmoid(x):
    return 1.0 / (1.0 + jnp.exp(-x))


def _pick_tile(n, candidates):
    for c in candidates:
        if n % c == 0:
            return c
    return n


def _rmsnorm_kernel(x_ref, w_ref, o_ref):
    x = x_ref[...]
    y = x * lax.rsqrt(jnp.mean(x * x, axis=-1, keepdims=True) + EPS)
    o_ref[...] = (y * w_ref[...]).astype(o_ref.dtype)


def _rmsnorm(x2d, w, width, col_block, out_dtype):
    m = x2d.shape[0]
    tm = _pick_tile(m, (512, 256, 128))
    return pl.pallas_call(
        _rmsnorm_kernel,
        grid=(m // tm,),
        in_specs=[pl.BlockSpec((tm, width), lambda i: (i, col_block)),
                  pl.BlockSpec((1, width), lambda i: (0, 0))],
        out_specs=pl.BlockSpec((tm, width), lambda i: (i, 0)),
        out_shape=jax.ShapeDtypeStruct((m, width), out_dtype),
        compiler_params=_cparams(("parallel",)),
        name="rmsnorm",
    )(x2d, w.reshape(1, width).astype(_F32))


def _matmul_kernel(a_ref, w_ref, o_ref):
    o_ref[...] = _dot(a_ref[...], w_ref[...]).astype(o_ref.dtype)


def _matmul(a, w, out_dtype):
    m, k = a.shape
    n = w.shape[1]
    tm = _pick_tile(m, (1024, 512, 256, 128))
    tn = _pick_tile(n, (1024, 768, 512, 384, 256, 128))
    return pl.pallas_call(
        _matmul_kernel,
        grid=(m // tm, n // tn),
        in_specs=[pl.BlockSpec((tm, k), lambda i, j: (i, 0)),
                  pl.BlockSpec((k, tn), lambda i, j: (0, j))],
        out_specs=pl.BlockSpec((tm, tn), lambda i, j: (i, j)),
        out_shape=jax.ShapeDtypeStruct((m, n), out_dtype),
        compiler_params=_cparams(("parallel", "arbitrary")),
        name="matmul",
    )(a, w)


def _rope_chunk(chunk, cos, sin):
    return chunk * cos + pltpu.roll(chunk, 64, 1) * sin


def _matmul_rope_kernel(a_ref, w_ref, cos_ref, sin_ref, o_ref, *, heads_per_tile):
    acc = _dot(a_ref[...], w_ref[...])
    cos = cos_ref[...]
    sin = sin_ref[...]
    for h in range(heads_per_tile):
        base = h * 2 * LANES
        o_ref[:, base:base + LANES] = acc[:, base:base + LANES].astype(o_ref.dtype)
        o_ref[:, base + LANES:base + 2 * LANES] = _rope_chunk(
            acc[:, base + LANES:base + 2 * LANES], cos, sin).astype(o_ref.dtype)


def _matmul_rope(a, w, cos, sin, seq):
    m, k = a.shape
    n = w.shape[1]
    tm = _pick_tile(seq, (512, 256, 128))
    tn = 1024
    tpb = seq // tm
    return pl.pallas_call(
        functools.partial(_matmul_rope_kernel, heads_per_tile=tn // (2 * LANES)),
        grid=(m // tm, n // tn),
        in_specs=[pl.BlockSpec((tm, k), lambda i, j: (i, 0)),
                  pl.BlockSpec((k, tn), lambda i, j: (0, j)),
                  pl.BlockSpec((tm, LANES), lambda i, j: (i % tpb, 0)),
                  pl.BlockSpec((tm, LANES), lambda i, j: (i % tpb, 0))],
        out_specs=pl.BlockSpec((tm, tn), lambda i, j: (i, j)),
        out_shape=jax.ShapeDtypeStruct((m, n), _BF),
        compiler_params=_cparams(("parallel", "arbitrary")),
        name="matmul_rope",
    )(a, w, cos, sin)


def _rope_cols_kernel(x_ref, cos_ref, sin_ref, o_ref):
    o_ref[...] = _rope_chunk(x_ref[...], cos_ref[...], sin_ref[...]).astype(o_ref.dtype)


def _rope_cols(z2d, col_block, cos, sin, seq):
    m = z2d.shape[0]
    tm = _pick_tile(seq, (512, 256, 128))
    tpb = seq // tm
    return pl.pallas_call(
        _rope_cols_kernel,
        grid=(m // tm,),
        in_specs=[pl.BlockSpec((tm, LANES), lambda i: (i, col_block)),
                  pl.BlockSpec((tm, LANES), lambda i: (i % tpb, 0)),
                  pl.BlockSpec((tm, LANES), lambda i: (i % tpb, 0))],
        out_specs=pl.BlockSpec((tm, LANES), lambda i: (i, 0)),
        out_shape=jax.ShapeDtypeStruct((m, LANES), _BF),
        compiler_params=_cparams(("parallel",)),
        name="rope_kpe",
    )(z2d, cos, sin)


def _compress_kernel(blk_ref, pos_ref, w1_ref, w2_ref, o_ref, acc_ref, *, lt):
    li = pl.program_id(1)

    @pl.when(li == 0)
    def _():
        acc_ref[...] = jnp.zeros_like(acc_ref)

    acc = acc_ref[...]
    for l in range(lt):
        a = (blk_ref[l] + pos_ref[l:l + 1, :]).astype(_BF)
        acc = acc + _dot(a, w1_ref[l])
    acc_ref[...] = acc

    @pl.when(li == pl.num_programs(1) - 1)
    def _():
        hid = _silu(acc_ref[...]).astype(_BF)
        o_ref[...] = _dot(hid, w2_ref[...])


def _compress(kcvc_t, pos, w1, w2):
    _, L, R, _ = kcvc_t.shape
    lt = 16
    return pl.pallas_call(
        functools.partial(_compress_kernel, lt=lt),
        grid=(4, L // lt),
        in_specs=[pl.BlockSpec((None, lt, R, HEAD_DIM), lambda c, l: (c, l, 0, 0)),
                  pl.BlockSpec((None, lt, HEAD_DIM), lambda c, l: (c // 2, l, 0)),
                  pl.BlockSpec((None, lt, HEAD_DIM, HEAD_DIM), lambda c, l: (c // 2, l, 0, 0)),
                  pl.BlockSpec((None, HEAD_DIM, HEAD_DIM), lambda c, l: (c // 2, 0, 0))],
        out_specs=pl.BlockSpec((None, R, HEAD_DIM), lambda c, l: (c, 0, 0)),
        out_shape=jax.ShapeDtypeStruct((4, R, HEAD_DIM), _F32),
        scratch_shapes=[pltpu.VMEM((R, HEAD_DIM), _F32)],
        compiler_params=_cparams(("parallel", "arbitrary")),
        name="nsa_compress",
    )(kcvc_t, pos, w1.reshape(2, L, HEAD_DIM, HEAD_DIM), w2)


def _flash_step(s, v, m, l, acc):
    m_new = jnp.maximum(m, jnp.max(s, axis=1, keepdims=True))
    alpha = jnp.exp(m - m_new)
    p = jnp.exp(s - m_new)
    l = alpha * l + jnp.sum(p, axis=1, keepdims=True)
    acc = alpha * acc + _dot(p.astype(_BF), v)
    return m_new, l, acc


def _flash_init(rows, width):
    return (jnp.full((rows, 1), -jnp.inf, _F32), jnp.zeros((rows, 1), _F32),
            jnp.zeros((rows, width), _F32))


def _nsa_kernel(q_ref, ks_ref, vs_ref, kw_ref, vw_ref, kc_ref, vc_ref, g3_ref, oh_ref,
                bt_ref, bc_ref, o_ref, *, seq):
    tq = NSA_TQ
    hp = NSA_HPG
    nblk = seq // NSA_BLOCK
    qi = pl.program_id(2)
    scale = HEAD_DIM ** -0.5
    t0 = qi * tq

    qs = jnp.concatenate([q_ref[:, h * HEAD_DIM:(h + 1) * HEAD_DIM] for h in range(hp)], axis=0)

    kc = kc_ref[...].astype(_BF)
    s_c = _dot_nt(kc, qs) * scale
    s_c = s_c + jnp.concatenate([bc_ref[h] for h in range(hp)], axis=1)
    n_col = lax.broadcasted_iota(jnp.int32, (nblk, tq), 0)
    t_row = t0 + lax.broadcasted_iota(jnp.int32, (nblk, tq), 1)
    n_all = lax.broadcasted_iota(jnp.int32, (nblk, hp * tq), 0)
    t_all = t0 + (lax.broadcasted_iota(jnp.int32, (nblk, hp * tq), 1) & (tq - 1))
    valid = t_all >= n_all * NSA_BLOCK + (NSA_BLOCK - 1)
    s_c = jnp.where(valid, s_c, NEG)
    m_c = jnp.max(s_c, axis=0, keepdims=True)
    p_c = jnp.where(valid, jnp.exp(s_c - m_c), 0.0)
    l_c = jnp.sum(p_c, axis=0, keepdims=True)
    p_c = p_c / jnp.where(l_c > 0.0, l_c, 1.0)
    imp = p_c[:, 0:tq]
    for h in range(1, hp):
        imp = imp + p_c[:, h * tq:(h + 1) * tq]

    pad_rows = LANES - nblk
    p_pad = jnp.concatenate([p_c, jnp.zeros((pad_rows, hp * tq), _F32)], axis=0)
    p_t = jnp.transpose(p_pad).astype(_BF)
    vc_pad = jnp.concatenate([vc_ref[...], jnp.zeros((pad_rows, HEAD_DIM), _F32)], axis=0).astype(_BF)
    o_cmp = _dot(p_t, vc_pad)

    cur = jnp.right_shift(t_row, int(math.log2(NSA_BLOCK)))
    forced = (n_col == 0) | (n_col == cur) | (n_col == cur - 1)
    impv = jnp.where(forced, jnp.inf, jnp.where(n_col > cur, -jnp.inf, imp))
    rank = jnp.zeros((nblk, tq), _F32)
    for i in range(nblk):
        row = impv[i:i + 1, :]
        ge = jnp.where(row >= impv, 1.0, 0.0)
        gt = jnp.where(row > impv, 1.0, 0.0)
        rank = rank + jnp.where(n_col > i, ge, gt)
    sel = (rank < float(min(NSA_TOPK, nblk))) & (n_col <= cur)
    pen = jnp.where(sel, 0.0, NEG)
    pen = jnp.concatenate([pen, jnp.full((pad_rows, tq), NEG, _F32)], axis=0)
    pen_t = jnp.transpose(pen).astype(_BF)
    q_aug = jnp.concatenate([qs, jnp.concatenate([pen_t] * hp, axis=0)], axis=1)

    ii = lax.broadcasted_iota(jnp.int32, (hp * tq, tq), 0) & (tq - 1)
    jj = lax.broadcasted_iota(jnp.int32, (hp * tq, tq), 1)

    def slc_tile(ki):
        rows = pl.ds(pl.multiple_of(ki * tq, tq), tq)
        k_aug = jnp.concatenate([ks_ref[rows, :], oh_ref[rows, :]], axis=1)
        s = _dot_nt(q_aug, k_aug) * scale
        return s + bt_ref[qi - ki].reshape(hp * tq, tq), vs_ref[rows, :]

    def slc_body(ki, carry):
        s, v = slc_tile(ki)
        return _flash_step(s, v, *carry)

    carry = lax.fori_loop(0, qi, slc_body, _flash_init(hp * tq, HEAD_DIM))
    s, v = slc_tile(qi)
    s = jnp.where(jj <= ii, s, NEG)
    m_s, l_s, acc_s = _flash_step(s, v, *carry)
    o_slc = acc_s / l_s

    def win_tile(delta):
        kt = jnp.maximum(qi - delta, 0)
        rows = pl.ds(pl.multiple_of(kt * tq, tq), tq)
        s = _dot_nt(qs, kw_ref[rows, :]) * scale + bt_ref[delta].reshape(hp * tq, tq)
        return s, vw_ref[rows, :]

    s, v = win_tile(0)
    carry = _flash_step(jnp.where(jj <= ii, s, NEG), v, *_flash_init(hp * tq, HEAD_DIM))
    s, v = win_tile(1)
    carry = _flash_step(jnp.where(qi >= 1, s, NEG), v, *carry)
    s, v = win_tile(2)
    m_w, l_w, acc_w = _flash_step(jnp.where((jj > ii) & (qi >= 2), s, NEG), v, *carry)
    o_win = acc_w / l_w

    g = _sigmoid(g3_ref[...])
    for h in range(hp):
        r = slice(h * tq, (h + 1) * tq)
        out = (g[:, 0 * hp + h:0 * hp + h + 1] * o_cmp[r]
               + g[:, 1 * hp + h:1 * hp + h + 1] * o_slc[r]
               + g[:, 2 * hp + h:2 * hp + h + 1] * o_win[r])
        o_ref[:, h * HEAD_DIM:(h + 1) * HEAD_DIM] = out


def _nsa_attention(zb, zf, kvc, onehot, bt, bc, bsz, seq, zb_cols, zf_cols):
    tq = NSA_TQ
    nblk = seq // NSA_BLOCK
    nd = seq // tq
    qcol, kvcol = zb_cols["q_a"] // (NSA_HPG * HEAD_DIM), zb_cols["kv"] // HEAD_DIM
    g3col = zf_cols["g3"] // LANES
    kv_spec = lambda kind: pl.BlockSpec((None, seq, HEAD_DIM),
                                        lambda g, b, i, kind=kind: (b, 0, kvcol + 2 * kind + g))
    return pl.pallas_call(
        functools.partial(_nsa_kernel, seq=seq),
        grid=(NSA_GROUPS, bsz, seq // tq),
        in_specs=[
            pl.BlockSpec((None, tq, NSA_HPG * HEAD_DIM), lambda g, b, i: (b, i, qcol + g)),
            kv_spec(0), kv_spec(1), kv_spec(2), kv_spec(3),
            pl.BlockSpec((None, None, nblk, HEAD_DIM), lambda g, b, i: (g, b, 0, 0)),
            pl.BlockSpec((None, None, nblk, HEAD_DIM), lambda g, b, i: (2 + g, b, 0, 0)),
            pl.BlockSpec((None, tq, LANES), lambda g, b, i: (b, i, g3col + g)),
            pl.BlockSpec((seq, LANES), lambda g, b, i: (0, 0)),
            pl.BlockSpec((None, nd, NSA_HPG, tq, tq), lambda g, b, i: (g, 0, 0, 0, 0),
                         pipeline_mode=pl.Buffered(1)),
            pl.BlockSpec((None, NSA_HPG, nblk, tq), lambda g, b, i: (g, 0, 0, i)),
        ],
        out_specs=pl.BlockSpec((None, tq, NSA_HPG * HEAD_DIM), lambda g, b, i: (b, i, g)),
        out_shape=jax.ShapeDtypeStruct((bsz, seq, NSA_HEADS * HEAD_DIM), _F32),
        compiler_params=_cparams(("arbitrary", "arbitrary", "arbitrary")),
        name="nsa_attention",
    )(zb, zb, zb, zb, zb, kvc, kvc, zf, onehot, bt, bc)


def _band_kernel(q_ref, k_ref, v_ref, b_ref, o_ref, lse_ref):
    tb = BAND_T
    qi = pl.program_id(2)
    scale = HEAD_DIM ** -0.5
    rows_d = pl.ds(pl.multiple_of(qi * tb, tb), tb)
    rows_p = pl.ds(pl.multiple_of(jnp.maximum(qi - 1, 0) * tb, tb), tb)
    ii = lax.broadcasted_iota(jnp.int32, (tb, tb), 0)
    jj = lax.broadcasted_iota(jnp.int32, (tb, tb), 1)
    mask_d = jj <= ii
    mask_p = (jj >= ii) & (qi >= 1)
    for h in range(DIL_HPG):
        cols = slice(h * HEAD_DIM, (h + 1) * HEAD_DIM)
        q = q_ref[:, cols]
        s_d = jnp.where(mask_d, _dot_nt(q, k_ref[rows_d, cols]) * scale + b_ref[h, 0], NEG)
        s_p = jnp.where(mask_p, _dot_nt(q, k_ref[rows_p, cols]) * scale + b_ref[h, 1], NEG)
        m = jnp.maximum(jnp.max(s_d, axis=1, keepdims=True), jnp.max(s_p, axis=1, keepdims=True))
        p_d = jnp.exp(s_d - m)
        p_p = jnp.exp(s_p - m)
        l = jnp.sum(p_d, axis=1, keepdims=True) + jnp.sum(p_p, axis=1, keepdims=True)
        acc = _dot(p_d.astype(_BF), v_ref[rows_d, cols]) + _dot(p_p.astype(_BF), v_ref[rows_p, cols])
        o_ref[:, cols] = acc / l
        lse_ref[:, cols] = jnp.broadcast_to(m + jnp.log(l), (tb, HEAD_DIM))


def _band_attention(zb, btab, grp, dil, bsz, seq, zb_cols):
    tb = BAND_T
    cb = zb.shape[-1]
    sub = seq // dil
    width = DIL_HPG * HEAD_DIM
    zv = zb.reshape(bsz, sub, dil * cb)
    per = cb // width
    qc = zb_cols["q_b"] // width + grp
    kc = zb_cols["k_b"] // width + grp
    vc = zb_cols["v_b"] // width + grp
    out_shape = jax.ShapeDtypeStruct((bsz, sub, dil * width), _F32)
    o, lse = pl.pallas_call(
        _band_kernel,
        grid=(bsz, dil, sub // tb),
        in_specs=[
            pl.BlockSpec((None, tb, width), lambda b, r, i: (b, i, r * per + qc)),
            pl.BlockSpec((None, sub, width), lambda b, r, i: (b, 0, r * per + kc)),
            pl.BlockSpec((None, sub, width), lambda b, r, i: (b, 0, r * per + vc)),
            pl.BlockSpec((None, DIL_HPG, 2, tb, tb), lambda b, r, i: (grp, 0, 0, 0, 0)),
        ],
        out_specs=[pl.BlockSpec((None, tb, width), lambda b, r, i: (b, i, r)),
                   pl.BlockSpec((None, tb, width), lambda b, r, i: (b, i, r))],
        out_shape=[out_shape, out_shape],
        compiler_params=_cparams(("parallel", "parallel", "arbitrary")),
        name="dilated_band_%d" % grp,
    )(zv, zv, zv, btab)
    return o.reshape(bsz * seq, width), lse.reshape(bsz * seq, width)


def _mla_kernel(q_ref, kn_ref, kp_ref, v_ref, o_ref):
    tq = MLA_TQ
    qi = pl.program_id(2)
    scale = (MLA_NOPE + MLA_ROPE) ** -0.5
    q = q_ref[...]

    def tile(ki):
        rows = pl.ds(pl.multiple_of(ki * tq, tq), tq)
        k = jnp.concatenate([kn_ref[rows, :], kp_ref[rows, :]], axis=1)
        return _dot_nt(q, k) * scale, v_ref[rows, :]

    def body(ki, carry):
        s, v = tile(ki)
        return _flash_step(s, v, *carry)

    carry = lax.fori_loop(0, qi, body, _flash_init(tq, MLA_V))
    s, v = tile(qi)
    ii = lax.broadcasted_iota(jnp.int32, (tq, tq), 0)
    jj = lax.broadcasted_iota(jnp.int32, (tq, tq), 1)
    m, l, acc = _flash_step(jnp.where(jj <= ii, s, NEG), v, *carry)
    o_ref[...] = acc / l


def _mla_attention(q, kv, kpe, bsz, seq):
    tq = MLA_TQ
    return pl.pallas_call(
        _mla_kernel,
        grid=(bsz, MLA_HEADS, seq // tq),
        in_specs=[
            pl.BlockSpec((None, tq, 2 * LANES), lambda b, h, i: (b, i, h)),
            pl.BlockSpec((None, seq, MLA_NOPE), lambda b, h, i: (b, 0, h)),
            pl.BlockSpec((None, seq, LANES), lambda b, h, i: (b, 0, 0)),
            pl.BlockSpec((None, seq, MLA_V), lambda b, h, i: (b, 0, MLA_HEADS + h)),
        ],
        out_specs=pl.BlockSpec((None, tq, MLA_V), lambda b, h, i: (b, i, h)),
        out_shape=jax.ShapeDtypeStruct((bsz, seq, MLA_HEADS * MLA_V), _F32),
        compiler_params=_cparams(("parallel", "parallel", "arbitrary")),
        name="mla_attention",
    )(q, kv, kpe, kv)


def _residual_norm_store(x_new, nw_ref, x_out_ref, h_out_ref):
    x_out_ref[...] = x_new
    y = x_new * lax.rsqrt(jnp.mean(x_new * x_new, axis=-1, keepdims=True) + EPS)
    h_out_ref[...] = (y * nw_ref[...]).astype(h_out_ref.dtype)


def _out_even_kernel(x_ref, oa_ref, ga_ref, gb_ref, o0_ref, o1_ref, o2_ref, l0_ref, l1_ref, l2_ref,
                     wa_ref, wb_ref, nw_ref, x_out_ref, h_out_ref):
    mixed_a = (oa_ref[...] * _silu(ga_ref[...])).astype(_BF)
    l0, l1, l2 = l0_ref[...], l1_ref[...], l2_ref[...]
    m = jnp.maximum(jnp.maximum(l0, l1), l2)
    e0, e1, e2 = jnp.exp(l0 - m), jnp.exp(l1 - m), jnp.exp(l2 - m)
    o_b = (e0 * o0_ref[...] + e1 * o1_ref[...] + e2 * o2_ref[...]) / (e0 + e1 + e2)
    mixed_b = (o_b * _silu(gb_ref[...])).astype(_BF)
    x_new = x_ref[...] + _dot(mixed_a, wa_ref[...]) + _dot(mixed_b, wb_ref[...])
    _residual_norm_store(x_new, nw_ref, x_out_ref, h_out_ref)


def _out_even(x2d, o_a, zf, band, w_out, nw_next, zf_cols, h_dtype):
    m, d = x2d.shape
    tm = 256
    wa = A_GATE
    wbd = B_GATE
    row = lambda width, cb=0: pl.BlockSpec((tm, width), lambda i, cb=cb: (i, cb))
    full = lambda r, c: pl.BlockSpec((r, c), lambda i: (0, 0))
    (o0, l0), (o1, l1), (o2, l2) = band
    return pl.pallas_call(
        _out_even_kernel,
        grid=(m // tm,),
        in_specs=[row(d), row(wa), row(wa, zf_cols["gate_a"] // wa), row(wbd, zf_cols["gate_b"] // wbd),
                  row(wbd), row(wbd), row(wbd), row(wbd), row(wbd), row(wbd),
                  full(wa, d), full(wbd, d), full(1, d)],
        out_specs=[row(d), row(d)],
        out_shape=[jax.ShapeDtypeStruct((m, d), _F32), jax.ShapeDtypeStruct((m, d), h_dtype)],
        compiler_params=_cparams(("parallel",)),
        name="out_proj_even",
    )(x2d, o_a, zf, zf, o0, o1, o2, l0, l1, l2, w_out[:wa], w_out[wa:], nw_next.reshape(1, d).astype(_F32))


def _out_odd_kernel(x_ref, oc_ref, gc_ref, w_ref, nw_ref, x_out_ref, h_out_ref):
    mixed = (oc_ref[...] * _silu(gc_ref[...])).astype(_BF)
    x_new = x_ref[...] + _dot(mixed, w_ref[...])
    _residual_norm_store(x_new, nw_ref, x_out_ref, h_out_ref)


def _out_odd(x2d, o_c, z, w_out, nw_next, gate_col_block, h_dtype):
    m, d = x2d.shape
    tm = 256
    width = MLA_HEADS * MLA_V
    row = lambda w, cb=0: pl.BlockSpec((tm, w), lambda i, cb=cb: (i, cb))
    return pl.pallas_call(
        _out_odd_kernel,
        grid=(m // tm,),
        in_specs=[row(d), row(width), row(width, gate_col_block),
                  pl.BlockSpec((width, d), lambda i: (0, 0)), pl.BlockSpec((1, d), lambda i: (0, 0))],
        out_specs=[row(d), row(d)],
        out_shape=[jax.ShapeDtypeStruct((m, d), _F32), jax.ShapeDtypeStruct((m, d), h_dtype)],
        compiler_params=_cparams(("parallel",)),
        name="out_proj_odd",
    )(x2d, o_c, z, w_out, nw_next.reshape(1, d).astype(_F32))


def _bucket_of_distance(n):
    d = np.arange(n)
    max_exact = NUM_BUCKETS // 2
    df = np.maximum(d, 1).astype(np.float32)
    large = max_exact + (np.log(df / max_exact) / math.log(MAX_DISTANCE / max_exact)
                         * (NUM_BUCKETS - max_exact)).astype(np.int32)
    large = np.minimum(large, NUM_BUCKETS - 1)
    return np.where(d < max_exact, d, large).astype(np.int32)


def _toeplitz_tiles(v, n, nd):
    hh = v.shape[0]
    vp = jnp.concatenate([jnp.zeros((hh, n), v.dtype), v, jnp.zeros((hh, n), v.dtype)], axis=1)
    idx = (n - np.arange(2 * n)) % (2 * n)
    tiles = []
    for dl in range(nd):
        w = vp[:, dl * n:dl * n + 2 * n]
        c = w[:, idx]
        flat = jnp.tile(c, (1, n))[:, :n * (2 * n - 1)]
        tiles.append(flat.reshape(hh, n, 2 * n - 1)[:, :, :n])
    return jnp.stack(tiles, axis=0)


def _bias_tables(rel_bias, seq):
    bd = rel_bias.astype(_F32)[_bucket_of_distance(seq)].T
    bd_nsa = bd[:NSA_HEADS]
    nd = seq // NSA_TQ
    bt = _toeplitz_tiles(bd_nsa, NSA_TQ, nd)
    bt = bt.reshape(nd, NSA_GROUPS, NSA_HPG, NSA_TQ, NSA_TQ).transpose(1, 0, 2, 3, 4)
    nblk = seq // NSA_BLOCK
    off = NSA_BLOCK * (nblk - 1) + NSA_BLOCK - 1
    bdp = jnp.concatenate([jnp.zeros((NSA_HEADS, off), _F32), bd_nsa], axis=1)
    bc = jnp.stack([bdp[:, off - (NSA_BLOCK * n + NSA_BLOCK - 1):off - (NSA_BLOCK * n + NSA_BLOCK - 1) + seq]
                    for n in range(nblk)], axis=1)
    bc = bc.reshape(NSA_GROUPS, NSA_HPG, nblk, seq)
    btabs = []
    for g, (w, d) in enumerate(DIL_PAIRS):
        heads = bd[NSA_HEADS + g * DIL_HPG:NSA_HEADS + (g + 1) * DIL_HPG]
        m = w // d + 1
        offs = np.minimum(np.arange(2 * BAND_T) * d, seq - 1)
        vals = jnp.where(jnp.asarray(np.arange(2 * BAND_T) < m)[None, :], heads[:, offs], 0.0)
        btabs.append(_toeplitz_tiles(vals, BAND_T, 2).transpose(1, 0, 2, 3))
    return bt, bc, jnp.stack(btabs, axis=0)


def _rope_tables(seq):
    half = MLA_ROPE // 2
    freqs = 1.0 / (ROPE_THETA ** (jnp.arange(half, dtype=_F32) / half))
    ang = jnp.arange(seq).astype(_F32)[:, None] * freqs[None, :]
    zeros = jnp.zeros((seq, LANES - MLA_ROPE), _F32)
    cos = jnp.concatenate([jnp.cos(ang), jnp.cos(ang), zeros], axis=1)
    sin = jnp.concatenate([jnp.sin(ang), jnp.sin(ang), zeros], axis=1)
    return cos, sin


def _rot_cols(w):
    half = w.shape[-1] // 2
    return jnp.concatenate([-w[..., half:], w[..., :half]], axis=-1)


ZB_COLS = {"q_a": 0, "kv": A_Q, "q_b": A_Q + 4 * NSA_GROUPS * HEAD_DIM,
           "k_b": A_Q + 4 * NSA_GROUPS * HEAD_DIM + DIL_HEADS * HEAD_DIM,
           "v_b": A_Q + 4 * NSA_GROUPS * HEAD_DIM + 2 * DIL_HEADS * HEAD_DIM}
ZF_COLS = {"gate_a": 0, "gate_b": A_GATE, "kcvc": A_GATE + B_GATE, "g3": A_GATE + B_GATE + 4 * HEAD_DIM}


def _even_weights(w_in):
    o = 0
    q_a = w_in[:, o:o + A_Q]; o += A_Q
    kv_a = w_in[:, o:o + A_KV]; o += A_KV
    g3 = w_in[:, o:o + A_G3]; o += A_G3
    gate_a = w_in[:, o:o + A_GATE]; o += A_GATE
    qkv_b = w_in[:, o:o + B_QKV]; o += B_QKV
    gate_b = w_in[:, o:o + B_GATE]
    ncv = 2 * NSA_GROUPS * HEAD_DIM
    wb = jnp.concatenate([q_a, kv_a[:, ncv:], qkv_b], axis=1).astype(_BF)
    g3_blocks = []
    for g in range(NSA_GROUPS):
        cols = [(g * NSA_HPG + h) * 3 + j for j in range(3) for h in range(NSA_HPG)]
        blk = g3[:, np.asarray(cols)]
        g3_blocks.append(jnp.pad(blk, ((0, 0), (0, LANES - len(cols)))))
    wf = jnp.concatenate([gate_a, gate_b, kv_a[:, :ncv]] + g3_blocks, axis=1).astype(_BF)
    return wb, wf


def _even_layer(x2d, h, w_in, cmp_pos, cmp_w1, cmp_w2, w_out, nw_next, tables, bsz, seq, h_dtype):
    bt, bc, btabs, onehot = tables
    wb, wf = _even_weights(w_in)
    zb = _matmul(h, wb, _BF)
    zf = _matmul(h, wf, _F32)
    nblk = seq // NSA_BLOCK
    kcvc = zf[:, ZF_COLS["kcvc"]:ZF_COLS["kcvc"] + 4 * HEAD_DIM]
    kcvc_t = kcvc.reshape(bsz * nblk, NSA_BLOCK, 4, HEAD_DIM).transpose(2, 1, 0, 3)
    kvc = _compress(kcvc_t, cmp_pos.astype(_F32), cmp_w1.astype(_BF), cmp_w2.astype(_BF))
    kvc = kvc.reshape(4, bsz, nblk, HEAD_DIM)
    zb3 = zb.reshape(bsz, seq, zb.shape[-1])
    zf3 = zf.reshape(bsz, seq, zf.shape[-1])
    o_a = _nsa_attention(zb3, zf3, kvc, onehot, bt, bc, bsz, seq, ZB_COLS, ZF_COLS)
    band = [_band_attention(zb3, btabs, g, d, bsz, seq, ZB_COLS) for g, (_, d) in enumerate(DIL_PAIRS)]
    return _out_even(x2d, o_a.reshape(bsz * seq, -1), zf, band, w_out.astype(_BF), nw_next, ZF_COLS, h_dtype)


def _odd_layer(x2d, h, w_in, q_norm, w_qb, kv_norm, w_kvb, w_out, nw_next, rope, bsz, seq, h_dtype):
    cos, sin = rope
    o = 0
    w_cq = w_in[:, o:o + MLA_Q_RANK]; o += MLA_Q_RANK
    w_ckv = w_in[:, o:o + MLA_KV_RANK]; o += MLA_KV_RANK
    w_kpe = w_in[:, o:o + MLA_ROPE]; o += MLA_ROPE
    w_gate = w_in[:, o:]
    w1 = jnp.concatenate([w_cq, w_ckv, w_gate, w_kpe, _rot_cols(w_kpe)], axis=1).astype(_BF)
    z = _matmul(h, w1, _F32)
    gate_width = MLA_HEADS * MLA_V
    cqn = _rmsnorm(z, q_norm, MLA_Q_RANK, 0, _BF)
    ckvn = _rmsnorm(z, kv_norm, MLA_KV_RANK, MLA_Q_RANK // MLA_KV_RANK, _BF)
    kpe = _rope_cols(z, (MLA_Q_RANK + MLA_KV_RANK + gate_width) // LANES, cos, sin, seq)
    wq = w_qb.reshape(MLA_Q_RANK, MLA_HEADS, MLA_NOPE + MLA_ROPE)
    wq_pe = wq[:, :, MLA_NOPE:]
    wq = jnp.concatenate([wq[:, :, :MLA_NOPE], wq_pe, _rot_cols(wq_pe)], axis=-1)
    q = _matmul_rope(cqn, wq.reshape(MLA_Q_RANK, MLA_HEADS * 2 * LANES).astype(_BF), cos, sin, seq)
    wkv = w_kvb.reshape(MLA_KV_RANK, MLA_HEADS, MLA_NOPE + MLA_V)
    wkv = jnp.concatenate([wkv[:, :, :MLA_NOPE].reshape(MLA_KV_RANK, -1),
                           wkv[:, :, MLA_NOPE:].reshape(MLA_KV_RANK, -1)], axis=1).astype(_BF)
    kv = _matmul(ckvn, wkv, _BF)
    o_c = _mla_attention(q.reshape(bsz, seq, -1), kv.reshape(bsz, seq, -1), kpe.reshape(bsz, seq, LANES),
                         bsz, seq)
    gate_block = (MLA_Q_RANK + MLA_KV_RANK) // gate_width
    return _out_odd(x2d, o_c.reshape(bsz * seq, -1), z, w_out.astype(_BF), nw_next, gate_block, h_dtype)


def kernel(x, rel_bias, norm_w, final_norm_w, ev_w_in, nsa_cmp_pos, nsa_cmp_w1, nsa_cmp_w2, ev_w_out,
           od_w_in, mla_q_norm, mla_w_qb, mla_kv_norm, mla_w_kvb, od_w_out):
    bsz, seq, d = x.shape
    depth = norm_w.shape[0]
    assert seq % MLA_TQ == 0 and seq % (BAND_T * DIL_PAIRS[-1][1]) == 0 and seq // NSA_BLOCK <= LANES
    bt, bc, btabs = _bias_tables(rel_bias, seq)
    blk_id = np.arange(seq)[:, None] // NSA_BLOCK
    onehot = jnp.asarray(blk_id == np.arange(LANES)[None, :], dtype=_BF)
    tables = (bt, bc, btabs, onehot)
    rope = _rope_tables(seq)
    x2d = x.reshape(bsz * seq, d).astype(_F32)
    h = _rmsnorm(x2d, norm_w[0], d, 0, _BF)
    for l in range(depth):
        last = l == depth - 1
        nw_next = final_norm_w if last else norm_w[l + 1]
        h_dtype = _F32 if last else _BF
        i = l // 2
        if l % 2 == 0:
            x2d, h = _even_layer(x2d, h, ev_w_in[i], nsa_cmp_pos[i], nsa_cmp_w1[i], nsa_cmp_w2[i],
                                 ev_w_out[i], nw_next, tables, bsz, seq, h_dtype)
        else:
            x2d, h = _odd_layer(x2d, h, od_w_in[i], mla_q_norm[i], mla_w_qb[i], mla_kv_norm[i],
                                mla_w_kvb[i], od_w_out[i], nw_next, rope, bsz, seq, h_dtype)
    return h.reshape(bsz, seq, d)
```

```python
import functools
import math

import numpy as np
import jax
import jax.numpy as jnp
from jax import lax
from jax.experimental import pallas as pl
from jax.experimental.pallas import tpu as pltpu

HEAD_DIM = 128
EPS = 1e-6
NEG = -1e30
NUM_BUCKETS = 32
MAX_DISTANCE = 2048
NSA_HEADS = 8
NSA_GROUPS = 2
NSA_HPG = 4
NSA_BLOCK = 64
NSA_TOPK = 16
NSA_WINDOW = 512
DIL_PAIRS = ((128, 1), (512, 4), (2048, 16))
DIL_GROUPS = 3
DIL_HPG = 4
DIL_HEADS = 12
MLA_HEADS = 16
MLA_Q_RANK = 1536
MLA_KV_RANK = 512
MLA_NOPE = 128
MLA_ROPE = 64
MLA_V = 128
ROPE_THETA = 10000.0

A_Q = NSA_HEADS * HEAD_DIM
A_KV = 6 * NSA_GROUPS * HEAD_DIM
A_G3 = 3 * NSA_HEADS
A_GATE = NSA_HEADS * HEAD_DIM
B_QKV = 3 * DIL_HEADS * HEAD_DIM
B_GATE = DIL_HPG * HEAD_DIM

LANES = 128
VMEM_LIMIT = 48 * 1024 * 1024

NSA_TQ = 256
BAND_T = 128
MLA_TQ = 512
LOG2E = math.log2(math.e)

_BF = jnp.bfloat16
_F32 = jnp.float32


def _cparams(sem):
    return pltpu.CompilerParams(dimension_semantics=sem, vmem_limit_bytes=VMEM_LIMIT)


def _dot(a, b):
    return jnp.dot(a, b, preferred_element_type=_F32)


def _dot_nt(a, b):
    return lax.dot_general(a, b, (((1,), (1,)), ((), ())), preferred_element_type=_F32)


def _silu(x):
    return x * (1.0 / (1.0 + jnp.exp(-x)))


def _sigmoid(x):
    return 1.0 / (1.0 + jnp.exp(-x))


def _pick_tile(n, candidates):
    for c in candidates:
        if n % c == 0:
            return c
    return n


def _rmsnorm_kernel(x_ref, w_ref, o_ref):
    x = x_ref[...]
    y = x * lax.rsqrt(jnp.mean(x * x, axis=-1, keepdims=True) + EPS)
    o_ref[...] = (y * w_ref[...]).astype(o_ref.dtype)


def _rmsnorm(x2d, w, width, col_block, out_dtype):
    m = x2d.shape[0]
    tm = _pick_tile(m, (512, 256, 128))
    return pl.pallas_call(
        _rmsnorm_kernel,
        grid=(m // tm,),
        in_specs=[pl.BlockSpec((tm, width), lambda i: (i, col_block)),
                  pl.BlockSpec((1, width), lambda i: (0, 0))],
        out_specs=pl.BlockSpec((tm, width), lambda i: (i, 0)),
        out_shape=jax.ShapeDtypeStruct((m, width), out_dtype),
        compiler_params=_cparams(("parallel",)),
        name="rmsnorm",
    )(x2d, w.reshape(1, width).astype(_F32))


def _matmul_kernel(a_ref, w_ref, o_ref):
    o_ref[...] = _dot(a_ref[...], w_ref[...]).astype(o_ref.dtype)


def _matmul(a, w, out_dtype):
    m, k = a.shape
    n = w.shape[1]
    tm = _pick_tile(m, (1024, 512, 256, 128))
    tn = _pick_tile(n, (1024, 768, 512, 384, 256, 128))
    return pl.pallas_call(
        _matmul_kernel,
        grid=(m // tm, n // tn),
        in_specs=[pl.BlockSpec((tm, k), lambda i, j: (i, 0)),
                  pl.BlockSpec((k, tn), lambda i, j: (0, j))],
        out_specs=pl.BlockSpec((tm, tn), lambda i, j: (i, j)),
        out_shape=jax.ShapeDtypeStruct((m, n), out_dtype),
        compiler_params=_cparams(("parallel", "arbitrary")),
        name="matmul",
    )(a, w)


def _matmul_nt_kernel(w_ref, a_ref, o_ref):
    o_ref[...] = _dot_nt(w_ref[...], a_ref[...]).astype(o_ref.dtype)


def _matmul_nt(w_t, a, out_dtype):
    n, k = w_t.shape
    m = a.shape[0]
    tm = _pick_tile(m, (1024, 512, 256, 128))
    tn = _pick_tile(n, (1024, 512, 256, 128))
    return pl.pallas_call(
        _matmul_nt_kernel,
        grid=(m // tm, n // tn),
        in_specs=[pl.BlockSpec((tn, k), lambda i, j: (j, 0)),
                  pl.BlockSpec((tm, k), lambda i, j: (i, 0))],
        out_specs=pl.BlockSpec((tn, tm), lambda i, j: (j, i)),
        out_shape=jax.ShapeDtypeStruct((n, m), out_dtype),
        compiler_params=_cparams(("parallel", "arbitrary")),
        name="matmul_nt",
    )(w_t, a)


def _rope_chunk(chunk, cos, sin):
    return chunk * cos + pltpu.roll(chunk, 64, 1) * sin


def _matmul_rope_kernel(a_ref, w_ref, cos_ref, sin_ref, o_ref, *, heads_per_tile):
    acc = _dot(a_ref[...], w_ref[...])
    cos = cos_ref[...]
    sin = sin_ref[...]
    for h in range(heads_per_tile):
        base = h * 2 * LANES
        o_ref[:, base:base + LANES] = acc[:, base:base + LANES].astype(o_ref.dtype)
        o_ref[:, base + LANES:base + 2 * LANES] = _rope_chunk(
            acc[:, base + LANES:base + 2 * LANES], cos, sin).astype(o_ref.dtype)


def _matmul_rope(a, w, cos, sin, seq):
    m, k = a.shape
    n = w.shape[1]
    tm = _pick_tile(seq, (512, 256, 128))
    tn = 1024
    tpb = seq // tm
    return pl.pallas_call(
        functools.partial(_matmul_rope_kernel, heads_per_tile=tn // (2 * LANES)),
        grid=(m // tm, n // tn),
        in_specs=[pl.BlockSpec((tm, k), lambda i, j: (i, 0)),
                  pl.BlockSpec((k, tn), lambda i, j: (0, j)),
                  pl.BlockSpec((tm, LANES), lambda i, j: (i % tpb, 0)),
                  pl.BlockSpec((tm, LANES), lambda i, j: (i % tpb, 0))],
        out_specs=pl.BlockSpec((tm, tn), lambda i, j: (i, j)),
        out_shape=jax.ShapeDtypeStruct((m, n), _BF),
        compiler_params=_cparams(("parallel", "arbitrary")),
        name="matmul_rope",
    )(a, w, cos, sin)


def _rope_cols_kernel(x_ref, cos_ref, sin_ref, o_ref):
    o_ref[...] = _rope_chunk(x_ref[...], cos_ref[...], sin_ref[...]).astype(o_ref.dtype)


def _rope_cols(z2d, col_block, cos, sin, seq):
    m = z2d.shape[0]
    tm = _pick_tile(seq, (512, 256, 128))
    tpb = seq // tm
    return pl.pallas_call(
        _rope_cols_kernel,
        grid=(m // tm,),
        in_specs=[pl.BlockSpec((tm, LANES), lambda i: (i, col_block)),
                  pl.BlockSpec((tm, LANES), lambda i: (i % tpb, 0)),
                  pl.BlockSpec((tm, LANES), lambda i: (i % tpb, 0))],
        out_specs=pl.BlockSpec((tm, LANES), lambda i: (i, 0)),
        out_shape=jax.ShapeDtypeStruct((m, LANES), _BF),
        compiler_params=_cparams(("parallel",)),
        name="rope_kpe",
    )(z2d, cos, sin)


def _compress_kernel(blk_ref, pos_ref, w1_ref, w2_ref, o_ref, acc_ref, *, lt):
    li = pl.program_id(1)

    @pl.when(li == 0)
    def _():
        acc_ref[...] = jnp.zeros_like(acc_ref)

    acc = acc_ref[...]
    for l in range(lt):
        a = (blk_ref[l] + pos_ref[l:l + 1, :]).astype(_BF)
        acc = acc + _dot(a, w1_ref[l])
    acc_ref[...] = acc

    @pl.when(li == pl.num_programs(1) - 1)
    def _():
        hid = _silu(acc_ref[...]).astype(_BF)
        o_ref[...] = _dot(hid, w2_ref[...])


def _compress(kcvc_t, pos, w1, w2):
    _, L, R, _ = kcvc_t.shape
    lt = 16
    return pl.pallas_call(
        functools.partial(_compress_kernel, lt=lt),
        grid=(4, L // lt),
        in_specs=[pl.BlockSpec((None, lt, R, HEAD_DIM), lambda c, l: (c, l, 0, 0)),
                  pl.BlockSpec((None, lt, HEAD_DIM), lambda c, l: (c // 2, l, 0)),
                  pl.BlockSpec((None, lt, HEAD_DIM, HEAD_DIM), lambda c, l: (c // 2, l, 0, 0)),
                  pl.BlockSpec((None, HEAD_DIM, HEAD_DIM), lambda c, l: (c // 2, 0, 0))],
        out_specs=pl.BlockSpec((None, R, HEAD_DIM), lambda c, l: (c, 0, 0)),
        out_shape=jax.ShapeDtypeStruct((4, R, HEAD_DIM), _F32),
        scratch_shapes=[pltpu.VMEM((R, HEAD_DIM), _F32)],
        compiler_params=_cparams(("parallel", "arbitrary")),
        name="nsa_compress",
    )(kcvc_t, pos, w1.reshape(2, L, HEAD_DIM, HEAD_DIM), w2)


def _flash_reset(m_ref, l_ref, acc_ref):
    m_ref[...] = jnp.full(m_ref.shape, -jnp.inf, _F32)
    l_ref[...] = jnp.zeros(l_ref.shape, _F32)
    acc_ref[...] = jnp.zeros(acc_ref.shape, _F32)


def _flash_update(s, vt, c, m_ref, l_ref, acc_ref):
    m_prev = m_ref[...]
    m_new = jnp.maximum(m_prev, jnp.max(s, axis=0, keepdims=True))
    if c is None:
        alpha = jnp.exp2(m_prev - m_new)
        p = jnp.exp2(s - m_new)
    else:
        alpha = jnp.exp2((m_prev - m_new) * c)
        p = jnp.exp2((s - m_new) * c)
    l_ref[...] = alpha * l_ref[...] + jnp.sum(p, axis=0, keepdims=True)
    acc_ref[...] = alpha * acc_ref[...] + _dot(vt, p.astype(_BF))
    m_ref[...] = m_new


def _causal_flash(n_off, scores_into, consume, sa_ref, sb_ref):
    scores_into(sa_ref, 0)

    def pair(p, carry):
        k0 = 2 * p
        scores_into(sb_ref, k0 + 1)
        consume(sa_ref, k0, False)
        scores_into(sa_ref, k0 + 2)
        consume(sb_ref, k0 + 1, False)
        return carry

    lax.fori_loop(0, n_off // 2, pair, 0)

    @pl.when(n_off % 2 == 1)
    def _():
        scores_into(sb_ref, n_off)
        consume(sa_ref, n_off - 1, False)
        consume(sb_ref, n_off, True)

    @pl.when(n_off % 2 == 0)
    def _():
        consume(sa_ref, n_off, True)


def _nsa_kernel(q_ref, ks_ref, kw_ref, vst_ref, vwt_ref, kc_ref, vc_ref, g3_ref, oh_ref, bt_ref, bc_ref,
                o_ref, sa_ref, sb_ref, m_ref, l_ref, acc_s_ref, acc_w_ref, *, seq):
    tq = NSA_TQ
    hp = NSA_HPG
    n = hp * tq
    nblk = seq // NSA_BLOCK
    qi = pl.program_id(2)
    scale = HEAD_DIM ** -0.5
    c = scale * LOG2E
    t0 = qi * tq

    qs = jnp.concatenate([q_ref[:, h * HEAD_DIM:(h + 1) * HEAD_DIM] for h in range(hp)], axis=0)

    kc = kc_ref[...].astype(_BF)
    s_c = _dot_nt(kc, qs) * scale
    s_c = s_c + jnp.concatenate([bc_ref[h] for h in range(hp)], axis=1)
    n_col = lax.broadcasted_iota(jnp.int32, (nblk, tq), 0)
    t_row = t0 + lax.broadcasted_iota(jnp.int32, (nblk, tq), 1)
    n_all = lax.broadcasted_iota(jnp.int32, (nblk, n), 0)
    t_all = t0 + (lax.broadcasted_iota(jnp.int32, (nblk, n), 1) & (tq - 1))
    valid = t_all >= n_all * NSA_BLOCK + (NSA_BLOCK - 1)
    s_c = jnp.where(valid, s_c, NEG)
    m_c = jnp.max(s_c, axis=0, keepdims=True)
    p_c = jnp.where(valid, jnp.exp(s_c - m_c), 0.0)
    l_c = jnp.sum(p_c, axis=0, keepdims=True)
    p_c = p_c / jnp.where(l_c > 0.0, l_c, 1.0)
    imp = p_c[:, 0:tq]
    for h in range(1, hp):
        imp = imp + p_c[:, h * tq:(h + 1) * tq]

    pad_rows = LANES - nblk
    vc_pad = jnp.concatenate([vc_ref[...], jnp.zeros((pad_rows, HEAD_DIM), _F32)], axis=0)
    vc_t = jnp.transpose(vc_pad).astype(_BF)
    p_pad = jnp.concatenate([p_c, jnp.zeros((pad_rows, n), _F32)], axis=0).astype(_BF)
    o_cmp = _dot(vc_t, p_pad)

    cur = jnp.right_shift(t_row, int(math.log2(NSA_BLOCK)))
    forced = (n_col == 0) | (n_col == cur) | (n_col == cur - 1)
    impv = jnp.where(forced, jnp.inf, jnp.where(n_col > cur, -jnp.inf, imp))
    rank = jnp.zeros((nblk, tq), _F32)
    for i in range(nblk):
        row = impv[i:i + 1, :]
        ge = jnp.where(row >= impv, 1.0, 0.0)
        gt = jnp.where(row > impv, 1.0, 0.0)
        rank = rank + jnp.where(n_col > i, ge, gt)
    sel = (rank < float(min(NSA_TOPK, nblk))) & (n_col <= cur)
    pen = jnp.where(sel, 0.0, NEG)
    pen = jnp.concatenate([pen, jnp.full((pad_rows, tq), NEG, _F32)], axis=0)
    pen_t = jnp.transpose(pen).astype(_BF)
    q_aug = jnp.concatenate([qs, jnp.concatenate([pen_t] * hp, axis=0)], axis=1)

    jj = lax.broadcasted_iota(jnp.int32, (tq, n), 0)
    ii = lax.broadcasted_iota(jnp.int32, (tq, n), 1) & (tq - 1)

    def rows_of(kt):
        return pl.ds(pl.multiple_of(kt * tq, tq), tq)

    def slc_scores(dst_ref, ki):
        rows = rows_of(ki)
        k_aug = jnp.concatenate([ks_ref[rows, :], oh_ref[rows, :]], axis=1)
        dst_ref[...] = _dot_nt(k_aug, q_aug) * c + bt_ref[qi - ki]

    def slc_consume(src_ref, ki, diagonal):
        s = src_ref[...]
        if diagonal:
            s = jnp.where(jj <= ii, s, NEG)
        _flash_update(s, vst_ref[:, rows_of(ki)], None, m_ref, l_ref, acc_s_ref)

    _flash_reset(m_ref, l_ref, acc_s_ref)
    _causal_flash(qi, slc_scores, slc_consume, sa_ref, sb_ref)
    o_slc = acc_s_ref[...] / l_ref[...]

    def win_scores(dst_ref, delta):
        dst_ref[...] = _dot_nt(kw_ref[rows_of(jnp.maximum(qi - delta, 0)), :], qs) * c + bt_ref[delta]

    def win_consume(src_ref, delta, keep):
        s = jnp.where(keep, src_ref[...], NEG)
        _flash_update(s, vwt_ref[:, rows_of(jnp.maximum(qi - delta, 0))], None, m_ref, l_ref, acc_w_ref)

    _flash_reset(m_ref, l_ref, acc_w_ref)
    win_scores(sa_ref, 0)
    win_scores(sb_ref, 1)
    win_consume(sa_ref, 0, jj <= ii)
    win_scores(sa_ref, 2)
    win_consume(sb_ref, 1, qi >= 1)
    win_consume(sa_ref, 2, (jj > ii) & (qi >= 2))
    o_win = acc_w_ref[...] / l_ref[...]

    g_t = jnp.transpose(_sigmoid(g3_ref[...]))
    for h in range(hp):
        cols = slice(h * tq, (h + 1) * tq)
        out_t = (g_t[0 * hp + h:0 * hp + h + 1, :] * o_cmp[:, cols]
                 + g_t[1 * hp + h:1 * hp + h + 1, :] * o_slc[:, cols]
                 + g_t[2 * hp + h:2 * hp + h + 1, :] * o_win[:, cols])
        o_ref[:, h * HEAD_DIM:(h + 1) * HEAD_DIM] = jnp.transpose(out_t)


def _nsa_attention(zb, vt, zf, kvc, onehot, bt, bc, bsz, seq):
    tq = NSA_TQ
    n = NSA_HPG * tq
    nblk = seq // NSA_BLOCK
    nd = seq // tq
    kcol = A_Q // HEAD_DIM
    g3col = ZF_COLS["g3"] // LANES
    return pl.pallas_call(
        functools.partial(_nsa_kernel, seq=seq),
        grid=(NSA_GROUPS, bsz, seq // tq),
        in_specs=[
            pl.BlockSpec((None, tq, NSA_HPG * HEAD_DIM), lambda g, b, i: (b, i, g)),
            pl.BlockSpec((None, seq, HEAD_DIM), lambda g, b, i: (b, 0, kcol + g)),
            pl.BlockSpec((None, seq, HEAD_DIM), lambda g, b, i: (b, 0, kcol + NSA_GROUPS + g)),
            pl.BlockSpec((HEAD_DIM, seq), lambda g, b, i: (g, b)),
            pl.BlockSpec((HEAD_DIM, seq), lambda g, b, i: (NSA_GROUPS + g, b)),
            pl.BlockSpec((None, None, nblk, HEAD_DIM), lambda g, b, i: (g, b, 0, 0)),
            pl.BlockSpec((None, None, nblk, HEAD_DIM), lambda g, b, i: (2 + g, b, 0, 0)),
            pl.BlockSpec((None, tq, LANES), lambda g, b, i: (b, i, g3col + g)),
            pl.BlockSpec((seq, LANES), lambda g, b, i: (0, 0)),
            pl.BlockSpec((None, nd, tq, n), lambda g, b, i: (g, 0, 0, 0), pipeline_mode=pl.Buffered(1)),
            pl.BlockSpec((None, NSA_HPG, nblk, tq), lambda g, b, i: (g, 0, 0, i)),
        ],
        out_specs=pl.BlockSpec((None, tq, NSA_HPG * HEAD_DIM), lambda g, b, i: (b, i, g)),
        out_shape=jax.ShapeDtypeStruct((bsz, seq, NSA_HEADS * HEAD_DIM), _F32),
        scratch_shapes=[pltpu.VMEM((tq, n), _F32), pltpu.VMEM((tq, n), _F32),
                        pltpu.VMEM((1, n), _F32), pltpu.VMEM((1, n), _F32),
                        pltpu.VMEM((HEAD_DIM, n), _F32), pltpu.VMEM((HEAD_DIM, n), _F32)],
        compiler_params=_cparams(("arbitrary", "arbitrary", "arbitrary")),
        name="nsa_attention",
    )(zb, zb, zb, vt, vt, kvc, kvc, zf, onehot, bt, bc)


def _band_kernel(q_ref, k_ref, v_ref, b_ref, o_ref, lse_ref):
    tb = BAND_T
    qi = pl.program_id(2)
    scale = HEAD_DIM ** -0.5
    rows_d = pl.ds(pl.multiple_of(qi * tb, tb), tb)
    rows_p = pl.ds(pl.multiple_of(jnp.maximum(qi - 1, 0) * tb, tb), tb)
    ii = lax.broadcasted_iota(jnp.int32, (tb, tb), 0)
    jj = lax.broadcasted_iota(jnp.int32, (tb, tb), 1)
    mask_d = jj <= ii
    mask_p = (jj >= ii) & (qi >= 1)
    for h in range(DIL_HPG):
        cols = slice(h * HEAD_DIM, (h + 1) * HEAD_DIM)
        q = q_ref[:, cols]
        s_d = jnp.where(mask_d, _dot_nt(q, k_ref[rows_d, cols]) * scale + b_ref[h, 0], NEG)
        s_p = jnp.where(mask_p, _dot_nt(q, k_ref[rows_p, cols]) * scale + b_ref[h, 1], NEG)
        m = jnp.maximum(jnp.max(s_d, axis=1, keepdims=True), jnp.max(s_p, axis=1, keepdims=True))
        p_d = jnp.exp(s_d - m)
        p_p = jnp.exp(s_p - m)
        l = jnp.sum(p_d, axis=1, keepdims=True) + jnp.sum(p_p, axis=1, keepdims=True)
        acc = _dot(p_d.astype(_BF), v_ref[rows_d, cols]) + _dot(p_p.astype(_BF), v_ref[rows_p, cols])
        o_ref[:, cols] = acc / l
        lse_ref[:, cols] = jnp.broadcast_to(m + jnp.log(l), (tb, HEAD_DIM))


def _band_attention(zd3, btab, grp, dil, bsz, seq):
    tb = BAND_T
    width = DIL_HPG * HEAD_DIM
    sub = seq // dil
    zg = zd3[:, :, grp * 3 * width:(grp + 1) * 3 * width].reshape(bsz, sub, dil * 3 * width)
    out_shape = jax.ShapeDtypeStruct((bsz, sub, dil * width), _F32)
    o, lse = pl.pallas_call(
        _band_kernel,
        grid=(bsz, dil, sub // tb),
        in_specs=[
            pl.BlockSpec((None, tb, width), lambda b, r, i: (b, i, r * 3)),
            pl.BlockSpec((None, sub, width), lambda b, r, i: (b, 0, r * 3 + 1)),
            pl.BlockSpec((None, sub, width), lambda b, r, i: (b, 0, r * 3 + 2)),
            pl.BlockSpec((None, DIL_HPG, 2, tb, tb), lambda b, r, i: (grp, 0, 0, 0, 0)),
        ],
        out_specs=[pl.BlockSpec((None, tb, width), lambda b, r, i: (b, i, r)),
                   pl.BlockSpec((None, tb, width), lambda b, r, i: (b, i, r))],
        out_shape=[out_shape, out_shape],
        compiler_params=_cparams(("parallel", "parallel", "arbitrary")),
        name="dilated_band_%d" % grp,
    )(zg, zg, zg, btab)
    return o.reshape(bsz * seq, width), lse.reshape(bsz * seq, width)


def _mla_kernel(q_ref, kn_ref, kp_ref, vt_ref, o_ref, sa_ref, sb_ref, m_ref, l_ref, acc_ref):
    tq = MLA_TQ
    qi = pl.program_id(2)
    c = (MLA_NOPE + MLA_ROPE) ** -0.5 * LOG2E
    q = q_ref[...]
    _flash_reset(m_ref, l_ref, acc_ref)
    jj = lax.broadcasted_iota(jnp.int32, (tq, tq), 0)
    ii = lax.broadcasted_iota(jnp.int32, (tq, tq), 1)

    def rows_of(ki):
        return pl.ds(pl.multiple_of(ki * tq, tq), tq)

    def scores_into(dst_ref, ki):
        rows = rows_of(ki)
        k = jnp.concatenate([kn_ref[rows, :], kp_ref[rows, :]], axis=1)
        dst_ref[...] = _dot_nt(k, q)

    def consume(src_ref, ki, diagonal):
        s = src_ref[...]
        if diagonal:
            s = jnp.where(jj <= ii, s, NEG)
        _flash_update(s, vt_ref[:, rows_of(ki)], c, m_ref, l_ref, acc_ref)

    _causal_flash(qi, scores_into, consume, sa_ref, sb_ref)
    o_ref[...] = jnp.transpose(acc_ref[...] / l_ref[...])


def _mla_attention(q, kn, kpe, vt, bsz, seq):
    tq = MLA_TQ
    return pl.pallas_call(
        _mla_kernel,
        grid=(bsz, MLA_HEADS, seq // tq),
        in_specs=[
            pl.BlockSpec((None, tq, 2 * LANES), lambda b, h, i: (b, i, h)),
            pl.BlockSpec((None, seq, MLA_NOPE), lambda b, h, i: (b, 0, h)),
            pl.BlockSpec((None, seq, LANES), lambda b, h, i: (b, 0, 0)),
            pl.BlockSpec((MLA_V, seq), lambda b, h, i: (h, b)),
        ],
        out_specs=pl.BlockSpec((None, tq, MLA_V), lambda b, h, i: (b, i, h)),
        out_shape=jax.ShapeDtypeStruct((bsz, seq, MLA_HEADS * MLA_V), _F32),
        scratch_shapes=[pltpu.VMEM((tq, tq), _F32), pltpu.VMEM((tq, tq), _F32),
                        pltpu.VMEM((1, tq), _F32), pltpu.VMEM((1, tq), _F32), pltpu.VMEM((MLA_V, tq), _F32)],
        compiler_params=_cparams(("parallel", "parallel", "arbitrary")),
        name="mla_attention",
    )(q, kn, kpe, vt)


def _residual_norm_store(x_new, nw_ref, x_out_ref, h_out_ref):
    x_out_ref[...] = x_new
    y = x_new * lax.rsqrt(jnp.mean(x_new * x_new, axis=-1, keepdims=True) + EPS)
    h_out_ref[...] = (y * nw_ref[...]).astype(h_out_ref.dtype)


def _out_even_kernel(x_ref, oa_ref, ga_ref, gb_ref, o0_ref, o1_ref, o2_ref, l0_ref, l1_ref, l2_ref,
                     wa_ref, wb_ref, nw_ref, x_out_ref, h_out_ref):
    mixed_a = (oa_ref[...] * _silu(ga_ref[...])).astype(_BF)
    l0, l1, l2 = l0_ref[...], l1_ref[...], l2_ref[...]
    m = jnp.maximum(jnp.maximum(l0, l1), l2)
    e0, e1, e2 = jnp.exp(l0 - m), jnp.exp(l1 - m), jnp.exp(l2 - m)
    o_b = (e0 * o0_ref[...] + e1 * o1_ref[...] + e2 * o2_ref[...]) / (e0 + e1 + e2)
    mixed_b = (o_b * _silu(gb_ref[...])).astype(_BF)
    x_new = x_ref[...] + _dot(mixed_a, wa_ref[...]) + _dot(mixed_b, wb_ref[...])
    _residual_norm_store(x_new, nw_ref, x_out_ref, h_out_ref)


def _out_even(x2d, o_a, zf, band, w_out, nw_next, zf_cols, h_dtype):
    m, d = x2d.shape
    tm = 256
    wa = A_GATE
    wbd = B_GATE
    row = lambda width, cb=0: pl.BlockSpec((tm, width), lambda i, cb=cb: (i, cb))
    full = lambda r, c: pl.BlockSpec((r, c), lambda i: (0, 0))
    (o0, l0), (o1, l1), (o2, l2) = band
    return pl.pallas_call(
        _out_even_kernel,
        grid=(m // tm,),
        in_specs=[row(d), row(wa), row(wa, zf_cols["gate_a"] // wa), row(wbd, zf_cols["gate_b"] // wbd),
                  row(wbd), row(wbd), row(wbd), row(wbd), row(wbd), row(wbd),
                  full(wa, d), full(wbd, d), full(1, d)],
        out_specs=[row(d), row(d)],
        out_shape=[jax.ShapeDtypeStruct((m, d), _F32), jax.ShapeDtypeStruct((m, d), h_dtype)],
        compiler_params=_cparams(("parallel",)),
        name="out_proj_even",
    )(x2d, o_a, zf, zf, o0, o1, o2, l0, l1, l2, w_out[:wa], w_out[wa:], nw_next.reshape(1, d).astype(_F32))


def _out_odd_kernel(x_ref, oc_ref, gc_ref, w_ref, nw_ref, x_out_ref, h_out_ref):
    mixed = (oc_ref[...] * _silu(gc_ref[...])).astype(_BF)
    x_new = x_ref[...] + _dot(mixed, w_ref[...])
    _residual_norm_store(x_new, nw_ref, x_out_ref, h_out_ref)


def _out_odd(x2d, o_c, z, w_out, nw_next, gate_col_block, h_dtype):
    m, d = x2d.shape
    tm = 256
    width = MLA_HEADS * MLA_V
    row = lambda w, cb=0: pl.BlockSpec((tm, w), lambda i, cb=cb: (i, cb))
    return pl.pallas_call(
        _out_odd_kernel,
        grid=(m // tm,),
        in_specs=[row(d), row(width), row(width, gate_col_block),
                  pl.BlockSpec((width, d), lambda i: (0, 0)), pl.BlockSpec((1, d), lambda i: (0, 0))],
        out_specs=[row(d), row(d)],
        out_shape=[jax.ShapeDtypeStruct((m, d), _F32), jax.ShapeDtypeStruct((m, d), h_dtype)],
        compiler_params=_cparams(("parallel",)),
        name="out_proj_odd",
    )(x2d, o_c, z, w_out, nw_next.reshape(1, d).astype(_F32))


def _bucket_of_distance(n):
    d = np.arange(n)
    max_exact = NUM_BUCKETS // 2
    df = np.maximum(d, 1).astype(np.float32)
    large = max_exact + (np.log(df / max_exact) / math.log(MAX_DISTANCE / max_exact)
                         * (NUM_BUCKETS - max_exact)).astype(np.int32)
    large = np.minimum(large, NUM_BUCKETS - 1)
    return np.where(d < max_exact, d, large).astype(np.int32)


def _toeplitz_tiles(v, n, nd):
    hh = v.shape[0]
    vp = jnp.concatenate([jnp.zeros((hh, n), v.dtype), v, jnp.zeros((hh, n), v.dtype)], axis=1)
    idx = (n - np.arange(2 * n)) % (2 * n)
    tiles = []
    for dl in range(nd):
        w = vp[:, dl * n:dl * n + 2 * n]
        c = w[:, idx]
        flat = jnp.tile(c, (1, n))[:, :n * (2 * n - 1)]
        tiles.append(flat.reshape(hh, n, 2 * n - 1)[:, :, :n])
    return jnp.stack(tiles, axis=0)


def _bias_tables(rel_bias, seq):
    bd = rel_bias.astype(_F32)[_bucket_of_distance(seq)].T
    bd_nsa = bd[:NSA_HEADS]
    nd = seq // NSA_TQ
    bt = _toeplitz_tiles(bd_nsa * LOG2E, NSA_TQ, nd)
    bt = bt.reshape(nd, NSA_GROUPS, NSA_HPG, NSA_TQ, NSA_TQ).transpose(1, 0, 4, 2, 3)
    bt = bt.reshape(NSA_GROUPS, nd, NSA_TQ, NSA_HPG * NSA_TQ)
    nblk = seq // NSA_BLOCK
    off = NSA_BLOCK * (nblk - 1) + NSA_BLOCK - 1
    bdp = jnp.concatenate([jnp.zeros((NSA_HEADS, off), _F32), bd_nsa], axis=1)
    bc = jnp.stack([bdp[:, off - (NSA_BLOCK * n + NSA_BLOCK - 1):off - (NSA_BLOCK * n + NSA_BLOCK - 1) + seq]
                    for n in range(nblk)], axis=1)
    bc = bc.reshape(NSA_GROUPS, NSA_HPG, nblk, seq)
    btabs = []
    for g, (w, d) in enumerate(DIL_PAIRS):
        heads = bd[NSA_HEADS + g * DIL_HPG:NSA_HEADS + (g + 1) * DIL_HPG]
        m = w // d + 1
        offs = np.minimum(np.arange(2 * BAND_T) * d, seq - 1)
        vals = jnp.where(jnp.asarray(np.arange(2 * BAND_T) < m)[None, :], heads[:, offs], 0.0)
        btabs.append(_toeplitz_tiles(vals, BAND_T, 2).transpose(1, 0, 2, 3))
    return bt, bc, jnp.stack(btabs, axis=0)


def _rope_tables(seq):
    half = MLA_ROPE // 2
    freqs = 1.0 / (ROPE_THETA ** (jnp.arange(half, dtype=_F32) / half))
    ang = jnp.arange(seq).astype(_F32)[:, None] * freqs[None, :]
    zeros = jnp.zeros((seq, LANES - MLA_ROPE), _F32)
    cos = jnp.concatenate([jnp.cos(ang), jnp.cos(ang), zeros], axis=1)
    sin = jnp.concatenate([jnp.sin(ang), jnp.sin(ang), zeros], axis=1)
    return cos, sin


def _rot_cols(w):
    half = w.shape[-1] // 2
    return jnp.concatenate([-w[..., half:], w[..., :half]], axis=-1)


ZF_COLS = {"gate_a": 0, "gate_b": A_GATE, "kcvc": A_GATE + B_GATE, "g3": A_GATE + B_GATE + 4 * HEAD_DIM}


def _even_weights(w_in):
    o = 0
    q_a = w_in[:, o:o + A_Q]; o += A_Q
    kv_a = w_in[:, o:o + A_KV]; o += A_KV
    g3 = w_in[:, o:o + A_G3]; o += A_G3
    gate_a = w_in[:, o:o + A_GATE]; o += A_GATE
    qkv_b = w_in[:, o:o + B_QKV]; o += B_QKV
    gate_b = w_in[:, o:o + B_GATE]
    gw = NSA_GROUPS * HEAD_DIM
    kc_vc, ks, vs, kw, vw = (kv_a[:, :2 * gw], kv_a[:, 2 * gw:3 * gw], kv_a[:, 3 * gw:4 * gw],
                             kv_a[:, 4 * gw:5 * gw], kv_a[:, 5 * gw:6 * gw])
    wb = jnp.concatenate([q_a, ks, kw], axis=1).astype(_BF)
    wvt = jnp.concatenate([vs, vw], axis=1).T.astype(_BF)
    dw = DIL_HPG * HEAD_DIM
    nq = DIL_HEADS * HEAD_DIM
    wd = jnp.concatenate([qkv_b[:, part * nq + g * dw:part * nq + (g + 1) * dw]
                          for g in range(DIL_GROUPS) for part in range(3)], axis=1).astype(_BF)
    g3_blocks = []
    for g in range(NSA_GROUPS):
        cols = [(g * NSA_HPG + h) * 3 + j for j in range(3) for h in range(NSA_HPG)]
        blk = g3[:, np.asarray(cols)]
        g3_blocks.append(jnp.pad(blk, ((0, 0), (0, LANES - len(cols)))))
    wf = jnp.concatenate([gate_a, gate_b, kc_vc] + g3_blocks, axis=1).astype(_BF)
    return wb, wvt, wd, wf


def _even_layer(x2d, h, w_in, cmp_pos, cmp_w1, cmp_w2, w_out, nw_next, tables, bsz, seq, h_dtype):
    bt, bc, btabs, onehot = tables
    wb, wvt, wd, wf = _even_weights(w_in)
    zb = _matmul(h, wb, _BF)
    vt = _matmul_nt(wvt, h, _BF)
    zd = _matmul(h, wd, _BF)
    zf = _matmul(h, wf, _F32)
    nblk = seq // NSA_BLOCK
    kcvc = zf[:, ZF_COLS["kcvc"]:ZF_COLS["kcvc"] + 4 * HEAD_DIM]
    kcvc_t = kcvc.reshape(bsz * nblk, NSA_BLOCK, 4, HEAD_DIM).transpose(2, 1, 0, 3)
    kvc = _compress(kcvc_t, cmp_pos.astype(_F32), cmp_w1.astype(_BF), cmp_w2.astype(_BF))
    kvc = kvc.reshape(4, bsz, nblk, HEAD_DIM)
    zf3 = zf.reshape(bsz, seq, zf.shape[-1])
    o_a = _nsa_attention(zb.reshape(bsz, seq, -1), vt, zf3, kvc, onehot, bt, bc, bsz, seq)
    zd3 = zd.reshape(bsz, seq, -1)
    band = [_band_attention(zd3, btabs, g, d, bsz, seq) for g, (_, d) in enumerate(DIL_PAIRS)]
    return _out_even(x2d, o_a.reshape(bsz * seq, -1), zf, band, w_out.astype(_BF), nw_next, ZF_COLS, h_dtype)


def _odd_layer(x2d, h, w_in, q_norm, w_qb, kv_norm, w_kvb, w_out, nw_next, rope, bsz, seq, h_dtype):
    cos, sin = rope
    o = 0
    w_cq = w_in[:, o:o + MLA_Q_RANK]; o += MLA_Q_RANK
    w_ckv = w_in[:, o:o + MLA_KV_RANK]; o += MLA_KV_RANK
    w_kpe = w_in[:, o:o + MLA_ROPE]; o += MLA_ROPE
    w_gate = w_in[:, o:]
    w1 = jnp.concatenate([w_cq, w_ckv, w_gate, w_kpe, _rot_cols(w_kpe)], axis=1).astype(_BF)
    z = _matmul(h, w1, _F32)
    gate_width = MLA_HEADS * MLA_V
    cqn = _rmsnorm(z, q_norm, MLA_Q_RANK, 0, _BF)
    ckvn = _rmsnorm(z, kv_norm, MLA_KV_RANK, MLA_Q_RANK // MLA_KV_RANK, _BF)
    kpe = _rope_cols(z, (MLA_Q_RANK + MLA_KV_RANK + gate_width) // LANES, cos, sin, seq)
    wq = w_qb.reshape(MLA_Q_RANK, MLA_HEADS, MLA_NOPE + MLA_ROPE)
    wq_pe = wq[:, :, MLA_NOPE:]
    wq = jnp.concatenate([wq[:, :, :MLA_NOPE], wq_pe, _rot_cols(wq_pe)], axis=-1)
    q = _matmul_rope(cqn, wq.reshape(MLA_Q_RANK, MLA_HEADS * 2 * LANES).astype(_BF), cos, sin, seq)
    wkv = w_kvb.reshape(MLA_KV_RANK, MLA_HEADS, MLA_NOPE + MLA_V)
    wk = wkv[:, :, :MLA_NOPE].reshape(MLA_KV_RANK, -1).astype(_BF)
    wv_t = wkv[:, :, MLA_NOPE:].reshape(MLA_KV_RANK, -1).T.astype(_BF)
    kn = _matmul(ckvn, wk, _BF)
    vt = _matmul_nt(wv_t, ckvn, _BF)
    o_c = _mla_attention(q.reshape(bsz, seq, -1), kn.reshape(bsz, seq, -1), kpe.reshape(bsz, seq, LANES),
                         vt, bsz, seq)
    gate_block = (MLA_Q_RANK + MLA_KV_RANK) // gate_width
    return _out_odd(x2d, o_c.reshape(bsz * seq, -1), z, w_out.astype(_BF), nw_next, gate_block, h_dtype)


def kernel(x, rel_bias, norm_w, final_norm_w, ev_w_in, nsa_cmp_pos, nsa_cmp_w1, nsa_cmp_w2, ev_w_out,
           od_w_in, mla_q_norm, mla_w_qb, mla_kv_norm, mla_w_kvb, od_w_out):
    bsz, seq, d = x.shape
    depth = norm_w.shape[0]
    assert seq % MLA_TQ == 0 and seq % (BAND_T * DIL_PAIRS[-1][1]) == 0 and seq // NSA_BLOCK <= LANES
    assert NSA_WINDOW == 2 * NSA_TQ
    bt, bc, btabs = _bias_tables(rel_bias, seq)
    blk_id = np.arange(seq)[:, None] // NSA_BLOCK
    onehot = jnp.asarray(blk_id == np.arange(LANES)[None, :], dtype=_BF)
    tables = (bt, bc, btabs, onehot)
    rope = _rope_tables(seq)
    x2d = x.reshape(bsz * seq, d).astype(_F32)
    h = _rmsnorm(x2d, norm_w[0], d, 0, _BF)
    for l in range(depth):
        last = l == depth - 1
        nw_next = final_norm_w if last else norm_w[l + 1]
        h_dtype = _F32 if last else _BF
        i = l // 2
        if l % 2 == 0:
            x2d, h = _even_layer(x2d, h, ev_w_in[i], nsa_cmp_pos[i], nsa_cmp_w1[i], nsa_cmp_w2[i],
                                 ev_w_out[i], nw_next, tables, bsz, seq, h_dtype)
        else:
            x2d, h = _odd_layer(x2d, h, od_w_in[i], mla_q_norm[i], mla_w_qb[i], mla_kv_norm[i],
                                mla_w_kvb[i], od_w_out[i], nw_next, rope, bsz, seq, h_dtype)
    return h.reshape(bsz, seq, d)
```

```python
import functools
import math

import numpy as np
import jax
import jax.numpy as jnp
from jax import lax
from jax.experimental import pallas as pl
from jax.experimental.pallas import tpu as pltpu

HEAD_DIM = 128
EPS = 1e-6
NEG = -1e30
NUM_BUCKETS = 32
MAX_DISTANCE = 2048
NSA_HEADS = 8
NSA_GROUPS = 2
NSA_HPG = 4
NSA_BLOCK = 64
NSA_TOPK = 16
NSA_WINDOW = 512
DIL_PAIRS = ((128, 1), (512, 4), (2048, 16))
DIL_GROUPS = 3
DIL_HPG = 4
DIL_HEADS = 12
MLA_HEADS = 16
MLA_Q_RANK = 1536
MLA_KV_RANK = 512
MLA_NOPE = 128
MLA_ROPE = 64
MLA_V = 128
ROPE_THETA = 10000.0

A_Q = NSA_HEADS * HEAD_DIM
A_KV = 6 * NSA_GROUPS * HEAD_DIM
A_G3 = 3 * NSA_HEADS
A_GATE = NSA_HEADS * HEAD_DIM
B_QKV = 3 * DIL_HEADS * HEAD_DIM
B_GATE = DIL_HPG * HEAD_DIM

LANES = 128
SUBLANES = 8
VMEM_LIMIT = 48 * 1024 * 1024

NSA_TQ = 256
DIL_TQ = 256
DIL_PW = 128
MLA_TQ = 512
LOG2E = math.log2(math.e)

_BF = jnp.bfloat16
_F32 = jnp.float32


def _cparams(sem):
    return pltpu.CompilerParams(dimension_semantics=sem, vmem_limit_bytes=VMEM_LIMIT)


def _dot(a, b):
    return jnp.dot(a, b, preferred_element_type=_F32)


def _dot_nt(a, b):
    return lax.dot_general(a, b, (((1,), (1,)), ((), ())), preferred_element_type=_F32)


def _silu(x):
    return x * (1.0 / (1.0 + jnp.exp(-x)))


def _sigmoid(x):
    return 1.0 / (1.0 + jnp.exp(-x))


def _pick_tile(n, candidates):
    for c in candidates:
        if n % c == 0:
            return c
    return n


def _rmsnorm_kernel(x_ref, w_ref, o_ref):
    x = x_ref[...]
    y = x * lax.rsqrt(jnp.mean(x * x, axis=-1, keepdims=True) + EPS)
    o_ref[...] = (y * w_ref[...]).astype(o_ref.dtype)


def _rmsnorm(x2d, w, width, col_block, out_dtype):
    m = x2d.shape[0]
    tm = _pick_tile(m, (512, 256, 128))
    return pl.pallas_call(
        _rmsnorm_kernel,
        grid=(m // tm,),
        in_specs=[pl.BlockSpec((tm, width), lambda i: (i, col_block)),
                  pl.BlockSpec((1, width), lambda i: (0, 0))],
        out_specs=pl.BlockSpec((tm, width), lambda i: (i, 0)),
        out_shape=jax.ShapeDtypeStruct((m, width), out_dtype),
        compiler_params=_cparams(("parallel",)),
        name="rmsnorm",
    )(x2d, w.reshape(1, width).astype(_F32))


def _matmul_kernel(a_ref, w_ref, o_ref):
    o_ref[...] = _dot(a_ref[...], w_ref[...]).astype(o_ref.dtype)


def _matmul(a, w, out_dtype):
    m, k = a.shape
    n = w.shape[1]
    tm = _pick_tile(m, (1024, 512, 256, 128))
    tn = _pick_tile(n, (1024, 768, 512, 384, 256, 128))
    return pl.pallas_call(
        _matmul_kernel,
        grid=(m // tm, n // tn),
        in_specs=[pl.BlockSpec((tm, k), lambda i, j: (i, 0)),
                  pl.BlockSpec((k, tn), lambda i, j: (0, j))],
        out_specs=pl.BlockSpec((tm, tn), lambda i, j: (i, j)),
        out_shape=jax.ShapeDtypeStruct((m, n), out_dtype),
        compiler_params=_cparams(("parallel", "arbitrary")),
        name="matmul",
    )(a, w)


def _matmul_nt_kernel(w_ref, a_ref, o_ref):
    o_ref[...] = _dot_nt(w_ref[...], a_ref[...]).astype(o_ref.dtype)


def _matmul_nt(w_t, a, out_dtype):
    n, k = w_t.shape
    m = a.shape[0]
    tm = _pick_tile(m, (1024, 512, 256, 128))
    tn = _pick_tile(n, (1024, 512, 256, 128))
    return pl.pallas_call(
        _matmul_nt_kernel,
        grid=(m // tm, n // tn),
        in_specs=[pl.BlockSpec((tn, k), lambda i, j: (j, 0)),
                  pl.BlockSpec((tm, k), lambda i, j: (i, 0))],
        out_specs=pl.BlockSpec((tn, tm), lambda i, j: (j, i)),
        out_shape=jax.ShapeDtypeStruct((n, m), out_dtype),
        compiler_params=_cparams(("parallel", "arbitrary")),
        name="matmul_nt",
    )(w_t, a)


def _rope_chunk(chunk, cos, sin):
    return chunk * cos + pltpu.roll(chunk, 64, 1) * sin


def _matmul_rope_kernel(a_ref, w_ref, cos_ref, sin_ref, o_ref, *, heads_per_tile):
    acc = _dot(a_ref[...], w_ref[...])
    cos = cos_ref[...]
    sin = sin_ref[...]
    for h in range(heads_per_tile):
        base = h * 2 * LANES
        o_ref[:, base:base + LANES] = acc[:, base:base + LANES].astype(o_ref.dtype)
        o_ref[:, base + LANES:base + 2 * LANES] = _rope_chunk(
            acc[:, base + LANES:base + 2 * LANES], cos, sin).astype(o_ref.dtype)


def _matmul_rope(a, w, cos, sin, seq):
    m, k = a.shape
    n = w.shape[1]
    tm = _pick_tile(seq, (1024, 512, 256, 128))
    tn = 1024
    tpb = seq // tm
    return pl.pallas_call(
        functools.partial(_matmul_rope_kernel, heads_per_tile=tn // (2 * LANES)),
        grid=(m // tm, n // tn),
        in_specs=[pl.BlockSpec((tm, k), lambda i, j: (i, 0)),
                  pl.BlockSpec((k, tn), lambda i, j: (0, j)),
                  pl.BlockSpec((tm, LANES), lambda i, j: (i % tpb, 0)),
                  pl.BlockSpec((tm, LANES), lambda i, j: (i % tpb, 0))],
        out_specs=pl.BlockSpec((tm, tn), lambda i, j: (i, j)),
        out_shape=jax.ShapeDtypeStruct((m, n), _BF),
        compiler_params=_cparams(("parallel", "arbitrary")),
        name="matmul_rope",
    )(a, w, cos, sin)


def _rope_cols_kernel(x_ref, cos_ref, sin_ref, o_ref):
    o_ref[...] = _rope_chunk(x_ref[...], cos_ref[...], sin_ref[...]).astype(o_ref.dtype)


def _rope_cols(z2d, col_block, cos, sin, seq):
    m = z2d.shape[0]
    tm = _pick_tile(seq, (512, 256, 128))
    tpb = seq // tm
    return pl.pallas_call(
        _rope_cols_kernel,
        grid=(m // tm,),
        in_specs=[pl.BlockSpec((tm, LANES), lambda i: (i, col_block)),
                  pl.BlockSpec((tm, LANES), lambda i: (i % tpb, 0)),
                  pl.BlockSpec((tm, LANES), lambda i: (i % tpb, 0))],
        out_specs=pl.BlockSpec((tm, LANES), lambda i: (i, 0)),
        out_shape=jax.ShapeDtypeStruct((m, LANES), _BF),
        compiler_params=_cparams(("parallel",)),
        name="rope_kpe",
    )(z2d, cos, sin)


def _compress_kernel(blk_ref, pos_ref, w1_ref, w2_ref, o_ref, acc_ref, *, lt):
    li = pl.program_id(1)

    @pl.when(li == 0)
    def _():
        acc_ref[...] = jnp.zeros_like(acc_ref)

    acc = acc_ref[...]
    for l in range(lt):
        a = (blk_ref[l] + pos_ref[l:l + 1, :]).astype(_BF)
        acc = acc + _dot(a, w1_ref[l])
    acc_ref[...] = acc

    @pl.when(li == pl.num_programs(1) - 1)
    def _():
        hid = _silu(acc_ref[...]).astype(_BF)
        o_ref[...] = _dot(hid, w2_ref[...])


def _compress(kcvc_t, pos, w1, w2):
    _, L, R, _ = kcvc_t.shape
    lt = 16
    return pl.pallas_call(
        functools.partial(_compress_kernel, lt=lt),
        grid=(4, L // lt),
        in_specs=[pl.BlockSpec((None, lt, R, HEAD_DIM), lambda c, l: (c, l, 0, 0)),
                  pl.BlockSpec((None, lt, HEAD_DIM), lambda c, l: (c // 2, l, 0)),
                  pl.BlockSpec((None, lt, HEAD_DIM, HEAD_DIM), lambda c, l: (c // 2, l, 0, 0)),
                  pl.BlockSpec((None, HEAD_DIM, HEAD_DIM), lambda c, l: (c // 2, 0, 0))],
        out_specs=pl.BlockSpec((None, R, HEAD_DIM), lambda c, l: (c, 0, 0)),
        out_shape=jax.ShapeDtypeStruct((4, R, HEAD_DIM), _F32),
        scratch_shapes=[pltpu.VMEM((R, HEAD_DIM), _F32)],
        compiler_params=_cparams(("parallel", "arbitrary")),
        name="nsa_compress",
    )(kcvc_t, pos, w1.reshape(2, L, HEAD_DIM, HEAD_DIM), w2)


def _flash_reset(m_ref, l_ref, acc_ref):
    m_ref[...] = jnp.full(m_ref.shape, -jnp.inf, _F32)
    l_ref[...] = jnp.zeros(l_ref.shape, _F32)
    acc_ref[...] = jnp.zeros(acc_ref.shape, _F32)


def _flash_update(s, vt, c, m_ref, l_ref, acc_ref):
    m_prev = m_ref[...]
    m_new = jnp.maximum(m_prev, jnp.max(s, axis=0, keepdims=True))
    if c is None:
        alpha = jnp.exp2(m_prev - m_new)
        p = jnp.exp2(s - m_new)
    else:
        alpha = jnp.exp2((m_prev - m_new) * c)
        p = jnp.exp2((s - m_new) * c)
    l_ref[...] = alpha * l_ref[...] + jnp.sum(p, axis=0, keepdims=True)
    acc_ref[...] = alpha * acc_ref[...] + _dot(vt, p.astype(_BF))
    m_ref[...] = m_new


def _causal_flash(n_off, scores_into, consume, sa_ref, sb_ref, primed=False):
    if not primed:
        scores_into(sa_ref, 0)

    def pair(p, carry):
        k0 = 2 * p
        scores_into(sb_ref, k0 + 1)
        consume(sa_ref, k0, False)
        scores_into(sa_ref, k0 + 2)
        consume(sb_ref, k0 + 1, False)
        return carry

    lax.fori_loop(0, n_off // 2, pair, 0)

    @pl.when(n_off % 2 == 1)
    def _():
        scores_into(sb_ref, n_off)
        consume(sa_ref, n_off - 1, False)
        consume(sb_ref, n_off, True)

    @pl.when(n_off % 2 == 0)
    def _():
        consume(sa_ref, n_off, True)


def _nsa_kernel(q_ref, ks_ref, kw_ref, vst_ref, vwt_ref, kc_ref, vc_ref, g3_ref, oh_ref, bt_ref, bc_ref,
                o_ref, sa_ref, sb_ref, w0_ref, w1_ref, w2_ref, m_ref, l_ref, mw_ref, lw_ref, acc_s_ref, acc_w_ref,
                *, seq):
    tq = NSA_TQ
    hp = NSA_HPG
    n = hp * tq
    nblk = seq // NSA_BLOCK
    qi = pl.program_id(2)
    scale = HEAD_DIM ** -0.5
    c = scale * LOG2E
    t0 = qi * tq

    qs = jnp.concatenate([q_ref[:, h * HEAD_DIM:(h + 1) * HEAD_DIM] for h in range(hp)], axis=0)

    def rows_of(kt):
        return pl.ds(pl.multiple_of(kt * tq, tq), tq)

    for delta, dst_ref in enumerate((w0_ref, w1_ref, w2_ref)):
        dst_ref[...] = _dot_nt(kw_ref[rows_of(jnp.maximum(qi - delta, 0)), :], qs) * c + bt_ref[delta]

    kc = kc_ref[...].astype(_BF)
    s_c = _dot_nt(kc, qs) * scale
    s_c = s_c + jnp.concatenate([bc_ref[h] for h in range(hp)], axis=1)
    n_col = lax.broadcasted_iota(jnp.int32, (nblk, tq), 0)
    t_row = t0 + lax.broadcasted_iota(jnp.int32, (nblk, tq), 1)
    n_all = lax.broadcasted_iota(jnp.int32, (nblk, n), 0)
    t_all = t0 + (lax.broadcasted_iota(jnp.int32, (nblk, n), 1) & (tq - 1))
    valid = t_all >= n_all * NSA_BLOCK + (NSA_BLOCK - 1)
    s_c = jnp.where(valid, s_c, NEG)
    m_c = jnp.max(s_c, axis=0, keepdims=True)
    p_c = jnp.where(valid, jnp.exp(s_c - m_c), 0.0)
    l_c = jnp.sum(p_c, axis=0, keepdims=True)
    p_c = p_c / jnp.where(l_c > 0.0, l_c, 1.0)
    imp = p_c[:, 0:tq]
    for h in range(1, hp):
        imp = imp + p_c[:, h * tq:(h + 1) * tq]

    pad_rows = LANES - nblk
    vc_pad = jnp.concatenate([vc_ref[...], jnp.zeros((pad_rows, HEAD_DIM), _F32)], axis=0)
    vc_t = jnp.transpose(vc_pad).astype(_BF)
    p_pad = jnp.concatenate([p_c, jnp.zeros((pad_rows, n), _F32)], axis=0).astype(_BF)
    o_cmp = _dot(vc_t, p_pad)

    cur = jnp.right_shift(t_row, int(math.log2(NSA_BLOCK)))
    forced = (n_col == 0) | (n_col == cur) | (n_col == cur - 1)
    impv = jnp.where(forced, jnp.inf, jnp.where(n_col > cur, -jnp.inf, imp))
    rank = jnp.zeros((nblk, tq), _F32)
    sub_row = lax.broadcasted_iota(jnp.int32, (SUBLANES, tq), 0)
    for i in range(nblk):
        row = impv[i:i + 1, :]
        lo = i // SUBLANES * SUBLANES
        parts = []
        if lo > 0:
            parts.append(jnp.where(row > impv[:lo], 1.0, 0.0))
        mid = impv[lo:lo + SUBLANES]
        parts.append(jnp.where(sub_row > i - lo,
                               jnp.where(row >= mid, 1.0, 0.0), jnp.where(row > mid, 1.0, 0.0)))
        if lo + SUBLANES < nblk:
            parts.append(jnp.where(row >= impv[lo + SUBLANES:], 1.0, 0.0))
        rank = rank + jnp.concatenate(parts, axis=0)
    sel = (rank < float(min(NSA_TOPK, nblk))) & (n_col <= cur)
    pen = jnp.where(sel, 0.0, NEG)
    pen = jnp.concatenate([pen, jnp.full((pad_rows, tq), NEG, _F32)], axis=0)
    pen_t = jnp.transpose(pen).astype(_BF)
    q_aug = jnp.concatenate([qs, jnp.concatenate([pen_t] * hp, axis=0)], axis=1)

    jj = lax.broadcasted_iota(jnp.int32, (tq, n), 0)
    ii = lax.broadcasted_iota(jnp.int32, (tq, n), 1) & (tq - 1)

    def slc_scores(dst_ref, ki):
        rows = rows_of(ki)
        k_aug = jnp.concatenate([ks_ref[rows, :], oh_ref[rows, :]], axis=1)
        dst_ref[...] = _dot_nt(k_aug, q_aug) * c + bt_ref[qi - ki]

    def slc_consume(src_ref, ki, diagonal):
        s = src_ref[...]
        if diagonal:
            s = jnp.where(jj <= ii, s, NEG)
        _flash_update(s, vst_ref[:, rows_of(ki)], None, m_ref, l_ref, acc_s_ref)

    slc_scores(sa_ref, 0)

    def win_consume(src_ref, delta, keep):
        s = jnp.where(keep, src_ref[...], NEG)
        _flash_update(s, vwt_ref[:, rows_of(jnp.maximum(qi - delta, 0))], None, mw_ref, lw_ref, acc_w_ref)

    _flash_reset(mw_ref, lw_ref, acc_w_ref)
    win_consume(w0_ref, 0, jj <= ii)
    win_consume(w1_ref, 1, qi >= 1)
    win_consume(w2_ref, 2, (jj > ii) & (qi >= 2))
    o_win = acc_w_ref[...] / lw_ref[...]

    _flash_reset(m_ref, l_ref, acc_s_ref)
    _causal_flash(qi, slc_scores, slc_consume, sa_ref, sb_ref, primed=True)
    o_slc = acc_s_ref[...] / l_ref[...]

    g_t = jnp.transpose(_sigmoid(g3_ref[...]))
    for h in range(hp):
        cols = slice(h * tq, (h + 1) * tq)
        out_t = (g_t[0 * hp + h:0 * hp + h + 1, :] * o_cmp[:, cols]
                 + g_t[1 * hp + h:1 * hp + h + 1, :] * o_slc[:, cols]
                 + g_t[2 * hp + h:2 * hp + h + 1, :] * o_win[:, cols])
        o_ref[:, h * HEAD_DIM:(h + 1) * HEAD_DIM] = jnp.transpose(out_t)


def _nsa_attention(zb, vt, zf, kvc, onehot, bt, bc, bsz, seq):
    tq = NSA_TQ
    n = NSA_HPG * tq
    nblk = seq // NSA_BLOCK
    nd = seq // tq
    kcol = A_Q // HEAD_DIM
    g3col = ZF_COLS["g3"] // LANES
    return pl.pallas_call(
        functools.partial(_nsa_kernel, seq=seq),
        grid=(NSA_GROUPS, bsz, seq // tq),
        in_specs=[
            pl.BlockSpec((None, tq, NSA_HPG * HEAD_DIM), lambda g, b, i: (b, i, g)),
            pl.BlockSpec((None, seq, HEAD_DIM), lambda g, b, i: (b, 0, kcol + g)),
            pl.BlockSpec((None, seq, HEAD_DIM), lambda g, b, i: (b, 0, kcol + NSA_GROUPS + g)),
            pl.BlockSpec((HEAD_DIM, seq), lambda g, b, i: (g, b)),
            pl.BlockSpec((HEAD_DIM, seq), lambda g, b, i: (NSA_GROUPS + g, b)),
            pl.BlockSpec((None, None, nblk, HEAD_DIM), lambda g, b, i: (g, b, 0, 0)),
            pl.BlockSpec((None, None, nblk, HEAD_DIM), lambda g, b, i: (2 + g, b, 0, 0)),
            pl.BlockSpec((None, tq, LANES), lambda g, b, i: (b, i, g3col + g)),
            pl.BlockSpec((seq, LANES), lambda g, b, i: (0, 0)),
            pl.BlockSpec((None, nd, tq, n), lambda g, b, i: (g, 0, 0, 0), pipeline_mode=pl.Buffered(1)),
            pl.BlockSpec((None, NSA_HPG, nblk, tq), lambda g, b, i: (g, 0, 0, i)),
        ],
        out_specs=pl.BlockSpec((None, tq, NSA_HPG * HEAD_DIM), lambda g, b, i: (b, i, g)),
        out_shape=jax.ShapeDtypeStruct((bsz, seq, NSA_HEADS * HEAD_DIM), _F32),
        scratch_shapes=[pltpu.VMEM((tq, n), _F32) for _ in range(5)]
        + [pltpu.VMEM((1, n), _F32) for _ in range(4)]
        + [pltpu.VMEM((HEAD_DIM, n), _F32) for _ in range(2)],
        compiler_params=_cparams(("arbitrary", "arbitrary", "arbitrary")),
        name="nsa_attention",
    )(zb, zb, zb, vt, vt, kvc, kvc, zf, onehot, bt, bc)


def _dilated_kernel(*refs, seq):
    qkv_refs, (tab_ref, o_ref, qs_ref, ks_ref, vs_ref, lse_ref) = refs[:9], refs[9:]
    tq, pw = DIL_TQ, DIL_PW
    c = HEAD_DIM ** -0.5 * LOG2E
    for g, (_, d) in enumerate(DIL_PAIRS):
        q_ref, k_ref, v_ref = qkv_refs[3 * g:3 * g + 3]
        qs_ref[...] = q_ref[...].astype(_F32)
        ks_ref[...] = k_ref[...].astype(_F32)
        vs_ref[...] = v_ref[...].astype(_F32)
        ntile = seq // d // tq
        tab_d = tab_ref[g, :, pw:]
        tab_p = tab_ref[g, :, :pw]

        def tile_body(idx, carry, g=g, d=d, ntile=ntile, tab_d=tab_d, tab_p=tab_p):
            r = idx // ntile
            tile = idx % ntile
            u0 = tile * tq
            rows = pl.ds(u0 * d + r, tq, stride=d)
            rows_p = pl.ds(jnp.maximum(u0 - pw, 0) * d + r, pw, stride=d)
            q = qs_ref[rows, :].astype(_BF)
            s_d = _dot_nt(q, ks_ref[rows, :].astype(_BF)) * c + tab_d
            s_p = jnp.where(tile > 0, _dot_nt(q, ks_ref[rows_p, :].astype(_BF)) * c + tab_p, NEG)
            m = jnp.maximum(jnp.max(s_d, axis=1, keepdims=True), jnp.max(s_p, axis=1, keepdims=True))
            p_d = jnp.exp2(s_d - m)
            p_p = jnp.exp2(s_p - m)
            l = jnp.sum(p_d, axis=1, keepdims=True) + jnp.sum(p_p, axis=1, keepdims=True)
            o = (_dot(p_d.astype(_BF), vs_ref[rows, :].astype(_BF))
                 + _dot(p_p.astype(_BF), vs_ref[rows_p, :].astype(_BF))) / l
            lse = jnp.broadcast_to(m + jnp.log2(l), (tq, HEAD_DIM))
            if g == 0:
                o_ref[rows, :] = o
                lse_ref[rows, :] = lse
            else:
                lse_old = lse_ref[rows, :]
                top = jnp.maximum(lse_old, lse)
                e_old = jnp.exp2(lse_old - top)
                e_new = jnp.exp2(lse - top)
                o_ref[rows, :] = (e_old * o_ref[rows, :] + e_new * o) / (e_old + e_new)
                lse_ref[rows, :] = top + jnp.log2(e_old + e_new)
            return carry

        lax.fori_loop(0, d * ntile, tile_body, 0)


def _dilated_attention(zd3, dtab, bsz, seq):
    width = DIL_HPG * HEAD_DIM
    col = lambda part: pl.BlockSpec((None, seq, HEAD_DIM), lambda b, h, part=part: (b, 0, part * DIL_HPG + h))
    return pl.pallas_call(
        functools.partial(_dilated_kernel, seq=seq),
        grid=(bsz, DIL_HPG),
        in_specs=[col(part) for part in range(3 * DIL_GROUPS)]
        + [pl.BlockSpec((DIL_GROUPS, None, DIL_TQ, DIL_PW + DIL_TQ), lambda b, h: (0, h, 0, 0))],
        out_specs=pl.BlockSpec((None, seq, HEAD_DIM), lambda b, h: (b, 0, h)),
        out_shape=jax.ShapeDtypeStruct((bsz, seq, width), _F32),
        scratch_shapes=[pltpu.VMEM((seq, HEAD_DIM), _F32) for _ in range(4)],
        compiler_params=_cparams(("parallel", "arbitrary")),
        name="dilated_attention",
    )(*([zd3] * (3 * DIL_GROUPS)), dtab)


def _mla_kernel(q_ref, kn_ref, kp_ref, vt_ref, o_ref, sa_ref, sb_ref, m_ref, l_ref, acc_ref):
    tq = MLA_TQ
    qi = pl.program_id(2)
    c = (MLA_NOPE + MLA_ROPE) ** -0.5 * LOG2E
    q = q_ref[...]
    _flash_reset(m_ref, l_ref, acc_ref)
    jj = lax.broadcasted_iota(jnp.int32, (tq, tq), 0)
    ii = lax.broadcasted_iota(jnp.int32, (tq, tq), 1)

    def rows_of(ki):
        return pl.ds(pl.multiple_of(ki * tq, tq), tq)

    def scores_into(dst_ref, ki):
        rows = rows_of(ki)
        k = jnp.concatenate([kn_ref[rows, :], kp_ref[rows, :]], axis=1)
        dst_ref[...] = _dot_nt(k, q)

    def consume(src_ref, ki, diagonal):
        s = src_ref[...]
        if diagonal:
            s = jnp.where(jj <= ii, s, NEG)
        _flash_update(s, vt_ref[:, rows_of(ki)], c, m_ref, l_ref, acc_ref)

    _causal_flash(qi, scores_into, consume, sa_ref, sb_ref)
    o_ref[...] = jnp.transpose(acc_ref[...] / l_ref[...])


def _mla_attention(q, kn, kpe, vt, bsz, seq):
    tq = MLA_TQ
    return pl.pallas_call(
        _mla_kernel,
        grid=(bsz, MLA_HEADS, seq // tq),
        in_specs=[
            pl.BlockSpec((None, tq, 2 * LANES), lambda b, h, i: (b, i, h)),
            pl.BlockSpec((None, seq, MLA_NOPE), lambda b, h, i: (b, 0, h)),
            pl.BlockSpec((None, seq, LANES), lambda b, h, i: (b, 0, 0)),
            pl.BlockSpec((MLA_V, seq), lambda b, h, i: (h, b)),
        ],
        out_specs=pl.BlockSpec((None, tq, MLA_V), lambda b, h, i: (b, i, h)),
        out_shape=jax.ShapeDtypeStruct((bsz, seq, MLA_HEADS * MLA_V), _F32),
        scratch_shapes=[pltpu.VMEM((tq, tq), _F32), pltpu.VMEM((tq, tq), _F32),
                        pltpu.VMEM((1, tq), _F32), pltpu.VMEM((1, tq), _F32), pltpu.VMEM((MLA_V, tq), _F32)],
        compiler_params=_cparams(("parallel", "parallel", "arbitrary")),
        name="mla_attention",
    )(q, kn, kpe, vt)


def _residual_norm_store(x_new, nw_ref, x_out_ref, h_out_ref):
    x_out_ref[...] = x_new
    y = x_new * lax.rsqrt(jnp.mean(x_new * x_new, axis=-1, keepdims=True) + EPS)
    h_out_ref[...] = (y * nw_ref[...]).astype(h_out_ref.dtype)


def _out_even_kernel(x_ref, oa_ref, ga_ref, ob_ref, gb_ref, wa_ref, wb_ref, nw_ref, x_out_ref, h_out_ref):
    mixed_a = (oa_ref[...] * _silu(ga_ref[...])).astype(_BF)
    mixed_b = (ob_ref[...] * _silu(gb_ref[...])).astype(_BF)
    x_new = x_ref[...] + _dot(mixed_a, wa_ref[...]) + _dot(mixed_b, wb_ref[...])
    _residual_norm_store(x_new, nw_ref, x_out_ref, h_out_ref)


def _out_even(x2d, o_a, o_b, zf, w_out, nw_next, zf_cols, h_dtype):
    m, d = x2d.shape
    tm = 256
    wa = A_GATE
    wbd = B_GATE
    row = lambda width, cb=0: pl.BlockSpec((tm, width), lambda i, cb=cb: (i, cb))
    full = lambda r, c: pl.BlockSpec((r, c), lambda i: (0, 0))
    return pl.pallas_call(
        _out_even_kernel,
        grid=(m // tm,),
        in_specs=[row(d), row(wa), row(wa, zf_cols["gate_a"] // wa), row(wbd), row(wbd, zf_cols["gate_b"] // wbd),
                  full(wa, d), full(wbd, d), full(1, d)],
        out_specs=[row(d), row(d)],
        out_shape=[jax.ShapeDtypeStruct((m, d), _F32), jax.ShapeDtypeStruct((m, d), h_dtype)],
        compiler_params=_cparams(("parallel",)),
        name="out_proj_even",
    )(x2d, o_a, zf, o_b, zf, w_out[:wa], w_out[wa:], nw_next.reshape(1, d).astype(_F32))


def _out_odd_kernel(x_ref, oc_ref, gc_ref, w_ref, nw_ref, x_out_ref, h_out_ref):
    mixed = (oc_ref[...] * _silu(gc_ref[...])).astype(_BF)
    x_new = x_ref[...] + _dot(mixed, w_ref[...])
    _residual_norm_store(x_new, nw_ref, x_out_ref, h_out_ref)


def _out_odd(x2d, o_c, z, w_out, nw_next, gate_col_block, h_dtype):
    m, d = x2d.shape
    tm = 256
    width = MLA_HEADS * MLA_V
    row = lambda w, cb=0: pl.BlockSpec((tm, w), lambda i, cb=cb: (i, cb))
    return pl.pallas_call(
        _out_odd_kernel,
        grid=(m // tm,),
        in_specs=[row(d), row(width), row(width, gate_col_block),
                  pl.BlockSpec((width, d), lambda i: (0, 0)), pl.BlockSpec((1, d), lambda i: (0, 0))],
        out_specs=[row(d), row(d)],
        out_shape=[jax.ShapeDtypeStruct((m, d), _F32), jax.ShapeDtypeStruct((m, d), h_dtype)],
        compiler_params=_cparams(("parallel",)),
        name="out_proj_odd",
    )(x2d, o_c, z, w_out, nw_next.reshape(1, d).astype(_F32))


def _bucket_of_distance(n):
    d = np.arange(n)
    max_exact = NUM_BUCKETS // 2
    df = np.maximum(d, 1).astype(np.float32)
    large = max_exact + (np.log(df / max_exact) / math.log(MAX_DISTANCE / max_exact)
                         * (NUM_BUCKETS - max_exact)).astype(np.int32)
    large = np.minimum(large, NUM_BUCKETS - 1)
    return np.where(d < max_exact, d, large).astype(np.int32)


def _toeplitz_tiles(v, n, nd):
    hh = v.shape[0]
    vp = jnp.concatenate([jnp.zeros((hh, n), v.dtype), v, jnp.zeros((hh, n), v.dtype)], axis=1)
    idx = (n - np.arange(2 * n)) % (2 * n)
    tiles = []
    for dl in range(nd):
        w = vp[:, dl * n:dl * n + 2 * n]
        c = w[:, idx]
        flat = jnp.tile(c, (1, n))[:, :n * (2 * n - 1)]
        tiles.append(flat.reshape(hh, n, 2 * n - 1)[:, :, :n])
    return jnp.stack(tiles, axis=0)


def _bias_tables(rel_bias, seq):
    bd = rel_bias.astype(_F32)[_bucket_of_distance(seq)].T
    bd_nsa = bd[:NSA_HEADS]
    nd = seq // NSA_TQ
    bt = _toeplitz_tiles(bd_nsa * LOG2E, NSA_TQ, nd)
    bt = bt.reshape(nd, NSA_GROUPS, NSA_HPG, NSA_TQ, NSA_TQ).transpose(1, 0, 4, 2, 3)
    bt = bt.reshape(NSA_GROUPS, nd, NSA_TQ, NSA_HPG * NSA_TQ)
    nblk = seq // NSA_BLOCK
    off = NSA_BLOCK * (nblk - 1) + NSA_BLOCK - 1
    bdp = jnp.concatenate([jnp.zeros((NSA_HEADS, off), _F32), bd_nsa], axis=1)
    bc = jnp.stack([bdp[:, off - (NSA_BLOCK * n + NSA_BLOCK - 1):off - (NSA_BLOCK * n + NSA_BLOCK - 1) + seq]
                    for n in range(nblk)], axis=1)
    bc = bc.reshape(NSA_GROUPS, NSA_HPG, nblk, seq)
    rows, cols = DIL_TQ, DIL_PW + DIL_TQ
    period = rows + cols
    dtabs = []
    for g, (w, d) in enumerate(DIL_PAIRS):
        heads = bd[NSA_HEADS + g * DIL_HPG:NSA_HEADS + (g + 1) * DIL_HPG] * LOG2E
        m = w // d + 1
        assert m == DIL_PW + 1
        by_offset = heads[:, 0:m * d:d]
        cyc = jnp.concatenate([by_offset[:, ::-1], jnp.full((DIL_HPG, period - m), NEG, _F32)], axis=1)
        flat = jnp.tile(cyc, (1, rows))[:, :rows * (period - 1)]
        dtabs.append(flat.reshape(DIL_HPG, rows, period - 1)[:, :, :cols])
    return bt, bc, jnp.stack(dtabs, axis=0)


def _rope_tables(seq):
    half = MLA_ROPE // 2
    freqs = 1.0 / (ROPE_THETA ** (jnp.arange(half, dtype=_F32) / half))
    ang = jnp.arange(seq).astype(_F32)[:, None] * freqs[None, :]
    zeros = jnp.zeros((seq, LANES - MLA_ROPE), _F32)
    cos = jnp.concatenate([jnp.cos(ang), jnp.cos(ang), zeros], axis=1)
    sin = jnp.concatenate([jnp.sin(ang), jnp.sin(ang), zeros], axis=1)
    return cos, sin


def _rot_cols(w):
    half = w.shape[-1] // 2
    return jnp.concatenate([-w[..., half:], w[..., :half]], axis=-1)


ZF_COLS = {"gate_a": 0, "gate_b": A_GATE, "kcvc": A_GATE + B_GATE, "g3": A_GATE + B_GATE + 4 * HEAD_DIM}


def _even_weights(w_in):
    o = 0
    q_a = w_in[:, o:o + A_Q]; o += A_Q
    kv_a = w_in[:, o:o + A_KV]; o += A_KV
    g3 = w_in[:, o:o + A_G3]; o += A_G3
    gate_a = w_in[:, o:o + A_GATE]; o += A_GATE
    qkv_b = w_in[:, o:o + B_QKV]; o += B_QKV
    gate_b = w_in[:, o:o + B_GATE]
    gw = NSA_GROUPS * HEAD_DIM
    kc_vc, ks, vs, kw, vw = (kv_a[:, :2 * gw], kv_a[:, 2 * gw:3 * gw], kv_a[:, 3 * gw:4 * gw],
                             kv_a[:, 4 * gw:5 * gw], kv_a[:, 5 * gw:6 * gw])
    wb = jnp.concatenate([q_a, ks, kw], axis=1).astype(_BF)
    wvt = jnp.concatenate([vs, vw], axis=1).T.astype(_BF)
    dw = DIL_HPG * HEAD_DIM
    nq = DIL_HEADS * HEAD_DIM
    wd = jnp.concatenate([qkv_b[:, part * nq + g * dw:part * nq + (g + 1) * dw]
                          for g in range(DIL_GROUPS) for part in range(3)], axis=1).astype(_BF)
    g3_blocks = []
    for g in range(NSA_GROUPS):
        cols = [(g * NSA_HPG + h) * 3 + j for j in range(3) for h in range(NSA_HPG)]
        blk = g3[:, np.asarray(cols)]
        g3_blocks.append(jnp.pad(blk, ((0, 0), (0, LANES - len(cols)))))
    wf = jnp.concatenate([gate_a, gate_b, kc_vc] + g3_blocks, axis=1).astype(_BF)
    return wb, wvt, wd, wf


def _even_layer(x2d, h, w_in, cmp_pos, cmp_w1, cmp_w2, w_out, nw_next, tables, bsz, seq, h_dtype):
    bt, bc, dtab, onehot = tables
    wb, wvt, wd, wf = _even_weights(w_in)
    zb = _matmul(h, wb, _BF)
    vt = _matmul_nt(wvt, h, _BF)
    zd = _matmul(h, wd, _BF)
    zf = _matmul(h, wf, _F32)
    nblk = seq // NSA_BLOCK
    kcvc = zf[:, ZF_COLS["kcvc"]:ZF_COLS["kcvc"] + 4 * HEAD_DIM]
    kcvc_t = kcvc.reshape(bsz * nblk, NSA_BLOCK, 4, HEAD_DIM).transpose(2, 1, 0, 3)
    kvc = _compress(kcvc_t, cmp_pos.astype(_F32), cmp_w1.astype(_BF), cmp_w2.astype(_BF))
    kvc = kvc.reshape(4, bsz, nblk, HEAD_DIM)
    zf3 = zf.reshape(bsz, seq, zf.shape[-1])
    o_a = _nsa_attention(zb.reshape(bsz, seq, -1), vt, zf3, kvc, onehot, bt, bc, bsz, seq)
    o_b = _dilated_attention(zd.reshape(bsz, seq, -1), dtab, bsz, seq)
    return _out_even(x2d, o_a.reshape(bsz * seq, -1), o_b.reshape(bsz * seq, -1), zf, w_out.astype(_BF), nw_next,
                     ZF_COLS, h_dtype)


def _odd_layer(x2d, h, w_in, q_norm, w_qb, kv_norm, w_kvb, w_out, nw_next, rope, bsz, seq, h_dtype):
    cos, sin = rope
    o = 0
    w_cq = w_in[:, o:o + MLA_Q_RANK]; o += MLA_Q_RANK
    w_ckv = w_in[:, o:o + MLA_KV_RANK]; o += MLA_KV_RANK
    w_kpe = w_in[:, o:o + MLA_ROPE]; o += MLA_ROPE
    w_gate = w_in[:, o:]
    w1 = jnp.concatenate([w_cq, w_ckv, w_gate], axis=1).astype(_BF)
    z = _matmul(h, w1, _F32)
    zk = _matmul(h, jnp.concatenate([w_kpe, _rot_cols(w_kpe)], axis=1).astype(_BF), _F32)
    gate_width = MLA_HEADS * MLA_V
    cqn = _rmsnorm(z, q_norm, MLA_Q_RANK, 0, _BF)
    ckvn = _rmsnorm(z, kv_norm, MLA_KV_RANK, MLA_Q_RANK // MLA_KV_RANK, _BF)
    kpe = _rope_cols(zk, 0, cos, sin, seq)
    wq = w_qb.reshape(MLA_Q_RANK, MLA_HEADS, MLA_NOPE + MLA_ROPE)
    wq_pe = wq[:, :, MLA_NOPE:]
    wq = jnp.concatenate([wq[:, :, :MLA_NOPE], wq_pe, _rot_cols(wq_pe)], axis=-1)
    q = _matmul_rope(cqn, wq.reshape(MLA_Q_RANK, MLA_HEADS * 2 * LANES).astype(_BF), cos, sin, seq)
    wkv = w_kvb.reshape(MLA_KV_RANK, MLA_HEADS, MLA_NOPE + MLA_V)
    wk = wkv[:, :, :MLA_NOPE].reshape(MLA_KV_RANK, -1).astype(_BF)
    wv_t = wkv[:, :, MLA_NOPE:].reshape(MLA_KV_RANK, -1).T.astype(_BF)
    kn = _matmul(ckvn, wk, _BF)
    vt = _matmul_nt(wv_t, ckvn, _BF)
    o_c = _mla_attention(q.reshape(bsz, seq, -1), kn.reshape(bsz, seq, -1), kpe.reshape(bsz, seq, LANES),
                         vt, bsz, seq)
    gate_block = (MLA_Q_RANK + MLA_KV_RANK) // gate_width
    return _out_odd(x2d, o_c.reshape(bsz * seq, -1), z, w_out.astype(_BF), nw_next, gate_block, h_dtype)


def kernel(x, rel_bias, norm_w, final_norm_w, ev_w_in, nsa_cmp_pos, nsa_cmp_w1, nsa_cmp_w2, ev_w_out,
           od_w_in, mla_q_norm, mla_w_qb, mla_kv_norm, mla_w_kvb, od_w_out):
    bsz, seq, d = x.shape
    depth = norm_w.shape[0]
    assert seq % MLA_TQ == 0 and seq % (DIL_TQ * DIL_PAIRS[-1][1]) == 0 and seq // NSA_BLOCK <= LANES
    assert NSA_WINDOW == 2 * NSA_TQ
    bt, bc, dtab = _bias_tables(rel_bias, seq)
    blk_id = np.arange(seq)[:, None] // NSA_BLOCK
    onehot = jnp.asarray(blk_id == np.arange(LANES)[None, :], dtype=_BF)
    tables = (bt, bc, dtab, onehot)
    rope = _rope_tables(seq)
    x2d = x.reshape(bsz * seq, d).astype(_F32)
    h = _rmsnorm(x2d, norm_w[0], d, 0, _BF)
    for l in range(depth):
        last = l == depth - 1
        nw_next = final_norm_w if last else norm_w[l + 1]
        h_dtype = _F32 if last else _BF
        i = l // 2
        if l % 2 == 0:
            x2d, h = _even_layer(x2d, h, ev_w_in[i], nsa_cmp_pos[i], nsa_cmp_w1[i], nsa_cmp_w2[i],
                                 ev_w_out[i], nw_next, tables, bsz, seq, h_dtype)
        else:
            x2d, h = _odd_layer(x2d, h, od_w_in[i], mla_q_norm[i], mla_w_qb[i], mla_kv_norm[i],
                                mla_w_kvb[i], od_w_out[i], nw_next, rope, bsz, seq, h_dtype)
    return h.reshape(bsz, seq, d)
```

```python
import functools
import math

import numpy as np
import jax
import jax.numpy as jnp
from jax import lax
from jax.experimental import pallas as pl
from jax.experimental.pallas import tpu as pltpu

HEAD_DIM = 128
EPS = 1e-6
NEG = -1e30
NUM_BUCKETS = 32
MAX_DISTANCE = 2048
NSA_HEADS = 8
NSA_GROUPS = 2
NSA_HPG = 4
NSA_BLOCK = 64
NSA_TOPK = 16
NSA_WINDOW = 512
DIL_PAIRS = ((128, 1), (512, 4), (2048, 16))
DIL_GROUPS = 3
DIL_HPG = 4
DIL_HEADS = 12
MLA_HEADS = 16
MLA_Q_RANK = 1536
MLA_KV_RANK = 512
MLA_NOPE = 128
MLA_ROPE = 64
MLA_V = 128
ROPE_THETA = 10000.0

A_Q = NSA_HEADS * HEAD_DIM
A_KV = 6 * NSA_GROUPS * HEAD_DIM
A_G3 = 3 * NSA_HEADS
A_GATE = NSA_HEADS * HEAD_DIM
B_QKV = 3 * DIL_HEADS * HEAD_DIM
B_GATE = DIL_HPG * HEAD_DIM

LANES = 128
SUBLANES = 8
VMEM_LIMIT = 48 * 1024 * 1024

NSA_TQ = 256
DIL_TQ = 256
DIL_PW = 128
DIL_UNROLL = 4
MLA_TQ = 512
LOG2E = math.log2(math.e)

_BF = jnp.bfloat16
_F32 = jnp.float32


def _cparams(sem):
    return pltpu.CompilerParams(dimension_semantics=sem, vmem_limit_bytes=VMEM_LIMIT)


def _dot(a, b):
    return jnp.dot(a, b, preferred_element_type=_F32)


def _dot_nt(a, b):
    return lax.dot_general(a, b, (((1,), (1,)), ((), ())), preferred_element_type=_F32)


def _silu(x):
    return x * (1.0 / (1.0 + jnp.exp(-x)))


def _sigmoid(x):
    return 1.0 / (1.0 + jnp.exp(-x))


def _pick_tile(n, candidates):
    for c in candidates:
        if n % c == 0:
            return c
    return n


def _rmsnorm_kernel(x_ref, w_ref, o_ref):
    x = x_ref[...]
    y = x * lax.rsqrt(jnp.mean(x * x, axis=-1, keepdims=True) + EPS)
    o_ref[...] = (y * w_ref[...]).astype(o_ref.dtype)


def _rmsnorm(x2d, w, width, col_block, out_dtype):
    m = x2d.shape[0]
    tm = _pick_tile(m, (512, 256, 128))
    return pl.pallas_call(
        _rmsnorm_kernel,
        grid=(m // tm,),
        in_specs=[pl.BlockSpec((tm, width), lambda i: (i, col_block)),
                  pl.BlockSpec((1, width), lambda i: (0, 0))],
        out_specs=pl.BlockSpec((tm, width), lambda i: (i, 0)),
        out_shape=jax.ShapeDtypeStruct((m, width), out_dtype),
        compiler_params=_cparams(("parallel",)),
        name="rmsnorm",
    )(x2d, w.reshape(1, width).astype(_F32))


def _matmul_kernel(a_ref, w_ref, o_ref):
    o_ref[...] = _dot(a_ref[...], w_ref[...]).astype(o_ref.dtype)


def _matmul(a, w, out_dtype):
    m, k = a.shape
    n = w.shape[1]
    tm = _pick_tile(m, (1024, 512, 256, 128))
    tn = _pick_tile(n, (1024, 768, 512, 384, 256, 128))
    return pl.pallas_call(
        _matmul_kernel,
        grid=(m // tm, n // tn),
        in_specs=[pl.BlockSpec((tm, k), lambda i, j: (i, 0)),
                  pl.BlockSpec((k, tn), lambda i, j: (0, j))],
        out_specs=pl.BlockSpec((tm, tn), lambda i, j: (i, j)),
        out_shape=jax.ShapeDtypeStruct((m, n), out_dtype),
        compiler_params=_cparams(("parallel", "arbitrary")),
        name="matmul",
    )(a, w)


def _matmul_nt_kernel(w_ref, a_ref, o_ref):
    o_ref[...] = _dot_nt(w_ref[...], a_ref[...]).astype(o_ref.dtype)


def _matmul_nt(w_t, a, out_dtype):
    n, k = w_t.shape
    m = a.shape[0]
    tm = _pick_tile(m, (1024, 512, 256, 128))
    tn = _pick_tile(n, (1024, 512, 256, 128))
    return pl.pallas_call(
        _matmul_nt_kernel,
        grid=(m // tm, n // tn),
        in_specs=[pl.BlockSpec((tn, k), lambda i, j: (j, 0)),
                  pl.BlockSpec((tm, k), lambda i, j: (i, 0))],
        out_specs=pl.BlockSpec((tn, tm), lambda i, j: (j, i)),
        out_shape=jax.ShapeDtypeStruct((n, m), out_dtype),
        compiler_params=_cparams(("parallel", "arbitrary")),
        name="matmul_nt",
    )(w_t, a)


def _rope_chunk(chunk, cos, sin):
    return chunk * cos + pltpu.roll(chunk, 64, 1) * sin


def _matmul_rope_kernel(a_ref, w_ref, cos_ref, sin_ref, o_ref, *, heads_per_tile):
    acc = _dot(a_ref[...], w_ref[...])
    cos = cos_ref[...]
    sin = sin_ref[...]
    for h in range(heads_per_tile):
        base = h * 2 * LANES
        o_ref[:, base:base + LANES] = acc[:, base:base + LANES].astype(o_ref.dtype)
        o_ref[:, base + LANES:base + 2 * LANES] = _rope_chunk(
            acc[:, base + LANES:base + 2 * LANES], cos, sin).astype(o_ref.dtype)


def _matmul_rope(a, w, cos, sin, seq):
    m, k = a.shape
    n = w.shape[1]
    tm = _pick_tile(seq, (1024, 512, 256, 128))
    tn = 1024
    tpb = seq // tm
    return pl.pallas_call(
        functools.partial(_matmul_rope_kernel, heads_per_tile=tn // (2 * LANES)),
        grid=(m // tm, n // tn),
        in_specs=[pl.BlockSpec((tm, k), lambda i, j: (i, 0)),
                  pl.BlockSpec((k, tn), lambda i, j: (0, j)),
                  pl.BlockSpec((tm, LANES), lambda i, j: (i % tpb, 0)),
                  pl.BlockSpec((tm, LANES), lambda i, j: (i % tpb, 0))],
        out_specs=pl.BlockSpec((tm, tn), lambda i, j: (i, j)),
        out_shape=jax.ShapeDtypeStruct((m, n), _BF),
        compiler_params=_cparams(("parallel", "arbitrary")),
        name="matmul_rope",
    )(a, w, cos, sin)


def _rope_cols_kernel(x_ref, cos_ref, sin_ref, o_ref):
    o_ref[...] = _rope_chunk(x_ref[...], cos_ref[...], sin_ref[...]).astype(o_ref.dtype)


def _rope_cols(z2d, col_block, cos, sin, seq):
    m = z2d.shape[0]
    tm = _pick_tile(seq, (512, 256, 128))
    tpb = seq // tm
    return pl.pallas_call(
        _rope_cols_kernel,
        grid=(m // tm,),
        in_specs=[pl.BlockSpec((tm, LANES), lambda i: (i, col_block)),
                  pl.BlockSpec((tm, LANES), lambda i: (i % tpb, 0)),
                  pl.BlockSpec((tm, LANES), lambda i: (i % tpb, 0))],
        out_specs=pl.BlockSpec((tm, LANES), lambda i: (i, 0)),
        out_shape=jax.ShapeDtypeStruct((m, LANES), _BF),
        compiler_params=_cparams(("parallel",)),
        name="rope_kpe",
    )(z2d, cos, sin)


def _compress_kernel(blk_ref, pos_ref, w1_ref, w2_ref, o_ref, acc_ref, *, lt):
    li = pl.program_id(1)

    @pl.when(li == 0)
    def _():
        acc_ref[...] = jnp.zeros_like(acc_ref)

    acc = acc_ref[...]
    for l in range(lt):
        a = (blk_ref[l] + pos_ref[l:l + 1, :]).astype(_BF)
        acc = acc + _dot(a, w1_ref[l])
    acc_ref[...] = acc

    @pl.when(li == pl.num_programs(1) - 1)
    def _():
        hid = _silu(acc_ref[...]).astype(_BF)
        o_ref[...] = _dot(hid, w2_ref[...])


def _compress(kcvc_t, pos, w1, w2):
    _, L, R, _ = kcvc_t.shape
    lt = 16
    return pl.pallas_call(
        functools.partial(_compress_kernel, lt=lt),
        grid=(4, L // lt),
        in_specs=[pl.BlockSpec((None, lt, R, HEAD_DIM), lambda c, l: (c, l, 0, 0)),
                  pl.BlockSpec((None, lt, HEAD_DIM), lambda c, l: (c // 2, l, 0)),
                  pl.BlockSpec((None, lt, HEAD_DIM, HEAD_DIM), lambda c, l: (c // 2, l, 0, 0)),
                  pl.BlockSpec((None, HEAD_DIM, HEAD_DIM), lambda c, l: (c // 2, 0, 0))],
        out_specs=pl.BlockSpec((None, R, HEAD_DIM), lambda c, l: (c, 0, 0)),
        out_shape=jax.ShapeDtypeStruct((4, R, HEAD_DIM), _F32),
        scratch_shapes=[pltpu.VMEM((R, HEAD_DIM), _F32)],
        compiler_params=_cparams(("parallel", "arbitrary")),
        name="nsa_compress",
    )(kcvc_t, pos, w1.reshape(2, L, HEAD_DIM, HEAD_DIM), w2)


def _flash_reset(m_ref, l_ref, acc_ref):
    m_ref[...] = jnp.full(m_ref.shape, -jnp.inf, _F32)
    l_ref[...] = jnp.zeros(l_ref.shape, _F32)
    acc_ref[...] = jnp.zeros(acc_ref.shape, _F32)


def _flash_update(s, vt, c, m_ref, l_ref, acc_ref, tile_max=None):
    m_prev = m_ref[...]
    if tile_max is None:
        tile_max = jnp.max(s, axis=0, keepdims=True)
    m_new = jnp.maximum(m_prev, tile_max)
    if c is None:
        alpha = jnp.exp2(m_prev - m_new)
        p = jnp.exp2(s - m_new)
    else:
        alpha = jnp.exp2((m_prev - m_new) * c)
        p = jnp.exp2((s - m_new) * c)
    l_ref[...] = alpha * l_ref[...] + jnp.sum(p, axis=0, keepdims=True)
    acc_ref[...] = alpha * acc_ref[...] + _dot(vt, p.astype(_BF))
    m_ref[...] = m_new


def _store_scores(buf, s):
    s_ref, mx_ref = buf
    s_ref[...] = s
    mx_ref[...] = jnp.max(s, axis=0, keepdims=True)


def _causal_flash(n_off, scores_into, consume, buf_a, buf_b, primed=False):
    if not primed:
        scores_into(buf_a, 0)

    def pair(p, carry):
        k0 = 2 * p
        scores_into(buf_b, k0 + 1)
        consume(buf_a, k0, False)
        scores_into(buf_a, k0 + 2)
        consume(buf_b, k0 + 1, False)
        return carry

    lax.fori_loop(0, n_off // 2, pair, 0)

    @pl.when(n_off % 2 == 1)
    def _():
        scores_into(buf_b, n_off)
        consume(buf_a, n_off - 1, False)
        consume(buf_b, n_off, True)

    @pl.when(n_off % 2 == 0)
    def _():
        consume(buf_a, n_off, True)


def _nsa_kernel(q_ref, ks_ref, kw_ref, vst_ref, vwt_ref, kc_ref, vc_ref, g3_ref, oh_ref, bt_ref, bc_ref,
                o_ref, sa_ref, sb_ref, w0_ref, w1_ref, w2_ref, m_ref, l_ref, mw_ref, lw_ref, mxa_ref, mxb_ref,
                acc_s_ref, acc_w_ref, *, seq):
    tq = NSA_TQ
    hp = NSA_HPG
    n = hp * tq
    nblk = seq // NSA_BLOCK
    qi = pl.program_id(2)
    scale = HEAD_DIM ** -0.5
    c = scale * LOG2E
    t0 = qi * tq

    qs = jnp.concatenate([q_ref[:, h * HEAD_DIM:(h + 1) * HEAD_DIM] for h in range(hp)], axis=0)

    def rows_of(kt):
        return pl.ds(pl.multiple_of(kt * tq, tq), tq)

    for delta, dst_ref in enumerate((w0_ref, w1_ref, w2_ref)):
        dst_ref[...] = _dot_nt(kw_ref[rows_of(jnp.maximum(qi - delta, 0)), :], qs) * c + bt_ref[delta]

    kc = kc_ref[...].astype(_BF)
    s_c = _dot_nt(kc, qs) * scale
    s_c = s_c + jnp.concatenate([bc_ref[h] for h in range(hp)], axis=1)
    n_col = lax.broadcasted_iota(jnp.int32, (nblk, tq), 0)
    t_row = t0 + lax.broadcasted_iota(jnp.int32, (nblk, tq), 1)
    n_all = lax.broadcasted_iota(jnp.int32, (nblk, n), 0)
    t_all = t0 + (lax.broadcasted_iota(jnp.int32, (nblk, n), 1) & (tq - 1))
    valid = t_all >= n_all * NSA_BLOCK + (NSA_BLOCK - 1)
    s_c = jnp.where(valid, s_c, NEG)
    m_c = jnp.max(s_c, axis=0, keepdims=True)
    p_c = jnp.where(valid, jnp.exp(s_c - m_c), 0.0)
    l_c = jnp.sum(p_c, axis=0, keepdims=True)
    p_c = p_c / jnp.where(l_c > 0.0, l_c, 1.0)
    imp = p_c[:, 0:tq]
    for h in range(1, hp):
        imp = imp + p_c[:, h * tq:(h + 1) * tq]

    pad_rows = LANES - nblk
    vc_pad = jnp.concatenate([vc_ref[...], jnp.zeros((pad_rows, HEAD_DIM), _F32)], axis=0)
    vc_t = jnp.transpose(vc_pad).astype(_BF)
    p_pad = jnp.concatenate([p_c, jnp.zeros((pad_rows, n), _F32)], axis=0).astype(_BF)
    o_cmp = _dot(vc_t, p_pad)

    cur = jnp.right_shift(t_row, int(math.log2(NSA_BLOCK)))
    forced = (n_col == 0) | (n_col == cur) | (n_col == cur - 1)
    impv = jnp.where(forced, jnp.inf, jnp.where(n_col > cur, -jnp.inf, imp))
    rank = jnp.zeros((nblk, tq), _F32)
    sub_row = lax.broadcasted_iota(jnp.int32, (SUBLANES, tq), 0)
    for i in range(nblk):
        row = impv[i:i + 1, :]
        lo = i // SUBLANES * SUBLANES
        parts = []
        if lo > 0:
            parts.append(jnp.where(row > impv[:lo], 1.0, 0.0))
        mid = impv[lo:lo + SUBLANES]
        parts.append(jnp.where(sub_row > i - lo,
                               jnp.where(row >= mid, 1.0, 0.0), jnp.where(row > mid, 1.0, 0.0)))
        if lo + SUBLANES < nblk:
            parts.append(jnp.where(row >= impv[lo + SUBLANES:], 1.0, 0.0))
        rank = rank + jnp.concatenate(parts, axis=0)
    sel = (rank < float(min(NSA_TOPK, nblk))) & (n_col <= cur)
    pen = jnp.where(sel, 0.0, NEG)
    pen = jnp.concatenate([pen, jnp.full((pad_rows, tq), NEG, _F32)], axis=0)
    pen_t = jnp.transpose(pen).astype(_BF)
    q_aug = jnp.concatenate([qs, jnp.concatenate([pen_t] * hp, axis=0)], axis=1)

    jj = lax.broadcasted_iota(jnp.int32, (tq, n), 0)
    ii = lax.broadcasted_iota(jnp.int32, (tq, n), 1) & (tq - 1)

    def slc_scores(buf, ki):
        rows = rows_of(ki)
        k_aug = jnp.concatenate([ks_ref[rows, :], oh_ref[rows, :]], axis=1)
        _store_scores(buf, _dot_nt(k_aug, q_aug) * c + bt_ref[qi - ki])

    def slc_consume(buf, ki, diagonal):
        s = buf[0][...]
        tile_max = buf[1][...]
        if diagonal:
            s = jnp.where(jj <= ii, s, NEG)
            tile_max = None
        _flash_update(s, vst_ref[:, rows_of(ki)], None, m_ref, l_ref, acc_s_ref, tile_max)

    slc_scores((sa_ref, mxa_ref), 0)

    def win_consume(src_ref, delta, keep):
        s = jnp.where(keep, src_ref[...], NEG)
        _flash_update(s, vwt_ref[:, rows_of(jnp.maximum(qi - delta, 0))], None, mw_ref, lw_ref, acc_w_ref)

    _flash_reset(mw_ref, lw_ref, acc_w_ref)
    win_consume(w0_ref, 0, jj <= ii)
    win_consume(w1_ref, 1, qi >= 1)
    win_consume(w2_ref, 2, (jj > ii) & (qi >= 2))
    o_win = acc_w_ref[...] / lw_ref[...]

    _flash_reset(m_ref, l_ref, acc_s_ref)
    _causal_flash(qi, slc_scores, slc_consume, (sa_ref, mxa_ref), (sb_ref, mxb_ref), primed=True)
    o_slc = acc_s_ref[...] / l_ref[...]

    g_t = jnp.transpose(_sigmoid(g3_ref[...]))
    for h in range(hp):
        cols = slice(h * tq, (h + 1) * tq)
        out_t = (g_t[0 * hp + h:0 * hp + h + 1, :] * o_cmp[:, cols]
                 + g_t[1 * hp + h:1 * hp + h + 1, :] * o_slc[:, cols]
                 + g_t[2 * hp + h:2 * hp + h + 1, :] * o_win[:, cols])
        o_ref[:, h * HEAD_DIM:(h + 1) * HEAD_DIM] = jnp.transpose(out_t)


def _nsa_attention(zb, vt, zf, kvc, onehot, bt, bc, bsz, seq):
    tq = NSA_TQ
    n = NSA_HPG * tq
    nblk = seq // NSA_BLOCK
    nd = seq // tq
    kcol = A_Q // HEAD_DIM
    g3col = ZF_COLS["g3"] // LANES
    return pl.pallas_call(
        functools.partial(_nsa_kernel, seq=seq),
        grid=(NSA_GROUPS, bsz, seq // tq),
        in_specs=[
            pl.BlockSpec((None, tq, NSA_HPG * HEAD_DIM), lambda g, b, i: (b, i, g)),
            pl.BlockSpec((None, seq, HEAD_DIM), lambda g, b, i: (b, 0, kcol + g)),
            pl.BlockSpec((None, seq, HEAD_DIM), lambda g, b, i: (b, 0, kcol + NSA_GROUPS + g)),
            pl.BlockSpec((HEAD_DIM, seq), lambda g, b, i: (g, b)),
            pl.BlockSpec((HEAD_DIM, seq), lambda g, b, i: (NSA_GROUPS + g, b)),
            pl.BlockSpec((None, None, nblk, HEAD_DIM), lambda g, b, i: (g, b, 0, 0)),
            pl.BlockSpec((None, None, nblk, HEAD_DIM), lambda g, b, i: (2 + g, b, 0, 0)),
            pl.BlockSpec((None, tq, LANES), lambda g, b, i: (b, i, g3col + g)),
            pl.BlockSpec((seq, LANES), lambda g, b, i: (0, 0)),
            pl.BlockSpec((None, nd, tq, n), lambda g, b, i: (g, 0, 0, 0), pipeline_mode=pl.Buffered(1)),
            pl.BlockSpec((None, NSA_HPG, nblk, tq), lambda g, b, i: (g, 0, 0, i)),
        ],
        out_specs=pl.BlockSpec((None, tq, NSA_HPG * HEAD_DIM), lambda g, b, i: (b, i, g)),
        out_shape=jax.ShapeDtypeStruct((bsz, seq, NSA_HEADS * HEAD_DIM), _F32),
        scratch_shapes=[pltpu.VMEM((tq, n), _F32) for _ in range(5)]
        + [pltpu.VMEM((1, n), _F32) for _ in range(6)]
        + [pltpu.VMEM((HEAD_DIM, n), _F32) for _ in range(2)],
        compiler_params=_cparams(("arbitrary", "arbitrary", "arbitrary")),
        name="nsa_attention",
    )(zb, zb, zb, vt, vt, kvc, kvc, zf, onehot, bt, bc)


def _dilated_kernel(*refs, seq):
    qkv_refs, (tab_ref, o_ref, qs_ref, ks_ref, vs_ref, lse_ref) = refs[:9], refs[9:]
    tq, pw = DIL_TQ, DIL_PW
    c = HEAD_DIM ** -0.5 * LOG2E
    for g, (_, d) in enumerate(DIL_PAIRS):
        q_ref, k_ref, v_ref = qkv_refs[3 * g:3 * g + 3]
        qs_ref[...] = q_ref[...].astype(_F32)
        ks_ref[...] = k_ref[...].astype(_F32)
        vs_ref[...] = v_ref[...].astype(_F32)
        ntile = seq // d // tq
        tab_p = tab_ref[g, :pw, :]
        tab_d = tab_ref[g, pw:, :]

        def scores(idx, d=d, ntile=ntile, tab_d=tab_d, tab_p=tab_p):
            r = idx // ntile
            tile = idx % ntile
            u0 = tile * tq
            rows = pl.ds(u0 * d + r, tq, stride=d)
            rows_p = pl.ds(jnp.maximum(u0 - pw, 0) * d + r, pw, stride=d)
            q = qs_ref[rows, :].astype(_BF)
            s_d = _dot_nt(ks_ref[rows, :].astype(_BF), q) * c + tab_d
            s_p = jnp.where(tile > 0, _dot_nt(ks_ref[rows_p, :].astype(_BF), q) * c + tab_p, NEG)
            return rows, rows_p, s_d, s_p

        def softmax(rows, rows_p, s_d, s_p):
            m = jnp.maximum(jnp.max(s_d, axis=0, keepdims=True), jnp.max(s_p, axis=0, keepdims=True))
            p_d = jnp.exp2(s_d - m)
            p_p = jnp.exp2(s_p - m)
            l = jnp.sum(p_d, axis=0, keepdims=True) + jnp.sum(p_p, axis=0, keepdims=True)
            return rows, rows_p, p_d.astype(_BF), p_p.astype(_BF), l, m + jnp.log2(l)

        def values(rows, rows_p, p_d, p_p, l, lse):
            vt_d = jnp.transpose(vs_ref[rows, :]).astype(_BF)
            vt_p = jnp.transpose(vs_ref[rows_p, :]).astype(_BF)
            acc = _dot(vt_d, p_d) + _dot(vt_p, p_p)
            return rows, jnp.transpose(acc / l), jnp.transpose(jnp.broadcast_to(lse, (HEAD_DIM, tq)))

        def tile_body(it, carry, g=g, scores=scores):
            staged = [scores(it * DIL_UNROLL + j) for j in range(DIL_UNROLL)]
            staged = [softmax(*x) for x in staged]
            for rows, o, lse in [values(*x) for x in staged]:
                if g == 0:
                    o_ref[rows, :] = o
                    lse_ref[rows, :] = lse
                else:
                    lse_old = lse_ref[rows, :]
                    top = jnp.maximum(lse_old, lse)
                    e_old = jnp.exp2(lse_old - top)
                    e_new = jnp.exp2(lse - top)
                    o_ref[rows, :] = (e_old * o_ref[rows, :] + e_new * o) / (e_old + e_new)
                    lse_ref[rows, :] = top + jnp.log2(e_old + e_new)
            return carry

        assert (d * ntile) % DIL_UNROLL == 0
        lax.fori_loop(0, d * ntile // DIL_UNROLL, tile_body, 0)


def _dilated_attention(zd3, dtab, bsz, seq):
    width = DIL_HPG * HEAD_DIM
    col = lambda part: pl.BlockSpec((None, seq, HEAD_DIM), lambda b, h, part=part: (b, 0, part * DIL_HPG + h))
    return pl.pallas_call(
        functools.partial(_dilated_kernel, seq=seq),
        grid=(bsz, DIL_HPG),
        in_specs=[col(part) for part in range(3 * DIL_GROUPS)]
        + [pl.BlockSpec((DIL_GROUPS, None, DIL_PW + DIL_TQ, DIL_TQ), lambda b, h: (0, h, 0, 0))],
        out_specs=pl.BlockSpec((None, seq, HEAD_DIM), lambda b, h: (b, 0, h)),
        out_shape=jax.ShapeDtypeStruct((bsz, seq, width), _F32),
        scratch_shapes=[pltpu.VMEM((seq, HEAD_DIM), _F32) for _ in range(4)],
        compiler_params=_cparams(("parallel", "arbitrary")),
        name="dilated_attention",
    )(*([zd3] * (3 * DIL_GROUPS)), dtab)


def _mla_kernel(qi_tab, ki_tab, q_ref, kn_ref, kp_ref, vt_ref, o_ref, sa_ref, sb_ref, m_ref, l_ref, mxa_ref, mxb_ref,
                acc_ref, *, ntiles):
    tq = MLA_TQ
    c = (MLA_NOPE + MLA_ROPE) ** -0.5 * LOG2E
    _flash_reset(m_ref, l_ref, acc_ref)
    jj = lax.broadcasted_iota(jnp.int32, (tq, tq), 0)
    ii = lax.broadcasted_iota(jnp.int32, (tq, tq), 1)
    buf_a, buf_b = (sa_ref, mxa_ref), (sb_ref, mxb_ref)

    def rows_of(i):
        return pl.ds(pl.multiple_of(i * tq, tq), tq)

    def scores_into(buf, t):
        rows = rows_of(ki_tab[t])
        k = jnp.concatenate([kn_ref[rows, :], kp_ref[rows, :]], axis=1)
        _store_scores(buf, _dot_nt(k, q_ref[rows_of(qi_tab[t]), :]))

    def consume(buf, t, diagonal):
        s = buf[0][...]
        tile_max = buf[1][...]
        if diagonal:
            s = jnp.where(jj <= ii, s, NEG)
            tile_max = None
        _flash_update(s, vt_ref[:, rows_of(ki_tab[t])], c, m_ref, l_ref, acc_ref, tile_max)
        if diagonal:
            o_ref[rows_of(qi_tab[t]), :] = jnp.transpose(acc_ref[...] / l_ref[...])
            _flash_reset(m_ref, l_ref, acc_ref)

    scores_into(buf_a, 0)

    def pair(p, carry):
        t0 = 2 * p
        d0 = ki_tab[t0] == qi_tab[t0]
        d1 = ki_tab[t0 + 1] == qi_tab[t0 + 1]
        for x, y in ((False, False), (True, False), (False, True)):
            @pl.when(jnp.logical_and(d0 == x, d1 == y))
            def _(x=x, y=y):
                scores_into(buf_b, t0 + 1)
                consume(buf_a, t0, x)
                scores_into(buf_a, t0 + 2)
                consume(buf_b, t0 + 1, y)
        return carry

    lax.fori_loop(0, ntiles // 2, pair, 0)


def _mla_attention(q, kn, kpe, vt, bsz, seq):
    tq = MLA_TQ
    pairs = [(qi, ki) for qi in range(seq // tq) for ki in range(qi + 1)]
    ntiles = len(pairs)
    diag = [qi == ki for qi, ki in pairs]
    assert ntiles % 2 == 0 and not any(diag[t] and diag[t + 1] for t in range(0, ntiles, 2))
    pairs.append((0, 0))
    qi_tab = jnp.asarray([p[0] for p in pairs], jnp.int32)
    ki_tab = jnp.asarray([p[1] for p in pairs], jnp.int32)
    smem = pl.BlockSpec(memory_space=pltpu.SMEM)
    return pl.pallas_call(
        functools.partial(_mla_kernel, ntiles=ntiles),
        grid=(bsz, MLA_HEADS),
        in_specs=[
            smem, smem,
            pl.BlockSpec((None, seq, 2 * LANES), lambda b, h: (b, 0, h)),
            pl.BlockSpec((None, seq, MLA_NOPE), lambda b, h: (b, 0, h)),
            pl.BlockSpec((None, seq, LANES), lambda b, h: (b, 0, 0)),
            pl.BlockSpec((MLA_V, seq), lambda b, h: (h, b)),
        ],
        out_specs=pl.BlockSpec((None, seq, MLA_V), lambda b, h: (b, 0, h)),
        out_shape=jax.ShapeDtypeStruct((bsz, seq, MLA_HEADS * MLA_V), _F32),
        scratch_shapes=[pltpu.VMEM((tq, tq), _F32), pltpu.VMEM((tq, tq), _F32)]
        + [pltpu.VMEM((1, tq), _F32) for _ in range(4)] + [pltpu.VMEM((MLA_V, tq), _F32)],
        compiler_params=_cparams(("parallel", "arbitrary")),
        name="mla_attention",
    )(qi_tab, ki_tab, q, kn, kpe, vt)


def _residual_norm_store(x_new, nw_ref, x_out_ref, h_out_ref):
    x_out_ref[...] = x_new
    y = x_new * lax.rsqrt(jnp.mean(x_new * x_new, axis=-1, keepdims=True) + EPS)
    h_out_ref[...] = (y * nw_ref[...]).astype(h_out_ref.dtype)


def _out_even_kernel(x_ref, oa_ref, ga_ref, ob_ref, gb_ref, wa_ref, wb_ref, nw_ref, x_out_ref, h_out_ref):
    mixed_a = (oa_ref[...] * _silu(ga_ref[...])).astype(_BF)
    mixed_b = (ob_ref[...] * _silu(gb_ref[...])).astype(_BF)
    x_new = x_ref[...] + _dot(mixed_a, wa_ref[...]) + _dot(mixed_b, wb_ref[...])
    _residual_norm_store(x_new, nw_ref, x_out_ref, h_out_ref)


def _out_even(x2d, o_a, o_b, zf, w_out, nw_next, zf_cols, h_dtype):
    m, d = x2d.shape
    tm = 256
    wa = A_GATE
    wbd = B_GATE
    row = lambda width, cb=0: pl.BlockSpec((tm, width), lambda i, cb=cb: (i, cb))
    full = lambda r, c: pl.BlockSpec((r, c), lambda i: (0, 0))
    return pl.pallas_call(
        _out_even_kernel,
        grid=(m // tm,),
        in_specs=[row(d), row(wa), row(wa, zf_cols["gate_a"] // wa), row(wbd), row(wbd, zf_cols["gate_b"] // wbd),
                  full(wa, d), full(wbd, d), full(1, d)],
        out_specs=[row(d), row(d)],
        out_shape=[jax.ShapeDtypeStruct((m, d), _F32), jax.ShapeDtypeStruct((m, d), h_dtype)],
        compiler_params=_cparams(("parallel",)),
        name="out_proj_even",
    )(x2d, o_a, zf, o_b, zf, w_out[:wa], w_out[wa:], nw_next.reshape(1, d).astype(_F32))


def _out_odd_kernel(x_ref, oc_ref, gc_ref, w_ref, nw_ref, x_out_ref, h_out_ref):
    mixed = (oc_ref[...] * _silu(gc_ref[...])).astype(_BF)
    x_new = x_ref[...] + _dot(mixed, w_ref[...])
    _residual_norm_store(x_new, nw_ref, x_out_ref, h_out_ref)


def _out_odd(x2d, o_c, z, w_out, nw_next, gate_col_block, h_dtype):
    m, d = x2d.shape
    tm = 256
    width = MLA_HEADS * MLA_V
    row = lambda w, cb=0: pl.BlockSpec((tm, w), lambda i, cb=cb: (i, cb))
    return pl.pallas_call(
        _out_odd_kernel,
        grid=(m // tm,),
        in_specs=[row(d), row(width), row(width, gate_col_block),
                  pl.BlockSpec((width, d), lambda i: (0, 0)), pl.BlockSpec((1, d), lambda i: (0, 0))],
        out_specs=[row(d), row(d)],
        out_shape=[jax.ShapeDtypeStruct((m, d), _F32), jax.ShapeDtypeStruct((m, d), h_dtype)],
        compiler_params=_cparams(("parallel",)),
        name="out_proj_odd",
    )(x2d, o_c, z, w_out, nw_next.reshape(1, d).astype(_F32))


def _bucket_of_distance(n):
    d = np.arange(n)
    max_exact = NUM_BUCKETS // 2
    df = np.maximum(d, 1).astype(np.float32)
    large = max_exact + (np.log(df / max_exact) / math.log(MAX_DISTANCE / max_exact)
                         * (NUM_BUCKETS - max_exact)).astype(np.int32)
    large = np.minimum(large, NUM_BUCKETS - 1)
    return np.where(d < max_exact, d, large).astype(np.int32)


def _toeplitz_tiles(v, n, nd):
    hh = v.shape[0]
    vp = jnp.concatenate([jnp.zeros((hh, n), v.dtype), v, jnp.zeros((hh, n), v.dtype)], axis=1)
    idx = (n - np.arange(2 * n)) % (2 * n)
    tiles = []
    for dl in range(nd):
        w = vp[:, dl * n:dl * n + 2 * n]
        c = w[:, idx]
        flat = jnp.tile(c, (1, n))[:, :n * (2 * n - 1)]
        tiles.append(flat.reshape(hh, n, 2 * n - 1)[:, :, :n])
    return jnp.stack(tiles, axis=0)


def _bias_tables(rel_bias, seq):
    bd = rel_bias.astype(_F32)[_bucket_of_distance(seq)].T
    bd_nsa = bd[:NSA_HEADS]
    nd = seq // NSA_TQ
    bt = _toeplitz_tiles(bd_nsa * LOG2E, NSA_TQ, nd)
    bt = bt.reshape(nd, NSA_GROUPS, NSA_HPG, NSA_TQ, NSA_TQ).transpose(1, 0, 4, 2, 3)
    bt = bt.reshape(NSA_GROUPS, nd, NSA_TQ, NSA_HPG * NSA_TQ)
    nblk = seq // NSA_BLOCK
    off = NSA_BLOCK * (nblk - 1) + NSA_BLOCK - 1
    bdp = jnp.concatenate([jnp.zeros((NSA_HEADS, off), _F32), bd_nsa], axis=1)
    bc = jnp.stack([bdp[:, off - (NSA_BLOCK * n + NSA_BLOCK - 1):off - (NSA_BLOCK * n + NSA_BLOCK - 1) + seq]
                    for n in range(nblk)], axis=1)
    bc = bc.reshape(NSA_GROUPS, NSA_HPG, nblk, seq)
    rows, cols = DIL_TQ, DIL_PW + DIL_TQ
    period = rows + cols
    dtabs = []
    for g, (w, d) in enumerate(DIL_PAIRS):
        heads = bd[NSA_HEADS + g * DIL_HPG:NSA_HEADS + (g + 1) * DIL_HPG] * LOG2E
        m = w // d + 1
        assert m == DIL_PW + 1
        by_offset = heads[:, 0:m * d:d]
        cyc = jnp.concatenate([by_offset[:, ::-1], jnp.full((DIL_HPG, period - m), NEG, _F32)], axis=1)
        flat = jnp.tile(cyc, (1, rows))[:, :rows * (period - 1)]
        by_query = flat.reshape(DIL_HPG, rows, period - 1)[:, :, :cols]
        dtabs.append(by_query.transpose(0, 2, 1))
    return bt, bc, jnp.stack(dtabs, axis=0)


def _rope_tables(seq):
    half = MLA_ROPE // 2
    freqs = 1.0 / (ROPE_THETA ** (jnp.arange(half, dtype=_F32) / half))
    ang = jnp.arange(seq).astype(_F32)[:, None] * freqs[None, :]
    zeros = jnp.zeros((seq, LANES - MLA_ROPE), _F32)
    cos = jnp.concatenate([jnp.cos(ang), jnp.cos(ang), zeros], axis=1)
    sin = jnp.concatenate([jnp.sin(ang), jnp.sin(ang), zeros], axis=1)
    return cos, sin


def _rot_cols(w):
    half = w.shape[-1] // 2
    return jnp.concatenate([-w[..., half:], w[..., :half]], axis=-1)


ZF_COLS = {"gate_a": 0, "gate_b": A_GATE, "kcvc": A_GATE + B_GATE, "g3": A_GATE + B_GATE + 4 * HEAD_DIM}


def _even_weights(w_in):
    o = 0
    q_a = w_in[:, o:o + A_Q]; o += A_Q
    kv_a = w_in[:, o:o + A_KV]; o += A_KV
    g3 = w_in[:, o:o + A_G3]; o += A_G3
    gate_a = w_in[:, o:o + A_GATE]; o += A_GATE
    qkv_b = w_in[:, o:o + B_QKV]; o += B_QKV
    gate_b = w_in[:, o:o + B_GATE]
    gw = NSA_GROUPS * HEAD_DIM
    kc_vc, ks, vs, kw, vw = (kv_a[:, :2 * gw], kv_a[:, 2 * gw:3 * gw], kv_a[:, 3 * gw:4 * gw],
                             kv_a[:, 4 * gw:5 * gw], kv_a[:, 5 * gw:6 * gw])
    wb = jnp.concatenate([q_a, ks, kw], axis=1).astype(_BF)
    wvt = jnp.concatenate([vs, vw], axis=1).T.astype(_BF)
    dw = DIL_HPG * HEAD_DIM
    nq = DIL_HEADS * HEAD_DIM
    wd = jnp.concatenate([qkv_b[:, part * nq + g * dw:part * nq + (g + 1) * dw]
                          for g in range(DIL_GROUPS) for part in range(3)], axis=1).astype(_BF)
    g3_blocks = []
    for g in range(NSA_GROUPS):
        cols = [(g * NSA_HPG + h) * 3 + j for j in range(3) for h in range(NSA_HPG)]
        blk = g3[:, np.asarray(cols)]
        g3_blocks.append(jnp.pad(blk, ((0, 0), (0, LANES - len(cols)))))
    wf = jnp.concatenate([gate_a, gate_b, kc_vc] + g3_blocks, axis=1).astype(_BF)
    return wb, wvt, wd, wf


def _even_layer(x2d, h, w_in, cmp_pos, cmp_w1, cmp_w2, w_out, nw_next, tables, bsz, seq, h_dtype):
    bt, bc, dtab, onehot = tables
    wb, wvt, wd, wf = _even_weights(w_in)
    zb = _matmul(h, wb, _BF)
    vt = _matmul_nt(wvt, h, _BF)
    zd = _matmul(h, wd, _BF)
    zf = _matmul(h, wf, _F32)
    nblk = seq // NSA_BLOCK
    kcvc = zf[:, ZF_COLS["kcvc"]:ZF_COLS["kcvc"] + 4 * HEAD_DIM]
    kcvc_t = kcvc.reshape(bsz * nblk, NSA_BLOCK, 4, HEAD_DIM).transpose(2, 1, 0, 3)
    kvc = _compress(kcvc_t, cmp_pos.astype(_F32), cmp_w1.astype(_BF), cmp_w2.astype(_BF))
    kvc = kvc.reshape(4, bsz, nblk, HEAD_DIM)
    zf3 = zf.reshape(bsz, seq, zf.shape[-1])
    o_a = _nsa_attention(zb.reshape(bsz, seq, -1), vt, zf3, kvc, onehot, bt, bc, bsz, seq)
    o_b = _dilated_attention(zd.reshape(bsz, seq, -1), dtab, bsz, seq)
    return _out_even(x2d, o_a.reshape(bsz * seq, -1), o_b.reshape(bsz * seq, -1), zf, w_out.astype(_BF), nw_next,
                     ZF_COLS, h_dtype)


def _odd_layer(x2d, h, w_in, q_norm, w_qb, kv_norm, w_kvb, w_out, nw_next, rope, bsz, seq, h_dtype):
    cos, sin = rope
    o = 0
    w_cq = w_in[:, o:o + MLA_Q_RANK]; o += MLA_Q_RANK
    w_ckv = w_in[:, o:o + MLA_KV_RANK]; o += MLA_KV_RANK
    w_kpe = w_in[:, o:o + MLA_ROPE]; o += MLA_ROPE
    w_gate = w_in[:, o:]
    w1 = jnp.concatenate([w_cq, w_ckv, w_gate], axis=1).astype(_BF)
    z = _matmul(h, w1, _F32)
    zk = _matmul(h, jnp.concatenate([w_kpe, _rot_cols(w_kpe)], axis=1).astype(_BF), _F32)
    gate_width = MLA_HEADS * MLA_V
    cqn = _rmsnorm(z, q_norm, MLA_Q_RANK, 0, _BF)
    ckvn = _rmsnorm(z, kv_norm, MLA_KV_RANK, MLA_Q_RANK // MLA_KV_RANK, _BF)
    kpe = _rope_cols(zk, 0, cos, sin, seq)
    wq = w_qb.reshape(MLA_Q_RANK, MLA_HEADS, MLA_NOPE + MLA_ROPE)
    wq_pe = wq[:, :, MLA_NOPE:]
    wq = jnp.concatenate([wq[:, :, :MLA_NOPE], wq_pe, _rot_cols(wq_pe)], axis=-1)
    q = _matmul_rope(cqn, wq.reshape(MLA_Q_RANK, MLA_HEADS * 2 * LANES).astype(_BF), cos, sin, seq)
    wkv = w_kvb.reshape(MLA_KV_RANK, MLA_HEADS, MLA_NOPE + MLA_V)
    wk = wkv[:, :, :MLA_NOPE].reshape(MLA_KV_RANK, -1).astype(_BF)
    wv_t = wkv[:, :, MLA_NOPE:].reshape(MLA_KV_RANK, -1).T.astype(_BF)
    kn = _matmul(ckvn, wk, _BF)
    vt = _matmul_nt(wv_t, ckvn, _BF)
    o_c = _mla_attention(q.reshape(bsz, seq, -1), kn.reshape(bsz, seq, -1), kpe.reshape(bsz, seq, LANES),
                         vt, bsz, seq)
    gate_block = (MLA_Q_RANK + MLA_KV_RANK) // gate_width
    return _out_odd(x2d, o_c.reshape(bsz * seq, -1), z, w_out.astype(_BF), nw_next, gate_block, h_dtype)


def kernel(x, rel_bias, norm_w, final_norm_w, ev_w_in, nsa_cmp_pos, nsa_cmp_w1, nsa_cmp_w2, ev_w_out,
           od_w_in, mla_q_norm, mla_w_qb, mla_kv_norm, mla_w_kvb, od_w_out):
    bsz, seq, d = x.shape
    depth = norm_w.shape[0]
    assert seq % MLA_TQ == 0 and seq % (DIL_TQ * DIL_PAIRS[-1][1]) == 0 and seq // NSA_BLOCK <= LANES
    assert NSA_WINDOW == 2 * NSA_TQ
    bt, bc, dtab = _bias_tables(rel_bias, seq)
    blk_id = np.arange(seq)[:, None] // NSA_BLOCK
    onehot = jnp.asarray(blk_id == np.arange(LANES)[None, :], dtype=_BF)
    tables = (bt, bc, dtab, onehot)
    rope = _rope_tables(seq)
    x2d = x.reshape(bsz * seq, d).astype(_F32)
    h = _rmsnorm(x2d, norm_w[0], d, 0, _BF)
    for l in range(depth):
        last = l == depth - 1
        nw_next = final_norm_w if last else norm_w[l + 1]
        h_dtype = _F32 if last else _BF
        i = l // 2
        if l % 2 == 0:
            x2d, h = _even_layer(x2d, h, ev_w_in[i], nsa_cmp_pos[i], nsa_cmp_w1[i], nsa_cmp_w2[i],
                                 ev_w_out[i], nw_next, tables, bsz, seq, h_dtype)
        else:
            x2d, h = _odd_layer(x2d, h, od_w_in[i], mla_q_norm[i], mla_w_qb[i], mla_kv_norm[i],
                                mla_w_kvb[i], od_w_out[i], nw_next, rope, bsz, seq, h_dtype)
    return h.reshape(bsz, seq, d)
```

```python
import functools
import math

import numpy as np
import jax
import jax.numpy as jnp
from jax import lax
from jax.experimental import pallas as pl
from jax.experimental.pallas import tpu as pltpu

HEAD_DIM = 128
EPS = 1e-6
NEG = -1e30
NUM_BUCKETS = 32
MAX_DISTANCE = 2048
NSA_HEADS = 8
NSA_GROUPS = 2
NSA_HPG = 4
NSA_BLOCK = 64
NSA_TOPK = 16
NSA_WINDOW = 512
DIL_PAIRS = ((128, 1), (512, 4), (2048, 16))
DIL_GROUPS = 3
DIL_HPG = 4
DIL_HEADS = 12
MLA_HEADS = 16
MLA_Q_RANK = 1536
MLA_KV_RANK = 512
MLA_NOPE = 128
MLA_ROPE = 64
MLA_V = 128
ROPE_THETA = 10000.0

A_Q = NSA_HEADS * HEAD_DIM
A_KV = 6 * NSA_GROUPS * HEAD_DIM
A_G3 = 3 * NSA_HEADS
A_GATE = NSA_HEADS * HEAD_DIM
B_QKV = 3 * DIL_HEADS * HEAD_DIM
B_GATE = DIL_HPG * HEAD_DIM

LANES = 128
SUBLANES = 8
VMEM_LIMIT = 48 * 1024 * 1024

NSA_TQ = 256
DIL_TQ = 256
DIL_PW = 128
DIL_UNROLL = 4
MLA_TQ = 512
LOG2E = math.log2(math.e)

_BF = jnp.bfloat16
_F32 = jnp.float32


def _cparams(sem):
    return pltpu.CompilerParams(dimension_semantics=sem, vmem_limit_bytes=VMEM_LIMIT)


def _dot(a, b):
    return jnp.dot(a, b, preferred_element_type=_F32)


def _dot_nt(a, b):
    return lax.dot_general(a, b, (((1,), (1,)), ((), ())), preferred_element_type=_F32)


def _silu(x):
    return x * (1.0 / (1.0 + jnp.exp(-x)))


def _sigmoid(x):
    return 1.0 / (1.0 + jnp.exp(-x))


def _pick_tile(n, candidates):
    for c in candidates:
        if n % c == 0:
            return c
    return n


def _rmsnorm_kernel(x_ref, w_ref, o_ref):
    x = x_ref[...]
    y = x * lax.rsqrt(jnp.mean(x * x, axis=-1, keepdims=True) + EPS)
    o_ref[...] = (y * w_ref[...]).astype(o_ref.dtype)


def _rmsnorm(x2d, w, width, col_block, out_dtype):
    m = x2d.shape[0]
    tm = _pick_tile(m, (512, 256, 128))
    return pl.pallas_call(
        _rmsnorm_kernel,
        grid=(m // tm,),
        in_specs=[pl.BlockSpec((tm, width), lambda i: (i, col_block)),
                  pl.BlockSpec((1, width), lambda i: (0, 0))],
        out_specs=pl.BlockSpec((tm, width), lambda i: (i, 0)),
        out_shape=jax.ShapeDtypeStruct((m, width), out_dtype),
        compiler_params=_cparams(("parallel",)),
        name="rmsnorm",
    )(x2d, w.reshape(1, width).astype(_F32))


def _matmul_kernel(a_ref, w_ref, o_ref):
    o_ref[...] = _dot(a_ref[...], w_ref[...]).astype(o_ref.dtype)


def _normed_lhs(a_ref, nw_ref, an_ref):
    @pl.when(pl.program_id(1) == 0)
    def _():
        x = a_ref[...]
        y = x * lax.rsqrt(jnp.mean(x * x, axis=-1, keepdims=True) + EPS)
        an_ref[...] = (y * nw_ref[...]).astype(an_ref.dtype)
    return an_ref[...]


def _norm_matmul_kernel(a_ref, nw_ref, w_ref, o_ref, an_ref):
    o_ref[...] = _dot(_normed_lhs(a_ref, nw_ref, an_ref), w_ref[...]).astype(o_ref.dtype)


def _lhs_specs(a, norm, tm):
    if norm is None:
        k = a.shape[1]
        return k, [pl.BlockSpec((tm, k), lambda i, j: (i, 0))], [], []
    nw, k, cb = norm
    specs = [pl.BlockSpec((tm, k), lambda i, j: (i, cb)), pl.BlockSpec((1, k), lambda i, j: (0, 0))]
    return k, specs, [nw.reshape(1, k).astype(_F32)], [pltpu.VMEM((tm, k), _BF)]


def _matmul(a, w, out_dtype, norm=None):
    m = a.shape[0]
    n = w.shape[1]
    tm = _pick_tile(m, (1024, 512, 256, 128))
    tn = _pick_tile(n, (1024, 768, 512, 384, 256, 128))
    k, a_specs, a_args, scratch = _lhs_specs(a, norm, tm)
    return pl.pallas_call(
        _matmul_kernel if norm is None else _norm_matmul_kernel,
        grid=(m // tm, n // tn),
        in_specs=a_specs + [pl.BlockSpec((k, tn), lambda i, j: (0, j))],
        out_specs=pl.BlockSpec((tm, tn), lambda i, j: (i, j)),
        out_shape=jax.ShapeDtypeStruct((m, n), out_dtype),
        scratch_shapes=scratch,
        compiler_params=_cparams(("parallel", "arbitrary")),
        name="matmul" if norm is None else "norm_matmul",
    )(a, *a_args, w)


def _matmul_nt_kernel(w_ref, a_ref, o_ref):
    o_ref[...] = _dot_nt(w_ref[...], a_ref[...]).astype(o_ref.dtype)


def _norm_matmul_nt_kernel(w_ref, a_ref, nw_ref, o_ref, an_ref):
    o_ref[...] = _dot_nt(w_ref[...], _normed_lhs(a_ref, nw_ref, an_ref)).astype(o_ref.dtype)


def _matmul_nt(w_t, a, out_dtype, norm=None):
    n = w_t.shape[0]
    m = a.shape[0]
    tm = _pick_tile(m, (1024, 512, 256, 128))
    tn = _pick_tile(n, (1024, 512, 256, 128))
    k, a_specs, a_args, scratch = _lhs_specs(a, norm, tm)
    return pl.pallas_call(
        _matmul_nt_kernel if norm is None else _norm_matmul_nt_kernel,
        grid=(m // tm, n // tn),
        in_specs=[pl.BlockSpec((tn, k), lambda i, j: (j, 0))] + a_specs,
        out_specs=pl.BlockSpec((tn, tm), lambda i, j: (j, i)),
        out_shape=jax.ShapeDtypeStruct((n, m), out_dtype),
        scratch_shapes=scratch,
        compiler_params=_cparams(("parallel", "arbitrary")),
        name="matmul_nt" if norm is None else "norm_matmul_nt",
    )(w_t, a, *a_args)


def _rope_chunk(chunk, cos, sin):
    return chunk * cos + pltpu.roll(chunk, 64, 1) * sin


def _matmul_rope_kernel(a_ref, nw_ref, w_ref, cos_ref, sin_ref, o_ref, an_ref, *, heads_per_tile):
    acc = _dot(_normed_lhs(a_ref, nw_ref, an_ref), w_ref[...])
    cos = cos_ref[...]
    sin = sin_ref[...]
    for h in range(heads_per_tile):
        base = h * 2 * LANES
        o_ref[:, base:base + LANES] = acc[:, base:base + LANES].astype(o_ref.dtype)
        o_ref[:, base + LANES:base + 2 * LANES] = _rope_chunk(
            acc[:, base + LANES:base + 2 * LANES], cos, sin).astype(o_ref.dtype)


def _matmul_rope(a, norm, w, cos, sin, seq):
    m = a.shape[0]
    n = w.shape[1]
    tm = _pick_tile(seq, (1024, 512, 256, 128))
    tn = 1024
    tpb = seq // tm
    k, a_specs, a_args, scratch = _lhs_specs(a, norm, tm)
    return pl.pallas_call(
        functools.partial(_matmul_rope_kernel, heads_per_tile=tn // (2 * LANES)),
        grid=(m // tm, n // tn),
        in_specs=a_specs + [pl.BlockSpec((k, tn), lambda i, j: (0, j)),
                            pl.BlockSpec((tm, LANES), lambda i, j: (i % tpb, 0)),
                            pl.BlockSpec((tm, LANES), lambda i, j: (i % tpb, 0))],
        out_specs=pl.BlockSpec((tm, tn), lambda i, j: (i, j)),
        out_shape=jax.ShapeDtypeStruct((m, n), _BF),
        scratch_shapes=scratch,
        compiler_params=_cparams(("parallel", "arbitrary")),
        name="matmul_rope",
    )(a, *a_args, w, cos, sin)


def _rope_cols_kernel(x_ref, cos_ref, sin_ref, o_ref):
    o_ref[...] = _rope_chunk(x_ref[...], cos_ref[...], sin_ref[...]).astype(o_ref.dtype)


def _rope_cols(z2d, col_block, cos, sin, seq):
    m = z2d.shape[0]
    tm = _pick_tile(seq, (512, 256, 128))
    tpb = seq // tm
    return pl.pallas_call(
        _rope_cols_kernel,
        grid=(m // tm,),
        in_specs=[pl.BlockSpec((tm, LANES), lambda i: (i, col_block)),
                  pl.BlockSpec((tm, LANES), lambda i: (i % tpb, 0)),
                  pl.BlockSpec((tm, LANES), lambda i: (i % tpb, 0))],
        out_specs=pl.BlockSpec((tm, LANES), lambda i: (i, 0)),
        out_shape=jax.ShapeDtypeStruct((m, LANES), _BF),
        compiler_params=_cparams(("parallel",)),
        name="rope_kpe",
    )(z2d, cos, sin)


def _compress_kernel(blk_ref, pos_ref, w1_ref, w2_ref, o_ref, acc_ref, *, lt):
    li = pl.program_id(1)

    @pl.when(li == 0)
    def _():
        acc_ref[...] = jnp.zeros_like(acc_ref)

    acc = acc_ref[...]
    for l in range(lt):
        a = (blk_ref[l] + pos_ref[l:l + 1, :]).astype(_BF)
        acc = acc + _dot(a, w1_ref[l])
    acc_ref[...] = acc

    @pl.when(li == pl.num_programs(1) - 1)
    def _():
        hid = _silu(acc_ref[...]).astype(_BF)
        o_ref[...] = _dot(hid, w2_ref[...])


def _compress(kcvc_t, pos, w1, w2):
    _, L, R, _ = kcvc_t.shape
    lt = 16
    return pl.pallas_call(
        functools.partial(_compress_kernel, lt=lt),
        grid=(4, L // lt),
        in_specs=[pl.BlockSpec((None, lt, R, HEAD_DIM), lambda c, l: (c, l, 0, 0)),
                  pl.BlockSpec((None, lt, HEAD_DIM), lambda c, l: (c // 2, l, 0)),
                  pl.BlockSpec((None, lt, HEAD_DIM, HEAD_DIM), lambda c, l: (c // 2, l, 0, 0)),
                  pl.BlockSpec((None, HEAD_DIM, HEAD_DIM), lambda c, l: (c // 2, 0, 0))],
        out_specs=pl.BlockSpec((None, R, HEAD_DIM), lambda c, l: (c, 0, 0)),
        out_shape=jax.ShapeDtypeStruct((4, R, HEAD_DIM), _F32),
        scratch_shapes=[pltpu.VMEM((R, HEAD_DIM), _F32)],
        compiler_params=_cparams(("parallel", "arbitrary")),
        name="nsa_compress",
    )(kcvc_t, pos, w1.reshape(2, L, HEAD_DIM, HEAD_DIM), w2)


def _flash_reset(m_ref, l_ref, acc_ref):
    m_ref[...] = jnp.full(m_ref.shape, -jnp.inf, _F32)
    l_ref[...] = jnp.zeros(l_ref.shape, _F32)
    acc_ref[...] = jnp.zeros(acc_ref.shape, _F32)


def _flash_update(s, vt, c, m_ref, l_ref, acc_ref, tile_max=None):
    m_prev = m_ref[...]
    if tile_max is None:
        tile_max = jnp.max(s, axis=0, keepdims=True)
    m_new = jnp.maximum(m_prev, tile_max)
    if c is None:
        alpha = jnp.exp2(m_prev - m_new)
        p = jnp.exp2(s - m_new)
    else:
        alpha = jnp.exp2((m_prev - m_new) * c)
        p = jnp.exp2((s - m_new) * c)
    l_ref[...] = alpha * l_ref[...] + jnp.sum(p, axis=0, keepdims=True)
    acc_ref[...] = alpha * acc_ref[...] + _dot(vt, p.astype(_BF))
    m_ref[...] = m_new


def _store_scores(buf, s):
    s_ref, mx_ref = buf
    s_ref[...] = s
    mx_ref[...] = jnp.max(s, axis=0, keepdims=True)


def _causal_flash(n_off, scores_into, consume, buf_a, buf_b, primed=False):
    if not primed:
        scores_into(buf_a, 0)

    def pair(p, carry):
        k0 = 2 * p
        scores_into(buf_b, k0 + 1)
        consume(buf_a, k0, False)
        scores_into(buf_a, k0 + 2)
        consume(buf_b, k0 + 1, False)
        return carry

    lax.fori_loop(0, n_off // 2, pair, 0)

    @pl.when(n_off % 2 == 1)
    def _():
        scores_into(buf_b, n_off)
        consume(buf_a, n_off - 1, False)
        consume(buf_b, n_off, True)

    @pl.when(n_off % 2 == 0)
    def _():
        consume(buf_a, n_off, True)


def _nsa_kernel(q_ref, ks_ref, kw_ref, vst_ref, vwt_ref, kc_ref, vc_ref, g3_ref, oh_ref, bt_ref, bc_ref,
                o_ref, sa_ref, sb_ref, w0_ref, w1_ref, w2_ref, m_ref, l_ref, mw_ref, lw_ref, mxa_ref, mxb_ref,
                acc_s_ref, acc_w_ref, *, seq):
    tq = NSA_TQ
    hp = NSA_HPG
    n = hp * tq
    nblk = seq // NSA_BLOCK
    qi = pl.program_id(2)
    scale = HEAD_DIM ** -0.5
    c = scale * LOG2E
    t0 = qi * tq

    qs = jnp.concatenate([q_ref[:, h * HEAD_DIM:(h + 1) * HEAD_DIM] for h in range(hp)], axis=0)

    def rows_of(kt):
        return pl.ds(pl.multiple_of(kt * tq, tq), tq)

    for delta, dst_ref in enumerate((w0_ref, w1_ref, w2_ref)):
        dst_ref[...] = _dot_nt(kw_ref[rows_of(jnp.maximum(qi - delta, 0)), :], qs) * c + bt_ref[delta]

    kc = kc_ref[...].astype(_BF)
    s_c = _dot_nt(kc, qs) * scale
    s_c = s_c + jnp.concatenate([bc_ref[h] for h in range(hp)], axis=1)
    n_col = lax.broadcasted_iota(jnp.int32, (nblk, tq), 0)
    t_row = t0 + lax.broadcasted_iota(jnp.int32, (nblk, tq), 1)
    n_all = lax.broadcasted_iota(jnp.int32, (nblk, n), 0)
    t_all = t0 + (lax.broadcasted_iota(jnp.int32, (nblk, n), 1) & (tq - 1))
    valid = t_all >= n_all * NSA_BLOCK + (NSA_BLOCK - 1)
    s_c = jnp.where(valid, s_c, NEG)
    m_c = jnp.max(s_c, axis=0, keepdims=True)
    p_c = jnp.where(valid, jnp.exp(s_c - m_c), 0.0)
    l_c = jnp.sum(p_c, axis=0, keepdims=True)
    p_c = p_c / jnp.where(l_c > 0.0, l_c, 1.0)
    imp = p_c[:, 0:tq]
    for h in range(1, hp):
        imp = imp + p_c[:, h * tq:(h + 1) * tq]

    pad_rows = LANES - nblk
    vc_pad = jnp.concatenate([vc_ref[...], jnp.zeros((pad_rows, HEAD_DIM), _F32)], axis=0)
    vc_t = jnp.transpose(vc_pad).astype(_BF)
    p_pad = jnp.concatenate([p_c, jnp.zeros((pad_rows, n), _F32)], axis=0).astype(_BF)
    o_cmp = _dot(vc_t, p_pad)

    cur = jnp.right_shift(t_row, int(math.log2(NSA_BLOCK)))
    forced = (n_col == 0) | (n_col == cur) | (n_col == cur - 1)
    impv = jnp.where(forced, jnp.inf, jnp.where(n_col > cur, -jnp.inf, imp))
    rank = jnp.zeros((nblk, tq), _F32)
    sub_row = lax.broadcasted_iota(jnp.int32, (SUBLANES, tq), 0)
    for i in range(nblk):
        row = impv[i:i + 1, :]
        lo = i // SUBLANES * SUBLANES
        parts = []
        if lo > 0:
            parts.append(jnp.where(row > impv[:lo], 1.0, 0.0))
        mid = impv[lo:lo + SUBLANES]
        parts.append(jnp.where(sub_row > i - lo,
                               jnp.where(row >= mid, 1.0, 0.0), jnp.where(row > mid, 1.0, 0.0)))
        if lo + SUBLANES < nblk:
            parts.append(jnp.where(row >= impv[lo + SUBLANES:], 1.0, 0.0))
        rank = rank + jnp.concatenate(parts, axis=0)
    sel = (rank < float(min(NSA_TOPK, nblk))) & (n_col <= cur)
    pen = jnp.where(sel, 0.0, NEG)
    pen = jnp.concatenate([pen, jnp.full((pad_rows, tq), NEG, _F32)], axis=0)
    pen_t = jnp.transpose(pen).astype(_BF)
    q_aug = jnp.concatenate([qs, jnp.concatenate([pen_t] * hp, axis=0)], axis=1)

    jj = lax.broadcasted_iota(jnp.int32, (tq, n), 0)
    ii = lax.broadcasted_iota(jnp.int32, (tq, n), 1) & (tq - 1)

    def slc_scores(buf, ki):
        rows = rows_of(ki)
        k_aug = jnp.concatenate([ks_ref[rows, :], oh_ref[rows, :]], axis=1)
        _store_scores(buf, _dot_nt(k_aug, q_aug) * c + bt_ref[qi - ki])

    def slc_consume(buf, ki, diagonal):
        s = buf[0][...]
        tile_max = buf[1][...]
        if diagonal:
            s = jnp.where(jj <= ii, s, NEG)
            tile_max = None
        _flash_update(s, vst_ref[:, rows_of(ki)], None, m_ref, l_ref, acc_s_ref, tile_max)

    slc_scores((sa_ref, mxa_ref), 0)

    def win_consume(src_ref, delta, keep):
        s = jnp.where(keep, src_ref[...], NEG)
        _flash_update(s, vwt_ref[:, rows_of(jnp.maximum(qi - delta, 0))], None, mw_ref, lw_ref, acc_w_ref)

    _flash_reset(mw_ref, lw_ref, acc_w_ref)
    win_consume(w0_ref, 0, jj <= ii)
    win_consume(w1_ref, 1, qi >= 1)
    win_consume(w2_ref, 2, (jj > ii) & (qi >= 2))
    o_win = acc_w_ref[...] / lw_ref[...]

    _flash_reset(m_ref, l_ref, acc_s_ref)
    _causal_flash(qi, slc_scores, slc_consume, (sa_ref, mxa_ref), (sb_ref, mxb_ref), primed=True)
    o_slc = acc_s_ref[...] / l_ref[...]

    g_t = jnp.transpose(_sigmoid(g3_ref[...]))
    for h in range(hp):
        cols = slice(h * tq, (h + 1) * tq)
        out_t = (g_t[0 * hp + h:0 * hp + h + 1, :] * o_cmp[:, cols]
                 + g_t[1 * hp + h:1 * hp + h + 1, :] * o_slc[:, cols]
                 + g_t[2 * hp + h:2 * hp + h + 1, :] * o_win[:, cols])
        o_ref[:, h * HEAD_DIM:(h + 1) * HEAD_DIM] = jnp.transpose(out_t).astype(o_ref.dtype)


def _nsa_attention(zb, vt, zf, kvc, onehot, bt, bc, bsz, seq):
    tq = NSA_TQ
    n = NSA_HPG * tq
    nblk = seq // NSA_BLOCK
    nd = seq // tq
    kcol = A_Q // HEAD_DIM
    g3col = ZF_COLS["g3"] // LANES
    return pl.pallas_call(
        functools.partial(_nsa_kernel, seq=seq),
        grid=(NSA_GROUPS, bsz, seq // tq),
        in_specs=[
            pl.BlockSpec((None, tq, NSA_HPG * HEAD_DIM), lambda g, b, i: (b, i, g)),
            pl.BlockSpec((None, seq, HEAD_DIM), lambda g, b, i: (b, 0, kcol + g)),
            pl.BlockSpec((None, seq, HEAD_DIM), lambda g, b, i: (b, 0, kcol + NSA_GROUPS + g)),
            pl.BlockSpec((HEAD_DIM, seq), lambda g, b, i: (g, b)),
            pl.BlockSpec((HEAD_DIM, seq), lambda g, b, i: (NSA_GROUPS + g, b)),
            pl.BlockSpec((None, None, nblk, HEAD_DIM), lambda g, b, i: (g, b, 0, 0)),
            pl.BlockSpec((None, None, nblk, HEAD_DIM), lambda g, b, i: (2 + g, b, 0, 0)),
            pl.BlockSpec((None, tq, LANES), lambda g, b, i: (b, i, g3col + g)),
            pl.BlockSpec((seq, LANES), lambda g, b, i: (0, 0)),
            pl.BlockSpec((None, nd, tq, n), lambda g, b, i: (g, 0, 0, 0), pipeline_mode=pl.Buffered(1)),
            pl.BlockSpec((None, NSA_HPG, nblk, tq), lambda g, b, i: (g, 0, 0, i)),
        ],
        out_specs=pl.BlockSpec((None, tq, NSA_HPG * HEAD_DIM), lambda g, b, i: (b, i, g)),
        out_shape=jax.ShapeDtypeStruct((bsz, seq, NSA_HEADS * HEAD_DIM), _BF),
        scratch_shapes=[pltpu.VMEM((tq, n), _F32) for _ in range(5)]
        + [pltpu.VMEM((1, n), _F32) for _ in range(6)]
        + [pltpu.VMEM((HEAD_DIM, n), _F32) for _ in range(2)],
        compiler_params=_cparams(("arbitrary", "arbitrary", "arbitrary")),
        name="nsa_attention",
    )(zb, zb, zb, vt, vt, kvc, kvc, zf, onehot, bt, bc)


def _dilated_kernel(*refs, seq):
    qkv_refs, (tab_ref, out_ref, qs_ref, ks_ref, vs_ref, lse_ref, o_ref) = refs[:9], refs[9:]
    tq, pw = DIL_TQ, DIL_PW
    c = HEAD_DIM ** -0.5 * LOG2E
    for g, (_, d) in enumerate(DIL_PAIRS):
        q_ref, k_ref, v_ref = qkv_refs[3 * g:3 * g + 3]
        qs_ref[...] = q_ref[...].astype(_F32)
        ks_ref[...] = k_ref[...].astype(_F32)
        vs_ref[...] = v_ref[...].astype(_F32)
        ntile = seq // d // tq
        tab_p = tab_ref[g, :pw, :]
        tab_d = tab_ref[g, pw:, :]

        def scores(idx, d=d, ntile=ntile, tab_d=tab_d, tab_p=tab_p):
            r = idx // ntile
            tile = idx % ntile
            u0 = tile * tq
            rows = pl.ds(u0 * d + r, tq, stride=d)
            rows_p = pl.ds(jnp.maximum(u0 - pw, 0) * d + r, pw, stride=d)
            q = qs_ref[rows, :].astype(_BF)
            s_d = _dot_nt(ks_ref[rows, :].astype(_BF), q) * c + tab_d
            s_p = jnp.where(tile > 0, _dot_nt(ks_ref[rows_p, :].astype(_BF), q) * c + tab_p, NEG)
            return rows, rows_p, s_d, s_p

        def softmax(rows, rows_p, s_d, s_p):
            m = jnp.maximum(jnp.max(s_d, axis=0, keepdims=True), jnp.max(s_p, axis=0, keepdims=True))
            p_d = jnp.exp2(s_d - m)
            p_p = jnp.exp2(s_p - m)
            l = jnp.sum(p_d, axis=0, keepdims=True) + jnp.sum(p_p, axis=0, keepdims=True)
            return rows, rows_p, p_d.astype(_BF), p_p.astype(_BF), l, m + jnp.log2(l)

        def values(rows, rows_p, p_d, p_p, l, lse):
            vt_d = jnp.transpose(vs_ref[rows, :]).astype(_BF)
            vt_p = jnp.transpose(vs_ref[rows_p, :]).astype(_BF)
            acc = _dot(vt_d, p_d) + _dot(vt_p, p_p)
            return rows, jnp.transpose(acc / l), jnp.transpose(jnp.broadcast_to(lse, (HEAD_DIM, tq)))

        def tile_body(it, carry, g=g, scores=scores):
            staged = [scores(it * DIL_UNROLL + j) for j in range(DIL_UNROLL)]
            staged = [softmax(*x) for x in staged]
            for rows, o, lse in [values(*x) for x in staged]:
                if g == 0:
                    o_ref[rows, :] = o
                    lse_ref[rows, :] = lse
                else:
                    lse_old = lse_ref[rows, :]
                    top = jnp.maximum(lse_old, lse)
                    e_old = jnp.exp2(lse_old - top)
                    e_new = jnp.exp2(lse - top)
                    o_ref[rows, :] = (e_old * o_ref[rows, :] + e_new * o) / (e_old + e_new)
                    lse_ref[rows, :] = top + jnp.log2(e_old + e_new)
            return carry

        assert (d * ntile) % DIL_UNROLL == 0
        lax.fori_loop(0, d * ntile // DIL_UNROLL, tile_body, 0)
    out_ref[...] = o_ref[...].astype(out_ref.dtype)


def _dilated_attention(zd3, dtab, bsz, seq):
    width = DIL_HPG * HEAD_DIM
    col = lambda part: pl.BlockSpec((None, seq, HEAD_DIM), lambda b, h, part=part: (b, 0, part * DIL_HPG + h))
    return pl.pallas_call(
        functools.partial(_dilated_kernel, seq=seq),
        grid=(bsz, DIL_HPG),
        in_specs=[col(part) for part in range(3 * DIL_GROUPS)]
        + [pl.BlockSpec((DIL_GROUPS, None, DIL_PW + DIL_TQ, DIL_TQ), lambda b, h: (0, h, 0, 0))],
        out_specs=pl.BlockSpec((None, seq, HEAD_DIM), lambda b, h: (b, 0, h)),
        out_shape=jax.ShapeDtypeStruct((bsz, seq, width), _BF),
        scratch_shapes=[pltpu.VMEM((seq, HEAD_DIM), _F32) for _ in range(5)],
        compiler_params=_cparams(("parallel", "arbitrary")),
        name="dilated_attention",
    )(*([zd3] * (3 * DIL_GROUPS)), dtab)


def _mla_kernel(qi_tab, ki_tab, q_ref, kn_ref, kp_ref, vt_ref, o_ref, sa_ref, sb_ref, m_ref, l_ref, mxa_ref, mxb_ref,
                acc_ref, *, ntiles):
    tq = MLA_TQ
    c = (MLA_NOPE + MLA_ROPE) ** -0.5 * LOG2E
    _flash_reset(m_ref, l_ref, acc_ref)
    jj = lax.broadcasted_iota(jnp.int32, (tq, tq), 0)
    ii = lax.broadcasted_iota(jnp.int32, (tq, tq), 1)
    buf_a, buf_b = (sa_ref, mxa_ref), (sb_ref, mxb_ref)

    def rows_of(i):
        return pl.ds(pl.multiple_of(i * tq, tq), tq)

    def scores_into(buf, t):
        rows = rows_of(ki_tab[t])
        k = jnp.concatenate([kn_ref[rows, :], kp_ref[rows, :]], axis=1)
        _store_scores(buf, _dot_nt(k, q_ref[rows_of(qi_tab[t]), :]))

    def consume(buf, t, diagonal):
        s = buf[0][...]
        tile_max = buf[1][...]
        if diagonal:
            s = jnp.where(jj <= ii, s, NEG)
            tile_max = None
        _flash_update(s, vt_ref[:, rows_of(ki_tab[t])], c, m_ref, l_ref, acc_ref, tile_max)
        if diagonal:
            o_ref[rows_of(qi_tab[t]), :] = jnp.transpose(acc_ref[...] / l_ref[...]).astype(o_ref.dtype)
            _flash_reset(m_ref, l_ref, acc_ref)

    scores_into(buf_a, 0)

    def pair(p, carry):
        t0 = 2 * p
        d0 = ki_tab[t0] == qi_tab[t0]
        d1 = ki_tab[t0 + 1] == qi_tab[t0 + 1]
        for x, y in ((False, False), (True, False), (False, True)):
            @pl.when(jnp.logical_and(d0 == x, d1 == y))
            def _(x=x, y=y):
                scores_into(buf_b, t0 + 1)
                consume(buf_a, t0, x)
                scores_into(buf_a, t0 + 2)
                consume(buf_b, t0 + 1, y)
        return carry

    lax.fori_loop(0, ntiles // 2, pair, 0)


def _mla_attention(q, kn, kpe, vt, bsz, seq):
    tq = MLA_TQ
    pairs = [(qi, ki) for qi in range(seq // tq) for ki in range(qi + 1)]
    ntiles = len(pairs)
    diag = [qi == ki for qi, ki in pairs]
    assert ntiles % 2 == 0 and not any(diag[t] and diag[t + 1] for t in range(0, ntiles, 2))
    pairs.append((0, 0))
    qi_tab = jnp.asarray([p[0] for p in pairs], jnp.int32)
    ki_tab = jnp.asarray([p[1] for p in pairs], jnp.int32)
    smem = pl.BlockSpec(memory_space=pltpu.SMEM)
    return pl.pallas_call(
        functools.partial(_mla_kernel, ntiles=ntiles),
        grid=(bsz, MLA_HEADS),
        in_specs=[
            smem, smem,
            pl.BlockSpec((None, seq, 2 * LANES), lambda b, h: (b, 0, h)),
            pl.BlockSpec((None, seq, MLA_NOPE), lambda b, h: (b, 0, h)),
            pl.BlockSpec((None, seq, LANES), lambda b, h: (b, 0, 0)),
            pl.BlockSpec((MLA_V, seq), lambda b, h: (h, b)),
        ],
        out_specs=pl.BlockSpec((None, seq, MLA_V), lambda b, h: (b, 0, h)),
        out_shape=jax.ShapeDtypeStruct((bsz, seq, MLA_HEADS * MLA_V), _BF),
        scratch_shapes=[pltpu.VMEM((tq, tq), _F32), pltpu.VMEM((tq, tq), _F32)]
        + [pltpu.VMEM((1, tq), _F32) for _ in range(4)] + [pltpu.VMEM((MLA_V, tq), _F32)],
        compiler_params=_cparams(("parallel", "arbitrary")),
        name="mla_attention",
    )(qi_tab, ki_tab, q, kn, kpe, vt)


def _residual_norm_store(x_new, nw_ref, x_out_ref, h_out_ref):
    x_out_ref[...] = x_new
    y = x_new * lax.rsqrt(jnp.mean(x_new * x_new, axis=-1, keepdims=True) + EPS)
    h_out_ref[...] = (y * nw_ref[...]).astype(h_out_ref.dtype)


def _out_even_kernel(x_ref, oa_ref, ga_ref, ob_ref, gb_ref, wa_ref, wb_ref, nw_ref, x_out_ref, h_out_ref):
    mixed_a = (oa_ref[...] * _silu(ga_ref[...])).astype(_BF)
    mixed_b = (ob_ref[...] * _silu(gb_ref[...])).astype(_BF)
    x_new = x_ref[...] + _dot(mixed_a, wa_ref[...]) + _dot(mixed_b, wb_ref[...])
    _residual_norm_store(x_new, nw_ref, x_out_ref, h_out_ref)


def _out_even(x2d, o_a, o_b, zf, w_out, nw_next, zf_cols, h_dtype):
    m, d = x2d.shape
    tm = 256
    wa = A_GATE
    wbd = B_GATE
    row = lambda width, cb=0: pl.BlockSpec((tm, width), lambda i, cb=cb: (i, cb))
    full = lambda r, c: pl.BlockSpec((r, c), lambda i: (0, 0))
    return pl.pallas_call(
        _out_even_kernel,
        grid=(m // tm,),
        in_specs=[row(d), row(wa), row(wa, zf_cols["gate_a"] // wa), row(wbd), row(wbd, zf_cols["gate_b"] // wbd),
                  full(wa, d), full(wbd, d), full(1, d)],
        out_specs=[row(d), row(d)],
        out_shape=[jax.ShapeDtypeStruct((m, d), _F32), jax.ShapeDtypeStruct((m, d), h_dtype)],
        compiler_params=_cparams(("parallel",)),
        name="out_proj_even",
    )(x2d, o_a, zf, o_b, zf, w_out[:wa], w_out[wa:], nw_next.reshape(1, d).astype(_F32))


def _out_odd_kernel(x_ref, oc_ref, gc_ref, w_ref, nw_ref, *out_refs):
    mixed = (oc_ref[...] * _silu(gc_ref[...])).astype(_BF)
    x_new = x_ref[...] + _dot(mixed, w_ref[...])
    if len(out_refs) == 2:
        _residual_norm_store(x_new, nw_ref, *out_refs)
    else:
        y = x_new * lax.rsqrt(jnp.mean(x_new * x_new, axis=-1, keepdims=True) + EPS)
        out_refs[0][...] = (y * nw_ref[...]).astype(out_refs[0].dtype)


def _out_odd(x2d, o_c, z, w_out, nw_next, gate_col_block, h_dtype, keep_x):
    m, d = x2d.shape
    tm = 256
    width = MLA_HEADS * MLA_V
    row = lambda w, cb=0: pl.BlockSpec((tm, w), lambda i, cb=cb: (i, cb))
    h_shape = jax.ShapeDtypeStruct((m, d), h_dtype)
    outs = pl.pallas_call(
        _out_odd_kernel,
        grid=(m // tm,),
        in_specs=[row(d), row(width), row(width, gate_col_block),
                  pl.BlockSpec((width, d), lambda i: (0, 0)), pl.BlockSpec((1, d), lambda i: (0, 0))],
        out_specs=[row(d), row(d)] if keep_x else [row(d)],
        out_shape=[jax.ShapeDtypeStruct((m, d), _F32), h_shape] if keep_x else [h_shape],
        compiler_params=_cparams(("parallel",)),
        name="out_proj_odd",
    )(x2d, o_c, z, w_out, nw_next.reshape(1, d).astype(_F32))
    return (outs[0], outs[1]) if keep_x else (None, outs[0])


def _bucket_of_distance(n):
    d = np.arange(n)
    max_exact = NUM_BUCKETS // 2
    df = np.maximum(d, 1).astype(np.float32)
    large = max_exact + (np.log(df / max_exact) / math.log(MAX_DISTANCE / max_exact)
                         * (NUM_BUCKETS - max_exact)).astype(np.int32)
    large = np.minimum(large, NUM_BUCKETS - 1)
    return np.where(d < max_exact, d, large).astype(np.int32)


def _toeplitz_tiles(v, n, nd):
    hh = v.shape[0]
    vp = jnp.concatenate([jnp.zeros((hh, n), v.dtype), v, jnp.zeros((hh, n), v.dtype)], axis=1)
    idx = (n - np.arange(2 * n)) % (2 * n)
    tiles = []
    for dl in range(nd):
        w = vp[:, dl * n:dl * n + 2 * n]
        c = w[:, idx]
        flat = jnp.tile(c, (1, n))[:, :n * (2 * n - 1)]
        tiles.append(flat.reshape(hh, n, 2 * n - 1)[:, :, :n])
    return jnp.stack(tiles, axis=0)


def _bias_tables(rel_bias, seq):
    bd = rel_bias.astype(_F32)[_bucket_of_distance(seq)].T
    bd_nsa = bd[:NSA_HEADS]
    nd = seq // NSA_TQ
    bt = _toeplitz_tiles(bd_nsa * LOG2E, NSA_TQ, nd)
    bt = bt.reshape(nd, NSA_GROUPS, NSA_HPG, NSA_TQ, NSA_TQ).transpose(1, 0, 4, 2, 3)
    bt = bt.reshape(NSA_GROUPS, nd, NSA_TQ, NSA_HPG * NSA_TQ)
    nblk = seq // NSA_BLOCK
    off = NSA_BLOCK * (nblk - 1) + NSA_BLOCK - 1
    bdp = jnp.concatenate([jnp.zeros((NSA_HEADS, off), _F32), bd_nsa], axis=1)
    bc = jnp.stack([bdp[:, off - (NSA_BLOCK * n + NSA_BLOCK - 1):off - (NSA_BLOCK * n + NSA_BLOCK - 1) + seq]
                    for n in range(nblk)], axis=1)
    bc = bc.reshape(NSA_GROUPS, NSA_HPG, nblk, seq)
    rows, cols = DIL_TQ, DIL_PW + DIL_TQ
    period = rows + cols
    dtabs = []
    for g, (w, d) in enumerate(DIL_PAIRS):
        heads = bd[NSA_HEADS + g * DIL_HPG:NSA_HEADS + (g + 1) * DIL_HPG] * LOG2E
        m = w // d + 1
        assert m == DIL_PW + 1
        by_offset = heads[:, 0:m * d:d]
        cyc = jnp.concatenate([by_offset[:, ::-1], jnp.full((DIL_HPG, period - m), NEG, _F32)], axis=1)
        flat = jnp.tile(cyc, (1, rows))[:, :rows * (period - 1)]
        by_query = flat.reshape(DIL_HPG, rows, period - 1)[:, :, :cols]
        dtabs.append(by_query.transpose(0, 2, 1))
    return bt, bc, jnp.stack(dtabs, axis=0)


def _rope_tables(seq):
    half = MLA_ROPE // 2
    freqs = 1.0 / (ROPE_THETA ** (jnp.arange(half, dtype=_F32) / half))
    ang = jnp.arange(seq).astype(_F32)[:, None] * freqs[None, :]
    zeros = jnp.zeros((seq, LANES - MLA_ROPE), _F32)
    cos = jnp.concatenate([jnp.cos(ang), jnp.cos(ang), zeros], axis=1)
    sin = jnp.concatenate([jnp.sin(ang), jnp.sin(ang), zeros], axis=1)
    return cos, sin


def _rot_cols(w):
    half = w.shape[-1] // 2
    return jnp.concatenate([-w[..., half:], w[..., :half]], axis=-1)


ZF_COLS = {"gate_a": 0, "gate_b": A_GATE, "kcvc": A_GATE + B_GATE, "g3": A_GATE + B_GATE + 4 * HEAD_DIM}


def _even_weights(w_in):
    w_in = w_in.astype(_BF)
    o = 0
    q_a = w_in[:, o:o + A_Q]; o += A_Q
    kv_a = w_in[:, o:o + A_KV]; o += A_KV
    g3 = w_in[:, o:o + A_G3]; o += A_G3
    gate_a = w_in[:, o:o + A_GATE]; o += A_GATE
    qkv_b = w_in[:, o:o + B_QKV]; o += B_QKV
    gate_b = w_in[:, o:o + B_GATE]
    gw = NSA_GROUPS * HEAD_DIM
    kc_vc, ks, vs, kw, vw = (kv_a[:, :2 * gw], kv_a[:, 2 * gw:3 * gw], kv_a[:, 3 * gw:4 * gw],
                             kv_a[:, 4 * gw:5 * gw], kv_a[:, 5 * gw:6 * gw])
    wb = jnp.concatenate([q_a, ks, kw], axis=1).astype(_BF)
    wvt = jnp.concatenate([vs, vw], axis=1).T.astype(_BF)
    dw = DIL_HPG * HEAD_DIM
    nq = DIL_HEADS * HEAD_DIM
    wd = jnp.concatenate([qkv_b[:, part * nq + g * dw:part * nq + (g + 1) * dw]
                          for g in range(DIL_GROUPS) for part in range(3)], axis=1).astype(_BF)
    g3_blocks = []
    for g in range(NSA_GROUPS):
        cols = [(g * NSA_HPG + h) * 3 + j for j in range(3) for h in range(NSA_HPG)]
        blk = g3[:, np.asarray(cols)]
        g3_blocks.append(jnp.pad(blk, ((0, 0), (0, LANES - len(cols)))))
    wf = jnp.concatenate([gate_a, gate_b, kc_vc] + g3_blocks, axis=1).astype(_BF)
    return wb, wvt, wd, wf


def _even_layer(x2d, h, w_in, cmp_pos, cmp_w1, cmp_w2, w_out, nw_next, tables, bsz, seq, h_dtype):
    bt, bc, dtab, onehot = tables
    wb, wvt, wd, wf = _even_weights(w_in)
    zb = _matmul(h, wb, _BF)
    vt = _matmul_nt(wvt, h, _BF)
    zd = _matmul(h, wd, _BF)
    zf = _matmul(h, wf, _F32)
    nblk = seq // NSA_BLOCK
    kcvc = zf[:, ZF_COLS["kcvc"]:ZF_COLS["kcvc"] + 4 * HEAD_DIM]
    kcvc_t = kcvc.reshape(bsz * nblk, NSA_BLOCK, 4, HEAD_DIM).transpose(2, 1, 0, 3)
    kvc = _compress(kcvc_t, cmp_pos.astype(_F32), cmp_w1.astype(_BF), cmp_w2.astype(_BF))
    kvc = kvc.reshape(4, bsz, nblk, HEAD_DIM)
    zf3 = zf.reshape(bsz, seq, zf.shape[-1])
    o_a = _nsa_attention(zb.reshape(bsz, seq, -1), vt, zf3, kvc, onehot, bt, bc, bsz, seq)
    o_b = _dilated_attention(zd.reshape(bsz, seq, -1), dtab, bsz, seq)
    return _out_even(x2d, o_a.reshape(bsz * seq, -1), o_b.reshape(bsz * seq, -1), zf, w_out.astype(_BF), nw_next,
                     ZF_COLS, h_dtype)


def _odd_layer(x2d, h, w_in, q_norm, w_qb, kv_norm, w_kvb, w_out, nw_next, rope, bsz, seq, h_dtype, keep_x):
    cos, sin = rope
    w_in, w_qb, w_kvb = w_in.astype(_BF), w_qb.astype(_BF), w_kvb.astype(_BF)
    o = 0
    w_cq = w_in[:, o:o + MLA_Q_RANK]; o += MLA_Q_RANK
    w_ckv = w_in[:, o:o + MLA_KV_RANK]; o += MLA_KV_RANK
    w_kpe = w_in[:, o:o + MLA_ROPE]; o += MLA_ROPE
    w_gate = w_in[:, o:]
    w1 = jnp.concatenate([w_cq, w_ckv, w_gate], axis=1).astype(_BF)
    z = _matmul(h, w1, _F32)
    zk = _matmul(h, jnp.concatenate([w_kpe, _rot_cols(w_kpe)], axis=1).astype(_BF), _F32)
    gate_width = MLA_HEADS * MLA_V
    q_latent = (q_norm, MLA_Q_RANK, 0)
    kv_latent = (kv_norm, MLA_KV_RANK, MLA_Q_RANK // MLA_KV_RANK)
    kpe = _rope_cols(zk, 0, cos, sin, seq)
    wq = w_qb.reshape(MLA_Q_RANK, MLA_HEADS, MLA_NOPE + MLA_ROPE)
    wq_pe = wq[:, :, MLA_NOPE:]
    wq = jnp.concatenate([wq[:, :, :MLA_NOPE], wq_pe, _rot_cols(wq_pe)], axis=-1)
    q = _matmul_rope(z, q_latent, wq.reshape(MLA_Q_RANK, MLA_HEADS * 2 * LANES).astype(_BF), cos, sin, seq)
    wkv = w_kvb.reshape(MLA_KV_RANK, MLA_HEADS, MLA_NOPE + MLA_V)
    wk = wkv[:, :, :MLA_NOPE].reshape(MLA_KV_RANK, -1).astype(_BF)
    wv_t = wkv[:, :, MLA_NOPE:].reshape(MLA_KV_RANK, -1).T.astype(_BF)
    kn = _matmul(z, wk, _BF, norm=kv_latent)
    vt = _matmul_nt(wv_t, z, _BF, norm=kv_latent)
    o_c = _mla_attention(q.reshape(bsz, seq, -1), kn.reshape(bsz, seq, -1), kpe.reshape(bsz, seq, LANES),
                         vt, bsz, seq)
    gate_block = (MLA_Q_RANK + MLA_KV_RANK) // gate_width
    return _out_odd(x2d, o_c.reshape(bsz * seq, -1), z, w_out.astype(_BF), nw_next, gate_block, h_dtype, keep_x)


def kernel(x, rel_bias, norm_w, final_norm_w, ev_w_in, nsa_cmp_pos, nsa_cmp_w1, nsa_cmp_w2, ev_w_out,
           od_w_in, mla_q_norm, mla_w_qb, mla_kv_norm, mla_w_kvb, od_w_out):
    bsz, seq, d = x.shape
    depth = norm_w.shape[0]
    assert seq % MLA_TQ == 0 and seq % (DIL_TQ * DIL_PAIRS[-1][1]) == 0 and seq // NSA_BLOCK <= LANES
    assert NSA_WINDOW == 2 * NSA_TQ
    bt, bc, dtab = _bias_tables(rel_bias, seq)
    blk_id = np.arange(seq)[:, None] // NSA_BLOCK
    onehot = jnp.asarray(blk_id == np.arange(LANES)[None, :], dtype=_BF)
    tables = (bt, bc, dtab, onehot)
    rope = _rope_tables(seq)
    x2d = x.reshape(bsz * seq, d).astype(_F32)
    h = _rmsnorm(x2d, norm_w[0], d, 0, _BF)
    for l in range(depth):
        last = l == depth - 1
        nw_next = final_norm_w if last else norm_w[l + 1]
        h_dtype = _F32 if last else _BF
        i = l // 2
        if l % 2 == 0:
            x2d, h = _even_layer(x2d, h, ev_w_in[i], nsa_cmp_pos[i], nsa_cmp_w1[i], nsa_cmp_w2[i],
                                 ev_w_out[i], nw_next, tables, bsz, seq, h_dtype)
        else:
            x2d, h = _odd_layer(x2d, h, od_w_in[i], mla_q_norm[i], mla_w_qb[i], mla_kv_norm[i],
                                mla_w_kvb[i], od_w_out[i], nw_next, rope, bsz, seq, h_dtype, keep_x=not last)
    return h.reshape(bsz, seq, d)
```

```python
import functools
import math

import numpy as np
import jax
import jax.numpy as jnp
from jax import lax
from jax.experimental import pallas as pl
from jax.experimental.pallas import tpu as pltpu

HEAD_DIM = 128
EPS = 1e-6
NEG = -1e30
NUM_BUCKETS = 32
MAX_DISTANCE = 2048
NSA_HEADS = 8
NSA_GROUPS = 2
NSA_HPG = 4
NSA_BLOCK = 64
NSA_TOPK = 16
NSA_WINDOW = 512
DIL_PAIRS = ((128, 1), (512, 4), (2048, 16))
DIL_GROUPS = 3
DIL_HPG = 4
DIL_HEADS = 12
MLA_HEADS = 16
MLA_Q_RANK = 1536
MLA_KV_RANK = 512
MLA_NOPE = 128
MLA_ROPE = 64
MLA_V = 128
ROPE_THETA = 10000.0

A_Q = NSA_HEADS * HEAD_DIM
A_KV = 6 * NSA_GROUPS * HEAD_DIM
A_G3 = 3 * NSA_HEADS
A_GATE = NSA_HEADS * HEAD_DIM
B_QKV = 3 * DIL_HEADS * HEAD_DIM
B_GATE = DIL_HPG * HEAD_DIM

LANES = 128
SUBLANES = 8
VMEM_LIMIT = 48 * 1024 * 1024

NSA_TQ = 256
DIL_TQ = 256
DIL_PW = 128
DIL_UNROLL = 8
MLA_TQ = 512
MLA_HPS = 2
LOG2E = math.log2(math.e)

_BF = jnp.bfloat16
_F32 = jnp.float32


def _cparams(sem):
    return pltpu.CompilerParams(dimension_semantics=sem, vmem_limit_bytes=VMEM_LIMIT)


def _dot(a, b):
    return jnp.dot(a, b, preferred_element_type=_F32)


def _dot_nt(a, b):
    return lax.dot_general(a, b, (((1,), (1,)), ((), ())), preferred_element_type=_F32)


def _silu(x):
    return x * (1.0 / (1.0 + jnp.exp(-x)))


def _sigmoid(x):
    return 1.0 / (1.0 + jnp.exp(-x))


def _pick_tile(n, candidates):
    for c in candidates:
        if n % c == 0:
            return c
    return n


def _rmsnorm_kernel(x_ref, w_ref, o_ref):
    x = x_ref[...]
    y = x * lax.rsqrt(jnp.mean(x * x, axis=-1, keepdims=True) + EPS)
    o_ref[...] = (y * w_ref[...]).astype(o_ref.dtype)


def _rmsnorm(x2d, w, width, col_block, out_dtype):
    m = x2d.shape[0]
    tm = _pick_tile(m, (512, 256, 128))
    return pl.pallas_call(
        _rmsnorm_kernel,
        grid=(m // tm,),
        in_specs=[pl.BlockSpec((tm, width), lambda i: (i, col_block)),
                  pl.BlockSpec((1, width), lambda i: (0, 0))],
        out_specs=pl.BlockSpec((tm, width), lambda i: (i, 0)),
        out_shape=jax.ShapeDtypeStruct((m, width), out_dtype),
        compiler_params=_cparams(("parallel",)),
        name="rmsnorm",
    )(x2d, w.reshape(1, width).astype(_F32))


def _matmul_kernel(a_ref, w_ref, o_ref):
    o_ref[...] = _dot(a_ref[...], w_ref[...]).astype(o_ref.dtype)


def _normed_lhs(a_ref, nw_ref, an_ref):
    @pl.when(pl.program_id(1) == 0)
    def _():
        x = a_ref[...]
        y = x * lax.rsqrt(jnp.mean(x * x, axis=-1, keepdims=True) + EPS)
        an_ref[...] = (y * nw_ref[...]).astype(an_ref.dtype)
    return an_ref[...]


def _norm_matmul_kernel(a_ref, nw_ref, w_ref, o_ref, an_ref):
    o_ref[...] = _dot(_normed_lhs(a_ref, nw_ref, an_ref), w_ref[...]).astype(o_ref.dtype)


def _lhs_specs(a, norm, tm):
    if norm is None:
        k = a.shape[1]
        return k, [pl.BlockSpec((tm, k), lambda i, j: (i, 0))], [], []
    nw, k, cb = norm
    specs = [pl.BlockSpec((tm, k), lambda i, j: (i, cb)), pl.BlockSpec((1, k), lambda i, j: (0, 0))]
    return k, specs, [nw.reshape(1, k).astype(_F32)], [pltpu.VMEM((tm, k), _BF)]


def _matmul(a, w, out_dtype, norm=None):
    m = a.shape[0]
    n = w.shape[1]
    tm = _pick_tile(m, (1024, 512, 256, 128))
    tn = _pick_tile(n, (1024, 768, 512, 384, 256, 128))
    k, a_specs, a_args, scratch = _lhs_specs(a, norm, tm)
    return pl.pallas_call(
        _matmul_kernel if norm is None else _norm_matmul_kernel,
        grid=(m // tm, n // tn),
        in_specs=a_specs + [pl.BlockSpec((k, tn), lambda i, j: (0, j))],
        out_specs=pl.BlockSpec((tm, tn), lambda i, j: (i, j)),
        out_shape=jax.ShapeDtypeStruct((m, n), out_dtype),
        scratch_shapes=scratch,
        compiler_params=_cparams(("parallel", "arbitrary")),
        name="matmul" if norm is None else "norm_matmul",
    )(a, *a_args, w)


def _matmul_nt_kernel(w_ref, a_ref, o_ref):
    o_ref[...] = _dot_nt(w_ref[...], a_ref[...]).astype(o_ref.dtype)


def _norm_matmul_nt_kernel(w_ref, a_ref, nw_ref, o_ref, an_ref):
    o_ref[...] = _dot_nt(w_ref[...], _normed_lhs(a_ref, nw_ref, an_ref)).astype(o_ref.dtype)


def _matmul_nt(w_t, a, out_dtype, norm=None):
    n = w_t.shape[0]
    m = a.shape[0]
    tm = _pick_tile(m, (1024, 512, 256, 128))
    tn = _pick_tile(n, (1024, 512, 256, 128))
    k, a_specs, a_args, scratch = _lhs_specs(a, norm, tm)
    return pl.pallas_call(
        _matmul_nt_kernel if norm is None else _norm_matmul_nt_kernel,
        grid=(m // tm, n // tn),
        in_specs=[pl.BlockSpec((tn, k), lambda i, j: (j, 0))] + a_specs,
        out_specs=pl.BlockSpec((tn, tm), lambda i, j: (j, i)),
        out_shape=jax.ShapeDtypeStruct((n, m), out_dtype),
        scratch_shapes=scratch,
        compiler_params=_cparams(("parallel", "arbitrary")),
        name="matmul_nt" if norm is None else "norm_matmul_nt",
    )(w_t, a, *a_args)


def _rope_chunk(chunk, cos, sin):
    return chunk * cos + pltpu.roll(chunk, 64, 1) * sin


def _matmul_rope_kernel(a_ref, nw_ref, w_ref, cos_ref, sin_ref, o_ref, an_ref, *, heads_per_tile):
    acc = _dot(_normed_lhs(a_ref, nw_ref, an_ref), w_ref[...])
    cos = cos_ref[...]
    sin = sin_ref[...]
    for h in range(heads_per_tile):
        base = h * 2 * LANES
        o_ref[:, base:base + LANES] = acc[:, base:base + LANES].astype(o_ref.dtype)
        o_ref[:, base + LANES:base + 2 * LANES] = _rope_chunk(
            acc[:, base + LANES:base + 2 * LANES], cos, sin).astype(o_ref.dtype)


def _matmul_rope(a, norm, w, cos, sin, seq):
    m = a.shape[0]
    n = w.shape[1]
    tm = _pick_tile(seq, (1024, 512, 256, 128))
    tn = 1024
    tpb = seq // tm
    k, a_specs, a_args, scratch = _lhs_specs(a, norm, tm)
    return pl.pallas_call(
        functools.partial(_matmul_rope_kernel, heads_per_tile=tn // (2 * LANES)),
        grid=(m // tm, n // tn),
        in_specs=a_specs + [pl.BlockSpec((k, tn), lambda i, j: (0, j)),
                            pl.BlockSpec((tm, LANES), lambda i, j: (i % tpb, 0)),
                            pl.BlockSpec((tm, LANES), lambda i, j: (i % tpb, 0))],
        out_specs=pl.BlockSpec((tm, tn), lambda i, j: (i, j)),
        out_shape=jax.ShapeDtypeStruct((m, n), _BF),
        scratch_shapes=scratch,
        compiler_params=_cparams(("parallel", "arbitrary")),
        name="matmul_rope",
    )(a, *a_args, w, cos, sin)


def _rope_cols_kernel(x_ref, cos_ref, sin_ref, o_ref):
    o_ref[...] = _rope_chunk(x_ref[...], cos_ref[...], sin_ref[...]).astype(o_ref.dtype)


def _rope_cols(z2d, col_block, cos, sin, seq):
    m = z2d.shape[0]
    tm = _pick_tile(seq, (512, 256, 128))
    tpb = seq // tm
    return pl.pallas_call(
        _rope_cols_kernel,
        grid=(m // tm,),
        in_specs=[pl.BlockSpec((tm, LANES), lambda i: (i, col_block)),
                  pl.BlockSpec((tm, LANES), lambda i: (i % tpb, 0)),
                  pl.BlockSpec((tm, LANES), lambda i: (i % tpb, 0))],
        out_specs=pl.BlockSpec((tm, LANES), lambda i: (i, 0)),
        out_shape=jax.ShapeDtypeStruct((m, LANES), _BF),
        compiler_params=_cparams(("parallel",)),
        name="rope_kpe",
    )(z2d, cos, sin)


def _compress_kernel(blk_ref, pos_ref, w1_ref, w2_ref, o_ref, acc_ref, *, lt):
    li = pl.program_id(1)

    @pl.when(li == 0)
    def _():
        acc_ref[...] = jnp.zeros_like(acc_ref)

    acc = acc_ref[...]
    for l in range(lt):
        a = (blk_ref[l] + pos_ref[l:l + 1, :]).astype(_BF)
        acc = acc + _dot(a, w1_ref[l])
    acc_ref[...] = acc

    @pl.when(li == pl.num_programs(1) - 1)
    def _():
        hid = _silu(acc_ref[...]).astype(_BF)
        o_ref[...] = _dot(hid, w2_ref[...])


def _compress(kcvc_t, pos, w1, w2):
    _, L, R, _ = kcvc_t.shape
    lt = 16
    return pl.pallas_call(
        functools.partial(_compress_kernel, lt=lt),
        grid=(4, L // lt),
        in_specs=[pl.BlockSpec((None, lt, R, HEAD_DIM), lambda c, l: (c, l, 0, 0)),
                  pl.BlockSpec((None, lt, HEAD_DIM), lambda c, l: (c // 2, l, 0)),
                  pl.BlockSpec((None, lt, HEAD_DIM, HEAD_DIM), lambda c, l: (c // 2, l, 0, 0)),
                  pl.BlockSpec((None, HEAD_DIM, HEAD_DIM), lambda c, l: (c // 2, 0, 0))],
        out_specs=pl.BlockSpec((None, R, HEAD_DIM), lambda c, l: (c, 0, 0)),
        out_shape=jax.ShapeDtypeStruct((4, R, HEAD_DIM), _F32),
        scratch_shapes=[pltpu.VMEM((R, HEAD_DIM), _F32)],
        compiler_params=_cparams(("parallel", "arbitrary")),
        name="nsa_compress",
    )(kcvc_t, pos, w1.reshape(2, L, HEAD_DIM, HEAD_DIM), w2)


def _flash_reset(m_ref, l_ref, acc_ref):
    m_ref[...] = jnp.full(m_ref.shape, -jnp.inf, _F32)
    l_ref[...] = jnp.zeros(l_ref.shape, _F32)
    acc_ref[...] = jnp.zeros(acc_ref.shape, _F32)


def _flash_update(s, vt, c, m_ref, l_ref, acc_ref, tile_max=None):
    m_prev = m_ref[...]
    if tile_max is None:
        tile_max = jnp.max(s, axis=0, keepdims=True)
    m_new = jnp.maximum(m_prev, tile_max)
    if c is None:
        alpha = jnp.exp2(m_prev - m_new)
        p = jnp.exp2(s - m_new)
    else:
        alpha = jnp.exp2((m_prev - m_new) * c)
        p = jnp.exp2((s - m_new) * c)
    l_ref[...] = alpha * l_ref[...] + jnp.sum(p, axis=0, keepdims=True)
    acc_ref[...] = alpha * acc_ref[...] + _dot(vt, p.astype(_BF))
    m_ref[...] = m_new


def _store_scores(buf, s):
    s_ref, mx_ref = buf
    s_ref[...] = s
    mx_ref[...] = jnp.max(s, axis=0, keepdims=True)


def _causal_flash(n_off, scores_into, consume, buf_a, buf_b, primed=False):
    if not primed:
        scores_into(buf_a, 0)

    def pair(p, carry):
        k0 = 2 * p
        scores_into(buf_b, k0 + 1)
        consume(buf_a, k0, False)
        scores_into(buf_a, k0 + 2)
        consume(buf_b, k0 + 1, False)
        return carry

    lax.fori_loop(0, n_off // 2, pair, 0)

    @pl.when(n_off % 2 == 1)
    def _():
        scores_into(buf_b, n_off)
        consume(buf_a, n_off - 1, False)
        consume(buf_b, n_off, True)

    @pl.when(n_off % 2 == 0)
    def _():
        consume(buf_a, n_off, True)


def _nsa_kernel(q_ref, ks_ref, kw_ref, vst_ref, vwt_ref, kc_ref, vc_ref, g3_ref, oh_ref, bt_ref, bc_ref,
                o_ref, sa_ref, sb_ref, w0_ref, w1_ref, w2_ref, m_ref, l_ref, mw_ref, lw_ref, mxa_ref, mxb_ref,
                acc_s_ref, acc_w_ref, *, seq):
    tq = NSA_TQ
    hp = NSA_HPG
    n = hp * tq
    nblk = seq // NSA_BLOCK
    qi = pl.program_id(2)
    scale = HEAD_DIM ** -0.5
    c = scale * LOG2E
    t0 = qi * tq

    qs = jnp.concatenate([q_ref[:, h * HEAD_DIM:(h + 1) * HEAD_DIM] for h in range(hp)], axis=0)

    def rows_of(kt):
        return pl.ds(pl.multiple_of(kt * tq, tq), tq)

    for delta, dst_ref in enumerate((w0_ref, w1_ref, w2_ref)):
        dst_ref[...] = _dot_nt(kw_ref[rows_of(jnp.maximum(qi - delta, 0)), :], qs) * c + bt_ref[delta]

    kc = kc_ref[...].astype(_BF)
    s_c = _dot_nt(kc, qs) * scale
    s_c = s_c + jnp.concatenate([bc_ref[h] for h in range(hp)], axis=1)
    n_col = lax.broadcasted_iota(jnp.int32, (nblk, tq), 0)
    t_row = t0 + lax.broadcasted_iota(jnp.int32, (nblk, tq), 1)
    n_all = lax.broadcasted_iota(jnp.int32, (nblk, n), 0)
    t_all = t0 + (lax.broadcasted_iota(jnp.int32, (nblk, n), 1) & (tq - 1))
    valid = t_all >= n_all * NSA_BLOCK + (NSA_BLOCK - 1)
    s_c = jnp.where(valid, s_c, NEG)
    m_c = jnp.max(s_c, axis=0, keepdims=True)
    p_c = jnp.where(valid, jnp.exp(s_c - m_c), 0.0)
    l_c = jnp.sum(p_c, axis=0, keepdims=True)
    p_c = p_c / jnp.where(l_c > 0.0, l_c, 1.0)
    imp = p_c[:, 0:tq]
    for h in range(1, hp):
        imp = imp + p_c[:, h * tq:(h + 1) * tq]

    pad_rows = LANES - nblk
    vc_pad = jnp.concatenate([vc_ref[...], jnp.zeros((pad_rows, HEAD_DIM), _F32)], axis=0)
    vc_t = jnp.transpose(vc_pad).astype(_BF)
    p_pad = jnp.concatenate([p_c, jnp.zeros((pad_rows, n), _F32)], axis=0).astype(_BF)
    o_cmp = _dot(vc_t, p_pad)

    cur = jnp.right_shift(t_row, int(math.log2(NSA_BLOCK)))
    forced = (n_col == 0) | (n_col == cur) | (n_col == cur - 1)
    impv = jnp.where(forced, jnp.inf, jnp.where(n_col > cur, -jnp.inf, imp))
    rank = jnp.zeros((nblk, tq), _F32)
    sub_row = lax.broadcasted_iota(jnp.int32, (SUBLANES, tq), 0)
    for i in range(nblk):
        row = impv[i:i + 1, :]
        lo = i // SUBLANES * SUBLANES
        parts = []
        if lo > 0:
            parts.append(jnp.where(row > impv[:lo], 1.0, 0.0))
        mid = impv[lo:lo + SUBLANES]
        parts.append(jnp.where(sub_row > i - lo,
                               jnp.where(row >= mid, 1.0, 0.0), jnp.where(row > mid, 1.0, 0.0)))
        if lo + SUBLANES < nblk:
            parts.append(jnp.where(row >= impv[lo + SUBLANES:], 1.0, 0.0))
        rank = rank + jnp.concatenate(parts, axis=0)
    sel = (rank < float(min(NSA_TOPK, nblk))) & (n_col <= cur)
    pen = jnp.where(sel, 0.0, NEG)
    pen = jnp.concatenate([pen, jnp.full((pad_rows, tq), NEG, _F32)], axis=0)
    pen_t = jnp.transpose(pen).astype(_BF)
    q_aug = jnp.concatenate([qs, jnp.concatenate([pen_t] * hp, axis=0)], axis=1)

    jj = lax.broadcasted_iota(jnp.int32, (tq, n), 0)
    ii = lax.broadcasted_iota(jnp.int32, (tq, n), 1) & (tq - 1)

    def slc_scores(buf, ki):
        rows = rows_of(ki)
        k_aug = jnp.concatenate([ks_ref[rows, :], oh_ref[rows, :]], axis=1)
        _store_scores(buf, _dot_nt(k_aug, q_aug) * c + bt_ref[qi - ki])

    def slc_consume(buf, ki, diagonal):
        s = buf[0][...]
        tile_max = buf[1][...]
        if diagonal:
            s = jnp.where(jj <= ii, s, NEG)
            tile_max = None
        _flash_update(s, vst_ref[:, rows_of(ki)], None, m_ref, l_ref, acc_s_ref, tile_max)

    slc_scores((sa_ref, mxa_ref), 0)

    def win_consume(src_ref, delta, keep):
        s = jnp.where(keep, src_ref[...], NEG)
        _flash_update(s, vwt_ref[:, rows_of(jnp.maximum(qi - delta, 0))], None, mw_ref, lw_ref, acc_w_ref)

    _flash_reset(mw_ref, lw_ref, acc_w_ref)
    win_consume(w0_ref, 0, jj <= ii)
    win_consume(w1_ref, 1, qi >= 1)
    win_consume(w2_ref, 2, (jj > ii) & (qi >= 2))
    o_win = acc_w_ref[...] / lw_ref[...]

    _flash_reset(m_ref, l_ref, acc_s_ref)
    _causal_flash(qi, slc_scores, slc_consume, (sa_ref, mxa_ref), (sb_ref, mxb_ref), primed=True)
    o_slc = acc_s_ref[...] / l_ref[...]

    g_t = jnp.transpose(_sigmoid(g3_ref[...]))
    for h in range(hp):
        cols = slice(h * tq, (h + 1) * tq)
        out_t = (g_t[0 * hp + h:0 * hp + h + 1, :] * o_cmp[:, cols]
                 + g_t[1 * hp + h:1 * hp + h + 1, :] * o_slc[:, cols]
                 + g_t[2 * hp + h:2 * hp + h + 1, :] * o_win[:, cols])
        o_ref[:, h * HEAD_DIM:(h + 1) * HEAD_DIM] = jnp.transpose(out_t).astype(o_ref.dtype)


def _nsa_attention(zb, vt, zf, kvc, onehot, bt, bc, bsz, seq):
    tq = NSA_TQ
    n = NSA_HPG * tq
    nblk = seq // NSA_BLOCK
    nd = seq // tq
    kcol = A_Q // HEAD_DIM
    g3col = ZF_COLS["g3"] // LANES
    return pl.pallas_call(
        functools.partial(_nsa_kernel, seq=seq),
        grid=(NSA_GROUPS, bsz, seq // tq),
        in_specs=[
            pl.BlockSpec((None, tq, NSA_HPG * HEAD_DIM), lambda g, b, i: (b, i, g)),
            pl.BlockSpec((None, seq, HEAD_DIM), lambda g, b, i: (b, 0, kcol + g)),
            pl.BlockSpec((None, seq, HEAD_DIM), lambda g, b, i: (b, 0, kcol + NSA_GROUPS + g)),
            pl.BlockSpec((HEAD_DIM, seq), lambda g, b, i: (g, b)),
            pl.BlockSpec((HEAD_DIM, seq), lambda g, b, i: (NSA_GROUPS + g, b)),
            pl.BlockSpec((None, None, nblk, HEAD_DIM), lambda g, b, i: (g, b, 0, 0)),
            pl.BlockSpec((None, None, nblk, HEAD_DIM), lambda g, b, i: (2 + g, b, 0, 0)),
            pl.BlockSpec((None, tq, LANES), lambda g, b, i: (b, i, g3col + g)),
            pl.BlockSpec((seq, LANES), lambda g, b, i: (0, 0)),
            pl.BlockSpec((None, nd, tq, n), lambda g, b, i: (g, 0, 0, 0), pipeline_mode=pl.Buffered(1)),
            pl.BlockSpec((None, NSA_HPG, nblk, tq), lambda g, b, i: (g, 0, 0, i)),
        ],
        out_specs=pl.BlockSpec((None, tq, NSA_HPG * HEAD_DIM), lambda g, b, i: (b, i, g)),
        out_shape=jax.ShapeDtypeStruct((bsz, seq, NSA_HEADS * HEAD_DIM), _BF),
        scratch_shapes=[pltpu.VMEM((tq, n), _F32) for _ in range(5)]
        + [pltpu.VMEM((1, n), _F32) for _ in range(6)]
        + [pltpu.VMEM((HEAD_DIM, n), _F32) for _ in range(2)],
        compiler_params=_cparams(("arbitrary", "arbitrary", "arbitrary")),
        name="nsa_attention",
    )(zb, zb, zb, vt, vt, kvc, kvc, zf, onehot, bt, bc)


def _dilated_kernel(*refs, seq):
    qkv_refs, (tab_ref, out_ref, qs_ref, ks_ref, vs_ref, lse_ref, o_ref) = refs[:9], refs[9:]
    tq, pw = DIL_TQ, DIL_PW
    c = HEAD_DIM ** -0.5 * LOG2E
    for g, (_, d) in enumerate(DIL_PAIRS):
        q_ref, k_ref, v_ref = qkv_refs[3 * g:3 * g + 3]
        qs_ref[...] = q_ref[...].astype(_F32)
        ks_ref[...] = k_ref[...].astype(_F32)
        vs_ref[...] = v_ref[...].astype(_F32)
        ntile = seq // d // tq
        tab_p = tab_ref[g, :pw, :]
        tab_d = tab_ref[g, pw:, :]

        def scores(idx, d=d, ntile=ntile, tab_d=tab_d, tab_p=tab_p):
            r = idx // ntile
            tile = idx % ntile
            u0 = tile * tq
            rows = pl.ds(u0 * d + r, tq, stride=d)
            rows_p = pl.ds(jnp.maximum(u0 - pw, 0) * d + r, pw, stride=d)
            q = qs_ref[rows, :].astype(_BF)
            s_d = _dot_nt(ks_ref[rows, :].astype(_BF), q) * c + tab_d
            s_p = jnp.where(tile > 0, _dot_nt(ks_ref[rows_p, :].astype(_BF), q) * c + tab_p, NEG)
            return rows, rows_p, s_d, s_p

        def softmax(rows, rows_p, s_d, s_p):
            m = jnp.maximum(jnp.max(s_d, axis=0, keepdims=True), jnp.max(s_p, axis=0, keepdims=True))
            p_d = jnp.exp2(s_d - m)
            p_p = jnp.exp2(s_p - m)
            l = jnp.sum(p_d, axis=0, keepdims=True) + jnp.sum(p_p, axis=0, keepdims=True)
            return rows, rows_p, p_d.astype(_BF), p_p.astype(_BF), l, m + jnp.log2(l)

        def values(rows, rows_p, p_d, p_p, l, lse):
            vt_d = jnp.transpose(vs_ref[rows, :]).astype(_BF)
            vt_p = jnp.transpose(vs_ref[rows_p, :]).astype(_BF)
            acc = _dot(vt_d, p_d) + _dot(vt_p, p_p)
            return rows, jnp.transpose(acc / l), jnp.transpose(jnp.broadcast_to(lse, (HEAD_DIM, tq)))

        def tile_body(it, carry, g=g, scores=scores):
            staged = [scores(it * DIL_UNROLL + j) for j in range(DIL_UNROLL)]
            staged = [softmax(*x) for x in staged]
            for rows, o, lse in [values(*x) for x in staged]:
                if g == 0:
                    o_ref[rows, :] = o
                    lse_ref[rows, :] = lse
                else:
                    lse_old = lse_ref[rows, :]
                    top = jnp.maximum(lse_old, lse)
                    e_old = jnp.exp2(lse_old - top)
                    e_new = jnp.exp2(lse - top)
                    o_ref[rows, :] = (e_old * o_ref[rows, :] + e_new * o) / (e_old + e_new)
                    lse_ref[rows, :] = top + jnp.log2(e_old + e_new)
            return carry

        assert (d * ntile) % DIL_UNROLL == 0
        lax.fori_loop(0, d * ntile // DIL_UNROLL, tile_body, 0)
    out_ref[...] = o_ref[...].astype(out_ref.dtype)


def _dilated_attention(zd3, dtab, bsz, seq):
    width = DIL_HPG * HEAD_DIM
    col = lambda part: pl.BlockSpec((None, seq, HEAD_DIM), lambda b, h, part=part: (b, 0, part * DIL_HPG + h))
    return pl.pallas_call(
        functools.partial(_dilated_kernel, seq=seq),
        grid=(bsz, DIL_HPG),
        in_specs=[col(part) for part in range(3 * DIL_GROUPS)]
        + [pl.BlockSpec((DIL_GROUPS, None, DIL_PW + DIL_TQ, DIL_TQ), lambda b, h: (0, h, 0, 0))],
        out_specs=pl.BlockSpec((None, seq, HEAD_DIM), lambda b, h: (b, 0, h)),
        out_shape=jax.ShapeDtypeStruct((bsz, seq, width), _BF),
        scratch_shapes=[pltpu.VMEM((seq, HEAD_DIM), _F32) for _ in range(5)],
        compiler_params=_cparams(("parallel", "arbitrary")),
        name="dilated_attention",
    )(*([zd3] * (3 * DIL_GROUPS)), dtab)


def _mla_kernel(qi_tab, ki_tab, q_ref, kn_ref, kp_ref, vt_ref, o_ref, *scratch, ntiles):
    tq = MLA_TQ
    c = (MLA_NOPE + MLA_ROPE) ** -0.5 * LOG2E
    per_head = len(scratch) // MLA_HPS
    heads = []
    for hd in range(MLA_HPS):
        sa_ref, sb_ref, mxa_ref, mxb_ref, m_ref, l_ref, acc_ref = scratch[hd * per_head:(hd + 1) * per_head]
        heads.append(dict(bufs=((sa_ref, mxa_ref), (sb_ref, mxb_ref)), state=(m_ref, l_ref, acc_ref),
                          qcols=slice(hd * 2 * LANES, (hd + 1) * 2 * LANES),
                          kcols=slice(hd * MLA_NOPE, (hd + 1) * MLA_NOPE), vrows=slice(hd * MLA_V, (hd + 1) * MLA_V)))
        _flash_reset(m_ref, l_ref, acc_ref)
    jj = lax.broadcasted_iota(jnp.int32, (tq, tq), 0)
    ii = lax.broadcasted_iota(jnp.int32, (tq, tq), 1)

    def rows_of(i):
        return pl.ds(pl.multiple_of(i * tq, tq), tq)

    def scores_into(slot, t):
        rows = rows_of(ki_tab[t])
        q_rows = rows_of(qi_tab[t])
        for hd in heads:
            k = jnp.concatenate([kn_ref[rows, hd["kcols"]], kp_ref[rows, :]], axis=1)
            _store_scores(hd["bufs"][slot], _dot_nt(k, q_ref[q_rows, hd["qcols"]]))

    def consume(slot, t, diagonal):
        rows = rows_of(ki_tab[t])
        for hd in heads:
            s_ref, mx_ref = hd["bufs"][slot]
            m_ref, l_ref, acc_ref = hd["state"]
            s = s_ref[...]
            tile_max = mx_ref[...]
            if diagonal:
                s = jnp.where(jj <= ii, s, NEG)
                tile_max = None
            _flash_update(s, vt_ref[hd["vrows"], rows], c, m_ref, l_ref, acc_ref, tile_max)
            if diagonal:
                o_ref[rows_of(qi_tab[t]), hd["vrows"]] = jnp.transpose(acc_ref[...] / l_ref[...]).astype(o_ref.dtype)
                _flash_reset(m_ref, l_ref, acc_ref)

    scores_into(0, 0)

    def pair(p, carry):
        t0 = 2 * p
        d0 = ki_tab[t0] == qi_tab[t0]
        d1 = ki_tab[t0 + 1] == qi_tab[t0 + 1]
        for x, y in ((False, False), (True, False), (False, True)):
            @pl.when(jnp.logical_and(d0 == x, d1 == y))
            def _(x=x, y=y):
                scores_into(1, t0 + 1)
                consume(0, t0, x)
                scores_into(0, t0 + 2)
                consume(1, t0 + 1, y)
        return carry

    lax.fori_loop(0, ntiles // 2, pair, 0)


def _mla_attention(q, kn, kpe, vt, bsz, seq):
    tq = MLA_TQ
    pairs = [(qi, ki) for qi in range(seq // tq) for ki in range(qi + 1)]
    ntiles = len(pairs)
    diag = [qi == ki for qi, ki in pairs]
    assert ntiles % 2 == 0 and not any(diag[t] and diag[t + 1] for t in range(0, ntiles, 2))
    pairs.append((0, 0))
    qi_tab = jnp.asarray([p[0] for p in pairs], jnp.int32)
    ki_tab = jnp.asarray([p[1] for p in pairs], jnp.int32)
    smem = pl.BlockSpec(memory_space=pltpu.SMEM)
    return pl.pallas_call(
        functools.partial(_mla_kernel, ntiles=ntiles),
        grid=(bsz, MLA_HEADS // MLA_HPS),
        in_specs=[
            smem, smem,
            pl.BlockSpec((None, seq, MLA_HPS * 2 * LANES), lambda b, h: (b, 0, h)),
            pl.BlockSpec((None, seq, MLA_HPS * MLA_NOPE), lambda b, h: (b, 0, h)),
            pl.BlockSpec((None, seq, LANES), lambda b, h: (b, 0, 0)),
            pl.BlockSpec((MLA_HPS * MLA_V, seq), lambda b, h: (h, b)),
        ],
        out_specs=pl.BlockSpec((None, seq, MLA_HPS * MLA_V), lambda b, h: (b, 0, h)),
        out_shape=jax.ShapeDtypeStruct((bsz, seq, MLA_HEADS * MLA_V), _BF),
        scratch_shapes=MLA_HPS * ([pltpu.VMEM((tq, tq), _F32) for _ in range(2)]
                                  + [pltpu.VMEM((1, tq), _F32) for _ in range(4)]
                                  + [pltpu.VMEM((MLA_V, tq), _F32)]),
        compiler_params=_cparams(("parallel", "arbitrary")),
        name="mla_attention",
    )(qi_tab, ki_tab, q, kn, kpe, vt)


def _residual_norm_store(x_new, nw_ref, x_out_ref, h_out_ref):
    x_out_ref[...] = x_new
    y = x_new * lax.rsqrt(jnp.mean(x_new * x_new, axis=-1, keepdims=True) + EPS)
    h_out_ref[...] = (y * nw_ref[...]).astype(h_out_ref.dtype)


def _out_even_kernel(x_ref, oa_ref, ga_ref, ob_ref, gb_ref, wa_ref, wb_ref, nw_ref, x_out_ref, h_out_ref):
    mixed_a = (oa_ref[...] * _silu(ga_ref[...])).astype(_BF)
    mixed_b = (ob_ref[...] * _silu(gb_ref[...])).astype(_BF)
    x_new = x_ref[...] + _dot(mixed_a, wa_ref[...]) + _dot(mixed_b, wb_ref[...])
    _residual_norm_store(x_new, nw_ref, x_out_ref, h_out_ref)


def _out_even(x2d, o_a, o_b, zf, w_out, nw_next, zf_cols, h_dtype):
    m, d = x2d.shape
    tm = 256
    wa = A_GATE
    wbd = B_GATE
    row = lambda width, cb=0: pl.BlockSpec((tm, width), lambda i, cb=cb: (i, cb))
    full = lambda r, c: pl.BlockSpec((r, c), lambda i: (0, 0))
    return pl.pallas_call(
        _out_even_kernel,
        grid=(m // tm,),
        in_specs=[row(d), row(wa), row(wa, zf_cols["gate_a"] // wa), row(wbd), row(wbd, zf_cols["gate_b"] // wbd),
                  full(wa, d), full(wbd, d), full(1, d)],
        out_specs=[row(d), row(d)],
        out_shape=[jax.ShapeDtypeStruct((m, d), _F32), jax.ShapeDtypeStruct((m, d), h_dtype)],
        compiler_params=_cparams(("parallel",)),
        name="out_proj_even",
    )(x2d, o_a, zf, o_b, zf, w_out[:wa], w_out[wa:], nw_next.reshape(1, d).astype(_F32))


def _out_odd_kernel(x_ref, oc_ref, gc_ref, w_ref, nw_ref, *out_refs):
    mixed = (oc_ref[...] * _silu(gc_ref[...])).astype(_BF)
    x_new = x_ref[...] + _dot(mixed, w_ref[...])
    if len(out_refs) == 2:
        _residual_norm_store(x_new, nw_ref, *out_refs)
    else:
        y = x_new * lax.rsqrt(jnp.mean(x_new * x_new, axis=-1, keepdims=True) + EPS)
        out_refs[0][...] = (y * nw_ref[...]).astype(out_refs[0].dtype)


def _out_odd(x2d, o_c, z, w_out, nw_next, gate_col_block, h_dtype, keep_x):
    m, d = x2d.shape
    tm = 256
    width = MLA_HEADS * MLA_V
    row = lambda w, cb=0: pl.BlockSpec((tm, w), lambda i, cb=cb: (i, cb))
    h_shape = jax.ShapeDtypeStruct((m, d), h_dtype)
    outs = pl.pallas_call(
        _out_odd_kernel,
        grid=(m // tm,),
        in_specs=[row(d), row(width), row(width, gate_col_block),
                  pl.BlockSpec((width, d), lambda i: (0, 0)), pl.BlockSpec((1, d), lambda i: (0, 0))],
        out_specs=[row(d), row(d)] if keep_x else [row(d)],
        out_shape=[jax.ShapeDtypeStruct((m, d), _F32), h_shape] if keep_x else [h_shape],
        compiler_params=_cparams(("parallel",)),
        name="out_proj_odd",
    )(x2d, o_c, z, w_out, nw_next.reshape(1, d).astype(_F32))
    return (outs[0], outs[1]) if keep_x else (None, outs[0])


def _bucket_of_distance(n):
    d = np.arange(n)
    max_exact = NUM_BUCKETS // 2
    df = np.maximum(d, 1).astype(np.float32)
    large = max_exact + (np.log(df / max_exact) / math.log(MAX_DISTANCE / max_exact)
                         * (NUM_BUCKETS - max_exact)).astype(np.int32)
    large = np.minimum(large, NUM_BUCKETS - 1)
    return np.where(d < max_exact, d, large).astype(np.int32)


def _toeplitz_tiles(v, n, nd):
    hh = v.shape[0]
    vp = jnp.concatenate([jnp.zeros((hh, n), v.dtype), v, jnp.zeros((hh, n), v.dtype)], axis=1)
    idx = (n - np.arange(2 * n)) % (2 * n)
    tiles = []
    for dl in range(nd):
        w = vp[:, dl * n:dl * n + 2 * n]
        c = w[:, idx]
        flat = jnp.tile(c, (1, n))[:, :n * (2 * n - 1)]
        tiles.append(flat.reshape(hh, n, 2 * n - 1)[:, :, :n])
    return jnp.stack(tiles, axis=0)


def _bias_tables(rel_bias, seq):
    bd = rel_bias.astype(_F32)[_bucket_of_distance(seq)].T
    bd_nsa = bd[:NSA_HEADS]
    nd = seq // NSA_TQ
    bt = _toeplitz_tiles(bd_nsa * LOG2E, NSA_TQ, nd)
    bt = bt.reshape(nd, NSA_GROUPS, NSA_HPG, NSA_TQ, NSA_TQ).transpose(1, 0, 4, 2, 3)
    bt = bt.reshape(NSA_GROUPS, nd, NSA_TQ, NSA_HPG * NSA_TQ)
    nblk = seq // NSA_BLOCK
    off = NSA_BLOCK * (nblk - 1) + NSA_BLOCK - 1
    bdp = jnp.concatenate([jnp.zeros((NSA_HEADS, off), _F32), bd_nsa], axis=1)
    bc = jnp.stack([bdp[:, off - (NSA_BLOCK * n + NSA_BLOCK - 1):off - (NSA_BLOCK * n + NSA_BLOCK - 1) + seq]
                    for n in range(nblk)], axis=1)
    bc = bc.reshape(NSA_GROUPS, NSA_HPG, nblk, seq)
    rows, cols = DIL_TQ, DIL_PW + DIL_TQ
    period = rows + cols
    dtabs = []
    for g, (w, d) in enumerate(DIL_PAIRS):
        heads = bd[NSA_HEADS + g * DIL_HPG:NSA_HEADS + (g + 1) * DIL_HPG] * LOG2E
        m = w // d + 1
        assert m == DIL_PW + 1
        by_offset = heads[:, 0:m * d:d]
        cyc = jnp.concatenate([by_offset[:, ::-1], jnp.full((DIL_HPG, period - m), NEG, _F32)], axis=1)
        flat = jnp.tile(cyc, (1, rows))[:, :rows * (period - 1)]
        by_query = flat.reshape(DIL_HPG, rows, period - 1)[:, :, :cols]
        dtabs.append(by_query.transpose(0, 2, 1))
    return bt, bc, jnp.stack(dtabs, axis=0)


def _rope_tables(seq):
    half = MLA_ROPE // 2
    freqs = 1.0 / (ROPE_THETA ** (jnp.arange(half, dtype=_F32) / half))
    ang = jnp.arange(seq).astype(_F32)[:, None] * freqs[None, :]
    zeros = jnp.zeros((seq, LANES - MLA_ROPE), _F32)
    cos = jnp.concatenate([jnp.cos(ang), jnp.cos(ang), zeros], axis=1)
    sin = jnp.concatenate([jnp.sin(ang), jnp.sin(ang), zeros], axis=1)
    return cos, sin


def _rot_cols(w):
    half = w.shape[-1] // 2
    return jnp.concatenate([-w[..., half:], w[..., :half]], axis=-1)


ZF_COLS = {"gate_a": 0, "gate_b": A_GATE, "kcvc": A_GATE + B_GATE, "g3": A_GATE + B_GATE + 4 * HEAD_DIM}


def _even_weights(w_in):
    w_in = w_in.astype(_BF)
    o = 0
    q_a = w_in[:, o:o + A_Q]; o += A_Q
    kv_a = w_in[:, o:o + A_KV]; o += A_KV
    g3 = w_in[:, o:o + A_G3]; o += A_G3
    gate_a = w_in[:, o:o + A_GATE]; o += A_GATE
    qkv_b = w_in[:, o:o + B_QKV]; o += B_QKV
    gate_b = w_in[:, o:o + B_GATE]
    gw = NSA_GROUPS * HEAD_DIM
    kc_vc, ks, vs, kw, vw = (kv_a[:, :2 * gw], kv_a[:, 2 * gw:3 * gw], kv_a[:, 3 * gw:4 * gw],
                             kv_a[:, 4 * gw:5 * gw], kv_a[:, 5 * gw:6 * gw])
    wb = jnp.concatenate([q_a, ks, kw], axis=1).astype(_BF)
    wvt = jnp.concatenate([vs, vw], axis=1).T.astype(_BF)
    dw = DIL_HPG * HEAD_DIM
    nq = DIL_HEADS * HEAD_DIM
    wd = jnp.concatenate([qkv_b[:, part * nq + g * dw:part * nq + (g + 1) * dw]
                          for g in range(DIL_GROUPS) for part in range(3)], axis=1).astype(_BF)
    g3_blocks = []
    for g in range(NSA_GROUPS):
        cols = [(g * NSA_HPG + h) * 3 + j for j in range(3) for h in range(NSA_HPG)]
        blk = g3[:, np.asarray(cols)]
        g3_blocks.append(jnp.pad(blk, ((0, 0), (0, LANES - len(cols)))))
    wf = jnp.concatenate([gate_a, gate_b, kc_vc] + g3_blocks, axis=1).astype(_BF)
    return wb, wvt, wd, wf


def _even_layer(x2d, h, w_in, cmp_pos, cmp_w1, cmp_w2, w_out, nw_next, tables, bsz, seq, h_dtype):
    bt, bc, dtab, onehot = tables
    wb, wvt, wd, wf = _even_weights(w_in)
    zb = _matmul(h, wb, _BF)
    vt = _matmul_nt(wvt, h, _BF)
    zd = _matmul(h, wd, _BF)
    zf = _matmul(h, wf, _F32)
    nblk = seq // NSA_BLOCK
    kcvc = zf[:, ZF_COLS["kcvc"]:ZF_COLS["kcvc"] + 4 * HEAD_DIM]
    kcvc_t = kcvc.reshape(bsz * nblk, NSA_BLOCK, 4, HEAD_DIM).transpose(2, 1, 0, 3)
    kvc = _compress(kcvc_t, cmp_pos.astype(_F32), cmp_w1.astype(_BF), cmp_w2.astype(_BF))
    kvc = kvc.reshape(4, bsz, nblk, HEAD_DIM)
    zf3 = zf.reshape(bsz, seq, zf.shape[-1])
    o_a = _nsa_attention(zb.reshape(bsz, seq, -1), vt, zf3, kvc, onehot, bt, bc, bsz, seq)
    o_b = _dilated_attention(zd.reshape(bsz, seq, -1), dtab, bsz, seq)
    return _out_even(x2d, o_a.reshape(bsz * seq, -1), o_b.reshape(bsz * seq, -1), zf, w_out.astype(_BF), nw_next,
                     ZF_COLS, h_dtype)


def _odd_layer(x2d, h, w_in, q_norm, w_qb, kv_norm, w_kvb, w_out, nw_next, rope, bsz, seq, h_dtype, keep_x):
    cos, sin = rope
    w_in, w_qb, w_kvb = w_in.astype(_BF), w_qb.astype(_BF), w_kvb.astype(_BF)
    o = 0
    w_cq = w_in[:, o:o + MLA_Q_RANK]; o += MLA_Q_RANK
    w_ckv = w_in[:, o:o + MLA_KV_RANK]; o += MLA_KV_RANK
    w_kpe = w_in[:, o:o + MLA_ROPE]; o += MLA_ROPE
    w_gate = w_in[:, o:]
    w1 = jnp.concatenate([w_cq, w_ckv, w_gate], axis=1).astype(_BF)
    z = _matmul(h, w1, _F32)
    zk = _matmul(h, jnp.concatenate([w_kpe, _rot_cols(w_kpe)], axis=1).astype(_BF), _F32)
    gate_width = MLA_HEADS * MLA_V
    q_latent = (q_norm, MLA_Q_RANK, 0)
    kv_latent = (kv_norm, MLA_KV_RANK, MLA_Q_RANK // MLA_KV_RANK)
    kpe = _rope_cols(zk, 0, cos, sin, seq)
    wq = w_qb.reshape(MLA_Q_RANK, MLA_HEADS, MLA_NOPE + MLA_ROPE)
    wq_pe = wq[:, :, MLA_NOPE:]
    wq = jnp.concatenate([wq[:, :, :MLA_NOPE], wq_pe, _rot_cols(wq_pe)], axis=-1)
    q = _matmul_rope(z, q_latent, wq.reshape(MLA_Q_RANK, MLA_HEADS * 2 * LANES).astype(_BF), cos, sin, seq)
    wkv = w_kvb.reshape(MLA_KV_RANK, MLA_HEADS, MLA_NOPE + MLA_V)
    wk = wkv[:, :, :MLA_NOPE].reshape(MLA_KV_RANK, -1).astype(_BF)
    wv_t = wkv[:, :, MLA_NOPE:].reshape(MLA_KV_RANK, -1).T.astype(_BF)
    kn = _matmul(z, wk, _BF, norm=kv_latent)
    vt = _matmul_nt(wv_t, z, _BF, norm=kv_latent)
    o_c = _mla_attention(q.reshape(bsz, seq, -1), kn.reshape(bsz, seq, -1), kpe.reshape(bsz, seq, LANES),
                         vt, bsz, seq)
    gate_block = (MLA_Q_RANK + MLA_KV_RANK) // gate_width
    return _out_odd(x2d, o_c.reshape(bsz * seq, -1), z, w_out.astype(_BF), nw_next, gate_block, h_dtype, keep_x)


def kernel(x, rel_bias, norm_w, final_norm_w, ev_w_in, nsa_cmp_pos, nsa_cmp_w1, nsa_cmp_w2, ev_w_out,
           od_w_in, mla_q_norm, mla_w_qb, mla_kv_norm, mla_w_kvb, od_w_out):
    bsz, seq, d = x.shape
    depth = norm_w.shape[0]
    assert seq % MLA_TQ == 0 and seq % (DIL_TQ * DIL_PAIRS[-1][1]) == 0 and seq // NSA_BLOCK <= LANES
    assert NSA_WINDOW == 2 * NSA_TQ
    bt, bc, dtab = _bias_tables(rel_bias, seq)
    blk_id = np.arange(seq)[:, None] // NSA_BLOCK
    onehot = jnp.asarray(blk_id == np.arange(LANES)[None, :], dtype=_BF)
    tables = (bt, bc, dtab, onehot)
    rope = _rope_tables(seq)
    x2d = x.reshape(bsz * seq, d).astype(_F32)
    h = _rmsnorm(x2d, norm_w[0], d, 0, _BF)
    for l in range(depth):
        last = l == depth - 1
        nw_next = final_norm_w if last else norm_w[l + 1]
        h_dtype = _F32 if last else _BF
        i = l // 2
        if l % 2 == 0:
            x2d, h = _even_layer(x2d, h, ev_w_in[i], nsa_cmp_pos[i], nsa_cmp_w1[i], nsa_cmp_w2[i],
                                 ev_w_out[i], nw_next, tables, bsz, seq, h_dtype)
        else:
            x2d, h = _odd_layer(x2d, h, od_w_in[i], mla_q_norm[i], mla_w_qb[i], mla_kv_norm[i],
                                mla_w_kvb[i], od_w_out[i], nw_next, rope, bsz, seq, h_dtype, keep_x=not last)
    return h.reshape(bsz, seq, d)
```

```python
import functools
import math

import numpy as np
import jax
import jax.numpy as jnp
from jax import lax
from jax.experimental import pallas as pl
from jax.experimental.pallas import tpu as pltpu

HEAD_DIM = 128
EPS = 1e-6
NEG = -1e30
NUM_BUCKETS = 32
MAX_DISTANCE = 2048
NSA_HEADS = 8
NSA_GROUPS = 2
NSA_HPG = 4
NSA_BLOCK = 64
NSA_TOPK = 16
NSA_WINDOW = 512
DIL_PAIRS = ((128, 1), (512, 4), (2048, 16))
DIL_GROUPS = 3
DIL_HPG = 4
DIL_HEADS = 12
MLA_HEADS = 16
MLA_Q_RANK = 1536
MLA_KV_RANK = 512
MLA_NOPE = 128
MLA_ROPE = 64
MLA_V = 128
ROPE_THETA = 10000.0

A_Q = NSA_HEADS * HEAD_DIM
A_KV = 6 * NSA_GROUPS * HEAD_DIM
A_G3 = 3 * NSA_HEADS
A_GATE = NSA_HEADS * HEAD_DIM
B_QKV = 3 * DIL_HEADS * HEAD_DIM
B_GATE = DIL_HPG * HEAD_DIM

LANES = 128
SUBLANES = 8
VMEM_LIMIT = 48 * 1024 * 1024

NSA_TQ = 256
NSA_CHAINS = 1
DIL_TQ = 256
DIL_PW = 128
DIL_UNROLL = 8
MLA_TQ = 1024
OUT_TM = 512
MLA_HPS = 2
LOG2E = math.log2(math.e)

_BF = jnp.bfloat16
_F32 = jnp.float32


def _cparams(sem):
    return pltpu.CompilerParams(dimension_semantics=sem, vmem_limit_bytes=VMEM_LIMIT)


def _dot(a, b):
    return jnp.dot(a, b, preferred_element_type=_F32)


def _dot_nt(a, b):
    return lax.dot_general(a, b, (((1,), (1,)), ((), ())), preferred_element_type=_F32)


def _silu(x):
    return x * (1.0 / (1.0 + jnp.exp(-x)))


def _sigmoid(x):
    return 1.0 / (1.0 + jnp.exp(-x))


def _pick_tile(n, candidates):
    for c in candidates:
        if n % c == 0:
            return c
    return n


def _rmsnorm_kernel(x_ref, w_ref, o_ref):
    x = x_ref[...]
    y = x * lax.rsqrt(jnp.mean(x * x, axis=-1, keepdims=True) + EPS)
    o_ref[...] = (y * w_ref[...]).astype(o_ref.dtype)


def _rmsnorm(x2d, w, width, col_block, out_dtype):
    m = x2d.shape[0]
    tm = _pick_tile(m, (512, 256, 128))
    return pl.pallas_call(
        _rmsnorm_kernel,
        grid=(m // tm,),
        in_specs=[pl.BlockSpec((tm, width), lambda i: (i, col_block)),
                  pl.BlockSpec((1, width), lambda i: (0, 0))],
        out_specs=pl.BlockSpec((tm, width), lambda i: (i, 0)),
        out_shape=jax.ShapeDtypeStruct((m, width), out_dtype),
        compiler_params=_cparams(("parallel",)),
        name="rmsnorm",
    )(x2d, w.reshape(1, width).astype(_F32))


def _matmul_kernel(a_ref, w_ref, o_ref):
    o_ref[...] = _dot(a_ref[...], w_ref[...]).astype(o_ref.dtype)


def _normed_lhs(a_ref, nw_ref, an_ref):
    @pl.when(pl.program_id(1) == 0)
    def _():
        x = a_ref[...]
        y = x * lax.rsqrt(jnp.mean(x * x, axis=-1, keepdims=True) + EPS)
        an_ref[...] = (y * nw_ref[...]).astype(an_ref.dtype)
    return an_ref[...]


def _norm_matmul_kernel(a_ref, nw_ref, w_ref, o_ref, an_ref):
    o_ref[...] = _dot(_normed_lhs(a_ref, nw_ref, an_ref), w_ref[...]).astype(o_ref.dtype)


def _lhs_specs(a, norm, tm):
    if norm is None:
        k = a.shape[1]
        return k, [pl.BlockSpec((tm, k), lambda i, j: (i, 0))], [], []
    nw, k, cb = norm
    specs = [pl.BlockSpec((tm, k), lambda i, j: (i, cb)), pl.BlockSpec((1, k), lambda i, j: (0, 0))]
    return k, specs, [nw.reshape(1, k).astype(_F32)], [pltpu.VMEM((tm, k), _BF)]


def _matmul(a, w, out_dtype, norm=None):
    m = a.shape[0]
    n = w.shape[1]
    tm = _pick_tile(m, (1024, 512, 256, 128))
    tn = _pick_tile(n, (1536, 1024, 768, 512, 384, 256, 128))
    k, a_specs, a_args, scratch = _lhs_specs(a, norm, tm)
    return pl.pallas_call(
        _matmul_kernel if norm is None else _norm_matmul_kernel,
        grid=(m // tm, n // tn),
        in_specs=a_specs + [pl.BlockSpec((k, tn), lambda i, j: (0, j))],
        out_specs=pl.BlockSpec((tm, tn), lambda i, j: (i, j)),
        out_shape=jax.ShapeDtypeStruct((m, n), out_dtype),
        scratch_shapes=scratch,
        compiler_params=_cparams(("parallel", "arbitrary")),
        name="matmul" if norm is None else "norm_matmul",
    )(a, *a_args, w)


def _matmul_nt_kernel(w_ref, a_ref, o_ref):
    o_ref[...] = _dot_nt(w_ref[...], a_ref[...]).astype(o_ref.dtype)


def _norm_matmul_nt_kernel(w_ref, a_ref, nw_ref, o_ref, an_ref):
    o_ref[...] = _dot_nt(w_ref[...], _normed_lhs(a_ref, nw_ref, an_ref)).astype(o_ref.dtype)


def _matmul_nt(w_t, a, out_dtype, norm=None):
    n = w_t.shape[0]
    m = a.shape[0]
    tm = _pick_tile(m, (1024, 512, 256, 128))
    tn = _pick_tile(n, (1024, 512, 256, 128))
    k, a_specs, a_args, scratch = _lhs_specs(a, norm, tm)
    return pl.pallas_call(
        _matmul_nt_kernel if norm is None else _norm_matmul_nt_kernel,
        grid=(m // tm, n // tn),
        in_specs=[pl.BlockSpec((tn, k), lambda i, j: (j, 0))] + a_specs,
        out_specs=pl.BlockSpec((tn, tm), lambda i, j: (j, i)),
        out_shape=jax.ShapeDtypeStruct((n, m), out_dtype),
        scratch_shapes=scratch,
        compiler_params=_cparams(("parallel", "arbitrary")),
        name="matmul_nt" if norm is None else "norm_matmul_nt",
    )(w_t, a, *a_args)


def _rope_chunk(chunk, cos, sin):
    return chunk * cos + pltpu.roll(chunk, 64, 1) * sin


def _matmul_rope_kernel(a_ref, nw_ref, w_ref, cos_ref, sin_ref, o_ref, an_ref, *, heads_per_tile):
    acc = _dot(_normed_lhs(a_ref, nw_ref, an_ref), w_ref[...])
    cos = cos_ref[...]
    sin = sin_ref[...]
    for h in range(heads_per_tile):
        base = h * 2 * LANES
        o_ref[:, base:base + LANES] = acc[:, base:base + LANES].astype(o_ref.dtype)
        o_ref[:, base + LANES:base + 2 * LANES] = _rope_chunk(
            acc[:, base + LANES:base + 2 * LANES], cos, sin).astype(o_ref.dtype)


def _matmul_rope(a, norm, w, cos, sin, seq):
    m = a.shape[0]
    n = w.shape[1]
    tm = _pick_tile(seq, (1024, 512, 256, 128))
    tn = 2048
    tpb = seq // tm
    k, a_specs, a_args, scratch = _lhs_specs(a, norm, tm)
    return pl.pallas_call(
        functools.partial(_matmul_rope_kernel, heads_per_tile=tn // (2 * LANES)),
        grid=(m // tm, n // tn),
        in_specs=a_specs + [pl.BlockSpec((k, tn), lambda i, j: (0, j)),
                            pl.BlockSpec((tm, LANES), lambda i, j: (i % tpb, 0)),
                            pl.BlockSpec((tm, LANES), lambda i, j: (i % tpb, 0))],
        out_specs=pl.BlockSpec((tm, tn), lambda i, j: (i, j)),
        out_shape=jax.ShapeDtypeStruct((m, n), _BF),
        scratch_shapes=scratch,
        compiler_params=_cparams(("parallel", "arbitrary")),
        name="matmul_rope",
    )(a, *a_args, w, cos, sin)


def _rope_cols_kernel(x_ref, cos_ref, sin_ref, o_ref):
    o_ref[...] = _rope_chunk(x_ref[...], cos_ref[...], sin_ref[...]).astype(o_ref.dtype)


def _rope_cols(z2d, col_block, cos, sin, seq):
    m = z2d.shape[0]
    tm = _pick_tile(seq, (512, 256, 128))
    tpb = seq // tm
    return pl.pallas_call(
        _rope_cols_kernel,
        grid=(m // tm,),
        in_specs=[pl.BlockSpec((tm, LANES), lambda i: (i, col_block)),
                  pl.BlockSpec((tm, LANES), lambda i: (i % tpb, 0)),
                  pl.BlockSpec((tm, LANES), lambda i: (i % tpb, 0))],
        out_specs=pl.BlockSpec((tm, LANES), lambda i: (i, 0)),
        out_shape=jax.ShapeDtypeStruct((m, LANES), _BF),
        compiler_params=_cparams(("parallel",)),
        name="rope_kpe",
    )(z2d, cos, sin)


def _compress_kernel(blk_ref, pos_ref, w1_ref, w2_ref, o_ref, acc_ref, *, lt):
    li = pl.program_id(1)

    @pl.when(li == 0)
    def _():
        acc_ref[...] = jnp.zeros_like(acc_ref)

    acc = acc_ref[...]
    for l in range(lt):
        a = (blk_ref[l] + pos_ref[l:l + 1, :]).astype(_BF)
        acc = acc + _dot(a, w1_ref[l])
    acc_ref[...] = acc

    @pl.when(li == pl.num_programs(1) - 1)
    def _():
        hid = _silu(acc_ref[...]).astype(_BF)
        o_ref[...] = _dot(hid, w2_ref[...])


def _compress(kcvc_t, pos, w1, w2):
    _, L, R, _ = kcvc_t.shape
    lt = 16
    return pl.pallas_call(
        functools.partial(_compress_kernel, lt=lt),
        grid=(4, L // lt),
        in_specs=[pl.BlockSpec((None, lt, R, HEAD_DIM), lambda c, l: (c, l, 0, 0)),
                  pl.BlockSpec((None, lt, HEAD_DIM), lambda c, l: (c // 2, l, 0)),
                  pl.BlockSpec((None, lt, HEAD_DIM, HEAD_DIM), lambda c, l: (c // 2, l, 0, 0)),
                  pl.BlockSpec((None, HEAD_DIM, HEAD_DIM), lambda c, l: (c // 2, 0, 0))],
        out_specs=pl.BlockSpec((None, R, HEAD_DIM), lambda c, l: (c, 0, 0)),
        out_shape=jax.ShapeDtypeStruct((4, R, HEAD_DIM), _F32),
        scratch_shapes=[pltpu.VMEM((R, HEAD_DIM), _F32)],
        compiler_params=_cparams(("parallel", "arbitrary")),
        name="nsa_compress",
    )(kcvc_t, pos, w1.reshape(2, L, HEAD_DIM, HEAD_DIM), w2)


def _flash_reset(m_ref, l_ref, acc_ref):
    m_ref[...] = jnp.full(m_ref.shape, -jnp.inf, _F32)
    l_ref[...] = jnp.zeros(l_ref.shape, _F32)
    acc_ref[...] = jnp.zeros(acc_ref.shape, _F32)


def _flash_update(s, vt, c, m_ref, l_ref, acc_ref, tile_max=None):
    m_prev = m_ref[...]
    if tile_max is None:
        tile_max = jnp.max(s, axis=0, keepdims=True)
    m_new = jnp.maximum(m_prev, tile_max)
    if c is None:
        alpha = jnp.exp2(m_prev - m_new)
        p = jnp.exp2(s - m_new)
    else:
        alpha = jnp.exp2((m_prev - m_new) * c)
        p = jnp.exp2((s - m_new) * c)
    l_ref[...] = alpha * l_ref[...] + jnp.sum(p, axis=0, keepdims=True)
    acc_ref[...] = alpha * acc_ref[...] + _dot(vt, p.astype(_BF))
    m_ref[...] = m_new


def _store_scores(buf, s):
    s_ref, mx_ref = buf
    s_ref[...] = s
    mx_ref[...] = jnp.max(s, axis=0, keepdims=True)


def _causal_flash(n_off, scores_into, consume, buf_a, buf_b, primed=False):
    if not primed:
        scores_into(buf_a, 0)

    def pair(p, carry):
        k0 = 2 * p
        scores_into(buf_b, k0 + 1)
        consume(buf_a, k0, False)
        scores_into(buf_a, k0 + 2)
        consume(buf_b, k0 + 1, False)
        return carry

    lax.fori_loop(0, n_off // 2, pair, 0)

    @pl.when(n_off % 2 == 1)
    def _():
        scores_into(buf_b, n_off)
        consume(buf_a, n_off - 1, False)
        consume(buf_b, n_off, True)

    @pl.when(n_off % 2 == 0)
    def _():
        consume(buf_a, n_off, True)


def _nsa_kernel(q_ref, ks_ref, kw_ref, vst_ref, vwt_ref, kc_ref, vc_ref, g3_ref, oh_ref, bt_ref, bc_ref,
                o_ref, *scratch, seq):
    tq = NSA_TQ
    hp = NSA_HPG
    n = hp * tq
    nc = n // NSA_CHAINS
    nblk = seq // NSA_BLOCK
    qi = pl.program_id(2)
    scale = HEAD_DIM ** -0.5
    c = scale * LOG2E
    t0 = qi * tq

    per_chain = len(scratch) // NSA_CHAINS
    chains = []
    for ch in range(NSA_CHAINS):
        (sa_ref, sb_ref, w0_ref, w1_ref, w2_ref, mxa_ref, mxb_ref, m_ref, l_ref, mw_ref, lw_ref,
         acc_s_ref, acc_w_ref) = scratch[ch * per_chain:(ch + 1) * per_chain]
        chains.append(dict(buf_a=(sa_ref, mxa_ref), buf_b=(sb_ref, mxb_ref), win=(w0_ref, w1_ref, w2_ref),
                           slc_state=(m_ref, l_ref, acc_s_ref), win_state=(mw_ref, lw_ref, acc_w_ref),
                           lanes=slice(ch * nc, (ch + 1) * nc)))

    qs = jnp.concatenate([q_ref[:, h * HEAD_DIM:(h + 1) * HEAD_DIM] for h in range(hp)], axis=0)

    def rows_of(kt):
        return pl.ds(pl.multiple_of(kt * tq, tq), tq)

    for delta in range(3):
        kw = kw_ref[rows_of(jnp.maximum(qi - delta, 0)), :]
        for ch in chains:
            ch["win"][delta][...] = _dot_nt(kw, qs[ch["lanes"], :]) * c + bt_ref[delta, :, ch["lanes"]]

    kc = kc_ref[...].astype(_BF)
    s_c = _dot_nt(kc, qs) * scale
    s_c = s_c + jnp.concatenate([bc_ref[h] for h in range(hp)], axis=1)
    n_col = lax.broadcasted_iota(jnp.int32, (nblk, tq), 0)
    t_row = t0 + lax.broadcasted_iota(jnp.int32, (nblk, tq), 1)
    n_all = lax.broadcasted_iota(jnp.int32, (nblk, n), 0)
    t_all = t0 + (lax.broadcasted_iota(jnp.int32, (nblk, n), 1) & (tq - 1))
    valid = t_all >= n_all * NSA_BLOCK + (NSA_BLOCK - 1)
    s_c = jnp.where(valid, s_c, NEG)
    m_c = jnp.max(s_c, axis=0, keepdims=True)
    p_c = jnp.where(valid, jnp.exp(s_c - m_c), 0.0)
    l_c = jnp.sum(p_c, axis=0, keepdims=True)
    p_c = p_c / jnp.where(l_c > 0.0, l_c, 1.0)
    imp = p_c[:, 0:tq]
    for h in range(1, hp):
        imp = imp + p_c[:, h * tq:(h + 1) * tq]

    pad_rows = LANES - nblk
    vc_pad = jnp.concatenate([vc_ref[...], jnp.zeros((pad_rows, HEAD_DIM), _F32)], axis=0)
    vc_t = jnp.transpose(vc_pad).astype(_BF)
    p_pad = jnp.concatenate([p_c, jnp.zeros((pad_rows, n), _F32)], axis=0).astype(_BF)
    o_cmp = _dot(vc_t, p_pad)

    cur = jnp.right_shift(t_row, int(math.log2(NSA_BLOCK)))
    forced = (n_col == 0) | (n_col == cur) | (n_col == cur - 1)
    impv = jnp.where(forced, jnp.inf, jnp.where(n_col > cur, -jnp.inf, imp))
    rank = jnp.zeros((nblk, tq), _F32)
    sub_row = lax.broadcasted_iota(jnp.int32, (SUBLANES, tq), 0)
    for i in range(nblk):
        row = impv[i:i + 1, :]
        lo = i // SUBLANES * SUBLANES
        parts = []
        if lo > 0:
            parts.append(jnp.where(row > impv[:lo], 1.0, 0.0))
        mid = impv[lo:lo + SUBLANES]
        parts.append(jnp.where(sub_row > i - lo,
                               jnp.where(row >= mid, 1.0, 0.0), jnp.where(row > mid, 1.0, 0.0)))
        if lo + SUBLANES < nblk:
            parts.append(jnp.where(row >= impv[lo + SUBLANES:], 1.0, 0.0))
        rank = rank + jnp.concatenate(parts, axis=0)
    sel = (rank < float(min(NSA_TOPK, nblk))) & (n_col <= cur)
    pen = jnp.where(sel, 0.0, NEG)
    pen = jnp.concatenate([pen, jnp.full((pad_rows, tq), NEG, _F32)], axis=0)
    pen_t = jnp.transpose(pen).astype(_BF)
    q_aug = jnp.concatenate([qs, jnp.concatenate([pen_t] * hp, axis=0)], axis=1)

    jj = lax.broadcasted_iota(jnp.int32, (tq, nc), 0)
    ii = lax.broadcasted_iota(jnp.int32, (tq, nc), 1) & (tq - 1)

    def slc_scores(side, ki):
        rows = rows_of(ki)
        k_aug = jnp.concatenate([ks_ref[rows, :], oh_ref[rows, :]], axis=1)
        for ch in chains:
            _store_scores(ch[side], _dot_nt(k_aug, q_aug[ch["lanes"], :]) * c + bt_ref[qi - ki, :, ch["lanes"]])

    def slc_consume(side, ki, diagonal):
        vt = vst_ref[:, rows_of(ki)]
        for ch in chains:
            s = ch[side][0][...]
            tile_max = ch[side][1][...]
            if diagonal:
                s = jnp.where(jj <= ii, s, NEG)
                tile_max = None
            _flash_update(s, vt, None, *ch["slc_state"], tile_max)

    slc_scores("buf_a", 0)

    def win_consume(delta, keep):
        vt = vwt_ref[:, rows_of(jnp.maximum(qi - delta, 0))]
        for ch in chains:
            _flash_update(jnp.where(keep, ch["win"][delta][...], NEG), vt, None, *ch["win_state"])

    for ch in chains:
        _flash_reset(*ch["win_state"])
        _flash_reset(*ch["slc_state"])
    win_consume(0, jj <= ii)
    win_consume(1, qi >= 1)
    win_consume(2, (jj > ii) & (qi >= 2))

    _causal_flash(qi, slc_scores, slc_consume, "buf_a", "buf_b", primed=True)

    g_t = jnp.transpose(_sigmoid(g3_ref[...]))
    for ch in chains:
        ch["o_win"] = ch["win_state"][2][...] / ch["win_state"][1][...]
        ch["o_slc"] = ch["slc_state"][2][...] / ch["slc_state"][1][...]
    for h in range(hp):
        ch = chains[h * tq // nc]
        cols = slice(h * tq, (h + 1) * tq)
        local = slice(h * tq % nc, h * tq % nc + tq)
        out_t = (g_t[0 * hp + h:0 * hp + h + 1, :] * o_cmp[:, cols]
                 + g_t[1 * hp + h:1 * hp + h + 1, :] * ch["o_slc"][:, local]
                 + g_t[2 * hp + h:2 * hp + h + 1, :] * ch["o_win"][:, local])
        o_ref[:, h * HEAD_DIM:(h + 1) * HEAD_DIM] = jnp.transpose(out_t).astype(o_ref.dtype)


def _nsa_attention(zb, vt, zf, kvc, onehot, bt, bc, bsz, seq):
    tq = NSA_TQ
    n = NSA_HPG * tq
    nblk = seq // NSA_BLOCK
    nd = seq // tq
    kcol = A_Q // HEAD_DIM
    g3col = ZF_COLS["g3"] // LANES
    return pl.pallas_call(
        functools.partial(_nsa_kernel, seq=seq),
        grid=(NSA_GROUPS, bsz, seq // tq),
        in_specs=[
            pl.BlockSpec((None, tq, NSA_HPG * HEAD_DIM), lambda g, b, i: (b, i, g)),
            pl.BlockSpec((None, seq, HEAD_DIM), lambda g, b, i: (b, 0, kcol + g)),
            pl.BlockSpec((None, seq, HEAD_DIM), lambda g, b, i: (b, 0, kcol + NSA_GROUPS + g)),
            pl.BlockSpec((HEAD_DIM, seq), lambda g, b, i: (g, b)),
            pl.BlockSpec((HEAD_DIM, seq), lambda g, b, i: (NSA_GROUPS + g, b)),
            pl.BlockSpec((None, None, nblk, HEAD_DIM), lambda g, b, i: (g, b, 0, 0)),
            pl.BlockSpec((None, None, nblk, HEAD_DIM), lambda g, b, i: (2 + g, b, 0, 0)),
            pl.BlockSpec((None, tq, LANES), lambda g, b, i: (b, i, g3col + g)),
            pl.BlockSpec((seq, LANES), lambda g, b, i: (0, 0)),
            pl.BlockSpec((None, nd, tq, n), lambda g, b, i: (g, 0, 0, 0), pipeline_mode=pl.Buffered(1)),
            pl.BlockSpec((None, NSA_HPG, nblk, tq), lambda g, b, i: (g, 0, 0, i)),
        ],
        out_specs=pl.BlockSpec((None, tq, NSA_HPG * HEAD_DIM), lambda g, b, i: (b, i, g)),
        out_shape=jax.ShapeDtypeStruct((bsz, seq, NSA_HEADS * HEAD_DIM), _BF),
        scratch_shapes=NSA_CHAINS * ([pltpu.VMEM((tq, n // NSA_CHAINS), _F32) for _ in range(5)]
                                     + [pltpu.VMEM((1, n // NSA_CHAINS), _F32) for _ in range(6)]
                                     + [pltpu.VMEM((HEAD_DIM, n // NSA_CHAINS), _F32) for _ in range(2)]),
        compiler_params=_cparams(("arbitrary", "arbitrary", "arbitrary")),
        name="nsa_attention",
    )(zb, zb, zb, vt, vt, kvc, kvc, zf, onehot, bt, bc)


def _dilated_kernel(*refs, seq):
    qkv_refs, (tab_ref, out_ref, qs_ref, ks_ref, vs_ref, lse_ref, o_ref) = refs[:9], refs[9:]
    tq, pw = DIL_TQ, DIL_PW
    c = HEAD_DIM ** -0.5 * LOG2E
    for g, (_, d) in enumerate(DIL_PAIRS):
        q_ref, k_ref, v_ref = qkv_refs[3 * g:3 * g + 3]
        qs_ref[...] = q_ref[...].astype(_F32)
        ks_ref[...] = k_ref[...].astype(_F32)
        vs_ref[...] = v_ref[...].astype(_F32)
        ntile = seq // d // tq
        tab_p = tab_ref[g, :pw, :]
        tab_d = tab_ref[g, pw:, :]

        def scores(idx, d=d, ntile=ntile, tab_d=tab_d, tab_p=tab_p):
            r = idx // ntile
            tile = idx % ntile
            u0 = tile * tq
            rows = pl.ds(u0 * d + r, tq, stride=d)
            rows_p = pl.ds(jnp.maximum(u0 - pw, 0) * d + r, pw, stride=d)
            q = qs_ref[rows, :].astype(_BF)
            s_d = _dot_nt(ks_ref[rows, :].astype(_BF), q) * c + tab_d
            s_p = jnp.where(tile > 0, _dot_nt(ks_ref[rows_p, :].astype(_BF), q) * c + tab_p, NEG)
            return rows, rows_p, s_d, s_p

        def softmax(rows, rows_p, s_d, s_p):
            m = jnp.maximum(jnp.max(s_d, axis=0, keepdims=True), jnp.max(s_p, axis=0, keepdims=True))
            p_d = jnp.exp2(s_d - m)
            p_p = jnp.exp2(s_p - m)
            l = jnp.sum(p_d, axis=0, keepdims=True) + jnp.sum(p_p, axis=0, keepdims=True)
            return rows, rows_p, p_d.astype(_BF), p_p.astype(_BF), l, m + jnp.log2(l)

        def values(rows, rows_p, p_d, p_p, l, lse):
            vt_d = jnp.transpose(vs_ref[rows, :]).astype(_BF)
            vt_p = jnp.transpose(vs_ref[rows_p, :]).astype(_BF)
            acc = _dot(vt_d, p_d) + _dot(vt_p, p_p)
            return rows, jnp.transpose(acc / l), jnp.transpose(jnp.broadcast_to(lse, (HEAD_DIM, tq)))

        def tile_body(it, carry, g=g, scores=scores):
            staged = [scores(it * DIL_UNROLL + j) for j in range(DIL_UNROLL)]
            staged = [softmax(*x) for x in staged]
            for rows, o, lse in [values(*x) for x in staged]:
                if g == 0:
                    o_ref[rows, :] = o
                    lse_ref[rows, :] = lse
                else:
                    lse_old = lse_ref[rows, :]
                    top = jnp.maximum(lse_old, lse)
                    e_old = jnp.exp2(lse_old - top)
                    e_new = jnp.exp2(lse - top)
                    o_ref[rows, :] = (e_old * o_ref[rows, :] + e_new * o) / (e_old + e_new)
                    lse_ref[rows, :] = top + jnp.log2(e_old + e_new)
            return carry

        assert (d * ntile) % DIL_UNROLL == 0
        lax.fori_loop(0, d * ntile // DIL_UNROLL, tile_body, 0)
    out_ref[...] = o_ref[...].astype(out_ref.dtype)


def _dilated_attention(zd3, dtab, bsz, seq):
    width = DIL_HPG * HEAD_DIM
    col = lambda part: pl.BlockSpec((None, seq, HEAD_DIM), lambda b, h, part=part: (b, 0, part * DIL_HPG + h))
    return pl.pallas_call(
        functools.partial(_dilated_kernel, seq=seq),
        grid=(bsz, DIL_HPG),
        in_specs=[col(part) for part in range(3 * DIL_GROUPS)]
        + [pl.BlockSpec((DIL_GROUPS, None, DIL_PW + DIL_TQ, DIL_TQ), lambda b, h: (0, h, 0, 0))],
        out_specs=pl.BlockSpec((None, seq, HEAD_DIM), lambda b, h: (b, 0, h)),
        out_shape=jax.ShapeDtypeStruct((bsz, seq, width), _BF),
        scratch_shapes=[pltpu.VMEM((seq, HEAD_DIM), _F32) for _ in range(5)],
        compiler_params=_cparams(("parallel", "arbitrary")),
        name="dilated_attention",
    )(*([zd3] * (3 * DIL_GROUPS)), dtab)


def _mla_kernel(qi_tab, ki_tab, q_ref, kn_ref, kp_ref, vt_ref, o_ref, *scratch, ntiles):
    tq = MLA_TQ
    c = (MLA_NOPE + MLA_ROPE) ** -0.5 * LOG2E
    per_head = len(scratch) // MLA_HPS
    heads = []
    for hd in range(MLA_HPS):
        sa_ref, sb_ref, mxa_ref, mxb_ref, m_ref, l_ref, acc_ref = scratch[hd * per_head:(hd + 1) * per_head]
        heads.append(dict(bufs=((sa_ref, mxa_ref), (sb_ref, mxb_ref)), state=(m_ref, l_ref, acc_ref),
                          qcols=slice(hd * 2 * LANES, (hd + 1) * 2 * LANES),
                          kcols=slice(hd * MLA_NOPE, (hd + 1) * MLA_NOPE), vrows=slice(hd * MLA_V, (hd + 1) * MLA_V)))
        _flash_reset(m_ref, l_ref, acc_ref)
    jj = lax.broadcasted_iota(jnp.int32, (tq, tq), 0)
    ii = lax.broadcasted_iota(jnp.int32, (tq, tq), 1)

    def rows_of(i):
        return pl.ds(pl.multiple_of(i * tq, tq), tq)

    def scores_into(slot, t):
        rows = rows_of(ki_tab[t])
        q_rows = rows_of(qi_tab[t])
        for hd in heads:
            k = jnp.concatenate([kn_ref[rows, hd["kcols"]], kp_ref[rows, :]], axis=1)
            _store_scores(hd["bufs"][slot], _dot_nt(k, q_ref[q_rows, hd["qcols"]]))

    def consume(slot, t, diagonal):
        rows = rows_of(ki_tab[t])
        for hd in heads:
            s_ref, mx_ref = hd["bufs"][slot]
            m_ref, l_ref, acc_ref = hd["state"]
            s = s_ref[...]
            tile_max = mx_ref[...]
            if diagonal:
                s = jnp.where(jj <= ii, s, NEG)
                tile_max = None
            _flash_update(s, vt_ref[hd["vrows"], rows], c, m_ref, l_ref, acc_ref, tile_max)
            if diagonal:
                o_ref[rows_of(qi_tab[t]), hd["vrows"]] = jnp.transpose(acc_ref[...] / l_ref[...]).astype(o_ref.dtype)
                _flash_reset(m_ref, l_ref, acc_ref)

    scores_into(0, 0)

    def pair(p, carry):
        t0 = 2 * p
        d0 = ki_tab[t0] == qi_tab[t0]
        d1 = ki_tab[t0 + 1] == qi_tab[t0 + 1]
        for x, y in ((False, False), (True, False), (False, True)):
            @pl.when(jnp.logical_and(d0 == x, d1 == y))
            def _(x=x, y=y):
                scores_into(1, t0 + 1)
                consume(0, t0, x)
                scores_into(0, t0 + 2)
                consume(1, t0 + 1, y)
        return carry

    lax.fori_loop(0, ntiles // 2, pair, 0)


def _mla_attention(q, kn, kpe, vt, bsz, seq):
    tq = MLA_TQ
    pairs = [(qi, ki) for qi in range(seq // tq) for ki in range(qi + 1)]
    ntiles = len(pairs)
    diag = [qi == ki for qi, ki in pairs]
    assert ntiles % 2 == 0 and not any(diag[t] and diag[t + 1] for t in range(0, ntiles, 2))
    pairs.append((0, 0))
    qi_tab = jnp.asarray([p[0] for p in pairs], jnp.int32)
    ki_tab = jnp.asarray([p[1] for p in pairs], jnp.int32)
    smem = pl.BlockSpec(memory_space=pltpu.SMEM)
    return pl.pallas_call(
        functools.partial(_mla_kernel, ntiles=ntiles),
        grid=(bsz, MLA_HEADS // MLA_HPS),
        in_specs=[
            smem, smem,
            pl.BlockSpec((None, seq, MLA_HPS * 2 * LANES), lambda b, h: (b, 0, h)),
            pl.BlockSpec((None, seq, MLA_HPS * MLA_NOPE), lambda b, h: (b, 0, h)),
            pl.BlockSpec((None, seq, LANES), lambda b, h: (b, 0, 0)),
            pl.BlockSpec((MLA_HPS * MLA_V, seq), lambda b, h: (h, b)),
        ],
        out_specs=pl.BlockSpec((None, seq, MLA_HPS * MLA_V), lambda b, h: (b, 0, h)),
        out_shape=jax.ShapeDtypeStruct((bsz, seq, MLA_HEADS * MLA_V), _BF),
        scratch_shapes=MLA_HPS * ([pltpu.VMEM((tq, tq), _F32) for _ in range(2)]
                                  + [pltpu.VMEM((1, tq), _F32) for _ in range(4)]
                                  + [pltpu.VMEM((MLA_V, tq), _F32)]),
        compiler_params=_cparams(("parallel", "arbitrary")),
        name="mla_attention",
    )(qi_tab, ki_tab, q, kn, kpe, vt)


def _residual_norm_store(x_new, nw_ref, x_out_ref, h_out_ref):
    x_out_ref[...] = x_new
    y = x_new * lax.rsqrt(jnp.mean(x_new * x_new, axis=-1, keepdims=True) + EPS)
    h_out_ref[...] = (y * nw_ref[...]).astype(h_out_ref.dtype)


def _out_even_kernel(x_ref, oa_ref, ga_ref, ob_ref, gb_ref, wa_ref, wb_ref, nw_ref, x_out_ref, h_out_ref):
    mixed_a = (oa_ref[...] * _silu(ga_ref[...])).astype(_BF)
    mixed_b = (ob_ref[...] * _silu(gb_ref[...])).astype(_BF)
    x_new = x_ref[...] + _dot(mixed_a, wa_ref[...]) + _dot(mixed_b, wb_ref[...])
    _residual_norm_store(x_new, nw_ref, x_out_ref, h_out_ref)


def _out_even(x2d, o_a, o_b, zf, w_out, nw_next, zf_cols, h_dtype):
    m, d = x2d.shape
    tm = OUT_TM
    wa = A_GATE
    wbd = B_GATE
    row = lambda width, cb=0: pl.BlockSpec((tm, width), lambda i, cb=cb: (i, cb))
    full = lambda r, c: pl.BlockSpec((r, c), lambda i: (0, 0), pipeline_mode=pl.Buffered(1))
    return pl.pallas_call(
        _out_even_kernel,
        grid=(m // tm,),
        in_specs=[row(d), row(wa), row(wa, zf_cols["gate_a"] // wa), row(wbd), row(wbd, zf_cols["gate_b"] // wbd),
                  full(wa, d), full(wbd, d), full(1, d)],
        out_specs=[row(d), row(d)],
        out_shape=[jax.ShapeDtypeStruct((m, d), _F32), jax.ShapeDtypeStruct((m, d), h_dtype)],
        compiler_params=_cparams(("parallel",)),
        name="out_proj_even",
    )(x2d, o_a, zf, o_b, zf, w_out[:wa], w_out[wa:], nw_next.reshape(1, d).astype(_F32))


def _out_odd_kernel(x_ref, oc_ref, gc_ref, w_ref, nw_ref, *out_refs):
    mixed = (oc_ref[...] * _silu(gc_ref[...])).astype(_BF)
    x_new = x_ref[...] + _dot(mixed, w_ref[...])
    if len(out_refs) == 2:
        _residual_norm_store(x_new, nw_ref, *out_refs)
    else:
        y = x_new * lax.rsqrt(jnp.mean(x_new * x_new, axis=-1, keepdims=True) + EPS)
        out_refs[0][...] = (y * nw_ref[...]).astype(out_refs[0].dtype)


def _out_odd(x2d, o_c, z, w_out, nw_next, gate_col_block, h_dtype, keep_x):
    m, d = x2d.shape
    tm = OUT_TM
    width = MLA_HEADS * MLA_V
    row = lambda w, cb=0: pl.BlockSpec((tm, w), lambda i, cb=cb: (i, cb))
    full = lambda r, c: pl.BlockSpec((r, c), lambda i: (0, 0), pipeline_mode=pl.Buffered(1))
    h_shape = jax.ShapeDtypeStruct((m, d), h_dtype)
    outs = pl.pallas_call(
        _out_odd_kernel,
        grid=(m // tm,),
        in_specs=[row(d), row(width), row(width, gate_col_block), full(width, d), full(1, d)],
        out_specs=[row(d), row(d)] if keep_x else [row(d)],
        out_shape=[jax.ShapeDtypeStruct((m, d), _F32), h_shape] if keep_x else [h_shape],
        compiler_params=_cparams(("parallel",)),
        name="out_proj_odd",
    )(x2d, o_c, z, w_out, nw_next.reshape(1, d).astype(_F32))
    return (outs[0], outs[1]) if keep_x else (None, outs[0])


def _bucket_of_distance(n):
    d = np.arange(n)
    max_exact = NUM_BUCKETS // 2
    df = np.maximum(d, 1).astype(np.float32)
    large = max_exact + (np.log(df / max_exact) / math.log(MAX_DISTANCE / max_exact)
                         * (NUM_BUCKETS - max_exact)).astype(np.int32)
    large = np.minimum(large, NUM_BUCKETS - 1)
    return np.where(d < max_exact, d, large).astype(np.int32)


def _toeplitz_tiles(v, n, nd):
    hh = v.shape[0]
    vp = jnp.concatenate([jnp.zeros((hh, n), v.dtype), v, jnp.zeros((hh, n), v.dtype)], axis=1)
    idx = (n - np.arange(2 * n)) % (2 * n)
    tiles = []
    for dl in range(nd):
        w = vp[:, dl * n:dl * n + 2 * n]
        c = w[:, idx]
        flat = jnp.tile(c, (1, n))[:, :n * (2 * n - 1)]
        tiles.append(flat.reshape(hh, n, 2 * n - 1)[:, :, :n])
    return jnp.stack(tiles, axis=0)


def _bias_tables(rel_bias, seq):
    bd = rel_bias.astype(_F32)[_bucket_of_distance(seq)].T
    bd_nsa = bd[:NSA_HEADS]
    nd = seq // NSA_TQ
    bt = _toeplitz_tiles(bd_nsa * LOG2E, NSA_TQ, nd)
    bt = bt.reshape(nd, NSA_GROUPS, NSA_HPG, NSA_TQ, NSA_TQ).transpose(1, 0, 4, 2, 3)
    bt = bt.reshape(NSA_GROUPS, nd, NSA_TQ, NSA_HPG * NSA_TQ)
    nblk = seq // NSA_BLOCK
    off = NSA_BLOCK * (nblk - 1) + NSA_BLOCK - 1
    bdp = jnp.concatenate([jnp.zeros((NSA_HEADS, off), _F32), bd_nsa], axis=1)
    bc = jnp.stack([bdp[:, off - (NSA_BLOCK * n + NSA_BLOCK - 1):off - (NSA_BLOCK * n + NSA_BLOCK - 1) + seq]
                    for n in range(nblk)], axis=1)
    bc = bc.reshape(NSA_GROUPS, NSA_HPG, nblk, seq)
    rows, cols = DIL_TQ, DIL_PW + DIL_TQ
    period = rows + cols
    dtabs = []
    for g, (w, d) in enumerate(DIL_PAIRS):
        heads = bd[NSA_HEADS + g * DIL_HPG:NSA_HEADS + (g + 1) * DIL_HPG] * LOG2E
        m = w // d + 1
        assert m == DIL_PW + 1
        by_offset = heads[:, 0:m * d:d]
        cyc = jnp.concatenate([by_offset[:, ::-1], jnp.full((DIL_HPG, period - m), NEG, _F32)], axis=1)
        flat = jnp.tile(cyc, (1, rows))[:, :rows * (period - 1)]
        by_query = flat.reshape(DIL_HPG, rows, period - 1)[:, :, :cols]
        dtabs.append(by_query.transpose(0, 2, 1))
    return bt, bc, jnp.stack(dtabs, axis=0)


def _rope_tables(seq):
    half = MLA_ROPE // 2
    freqs = 1.0 / (ROPE_THETA ** (jnp.arange(half, dtype=_F32) / half))
    ang = jnp.arange(seq).astype(_F32)[:, None] * freqs[None, :]
    zeros = jnp.zeros((seq, LANES - MLA_ROPE), _F32)
    cos = jnp.concatenate([jnp.cos(ang), jnp.cos(ang), zeros], axis=1)
    sin = jnp.concatenate([jnp.sin(ang), jnp.sin(ang), zeros], axis=1)
    return cos, sin


def _rot_cols(w):
    half = w.shape[-1] // 2
    return jnp.concatenate([-w[..., half:], w[..., :half]], axis=-1)


ZF_COLS = {"gate_a": 0, "gate_b": A_GATE, "kcvc": A_GATE + B_GATE, "g3": A_GATE + B_GATE + 4 * HEAD_DIM}


def _even_weights(w_in):
    w_in = w_in.astype(_BF)
    o = 0
    q_a = w_in[:, o:o + A_Q]; o += A_Q
    kv_a = w_in[:, o:o + A_KV]; o += A_KV
    g3 = w_in[:, o:o + A_G3]; o += A_G3
    gate_a = w_in[:, o:o + A_GATE]; o += A_GATE
    qkv_b = w_in[:, o:o + B_QKV]; o += B_QKV
    gate_b = w_in[:, o:o + B_GATE]
    gw = NSA_GROUPS * HEAD_DIM
    kc_vc, ks, vs, kw, vw = (kv_a[:, :2 * gw], kv_a[:, 2 * gw:3 * gw], kv_a[:, 3 * gw:4 * gw],
                             kv_a[:, 4 * gw:5 * gw], kv_a[:, 5 * gw:6 * gw])
    wb = jnp.concatenate([q_a, ks, kw], axis=1).astype(_BF)
    wvt = jnp.concatenate([vs, vw], axis=1).T.astype(_BF)
    dw = DIL_HPG * HEAD_DIM
    nq = DIL_HEADS * HEAD_DIM
    wd = jnp.concatenate([qkv_b[:, part * nq + g * dw:part * nq + (g + 1) * dw]
                          for g in range(DIL_GROUPS) for part in range(3)], axis=1).astype(_BF)
    g3_blocks = []
    for g in range(NSA_GROUPS):
        cols = [(g * NSA_HPG + h) * 3 + j for j in range(3) for h in range(NSA_HPG)]
        blk = g3[:, np.asarray(cols)]
        g3_blocks.append(jnp.pad(blk, ((0, 0), (0, LANES - len(cols)))))
    wf = jnp.concatenate([gate_a, gate_b, kc_vc] + g3_blocks, axis=1).astype(_BF)
    return wb, wvt, wd, wf


def _even_layer(x2d, h, w_in, cmp_pos, cmp_w1, cmp_w2, w_out, nw_next, tables, bsz, seq, h_dtype):
    bt, bc, dtab, onehot = tables
    wb, wvt, wd, wf = _even_weights(w_in)
    zb = _matmul(h, wb, _BF)
    vt = _matmul_nt(wvt, h, _BF)
    zd = _matmul(h, wd, _BF)
    zf = _matmul(h, wf, _F32)
    nblk = seq // NSA_BLOCK
    kcvc = zf[:, ZF_COLS["kcvc"]:ZF_COLS["kcvc"] + 4 * HEAD_DIM]
    kcvc_t = kcvc.reshape(bsz * nblk, NSA_BLOCK, 4, HEAD_DIM).transpose(2, 1, 0, 3)
    kvc = _compress(kcvc_t, cmp_pos.astype(_F32), cmp_w1.astype(_BF), cmp_w2.astype(_BF))
    kvc = kvc.reshape(4, bsz, nblk, HEAD_DIM)
    zf3 = zf.reshape(bsz, seq, zf.shape[-1])
    o_a = _nsa_attention(zb.reshape(bsz, seq, -1), vt, zf3, kvc, onehot, bt, bc, bsz, seq)
    o_b = _dilated_attention(zd.reshape(bsz, seq, -1), dtab, bsz, seq)
    return _out_even(x2d, o_a.reshape(bsz * seq, -1), o_b.reshape(bsz * seq, -1), zf, w_out.astype(_BF), nw_next,
                     ZF_COLS, h_dtype)


def _odd_layer(x2d, h, w_in, q_norm, w_qb, kv_norm, w_kvb, w_out, nw_next, rope, bsz, seq, h_dtype, keep_x):
    cos, sin = rope
    w_in, w_qb, w_kvb = w_in.astype(_BF), w_qb.astype(_BF), w_kvb.astype(_BF)
    o = 0
    w_cq = w_in[:, o:o + MLA_Q_RANK]; o += MLA_Q_RANK
    w_ckv = w_in[:, o:o + MLA_KV_RANK]; o += MLA_KV_RANK
    w_kpe = w_in[:, o:o + MLA_ROPE]; o += MLA_ROPE
    w_gate = w_in[:, o:]
    w1 = jnp.concatenate([w_cq, w_ckv, w_gate], axis=1).astype(_BF)
    z = _matmul(h, w1, _F32)
    zk = _matmul(h, jnp.concatenate([w_kpe, _rot_cols(w_kpe)], axis=1).astype(_BF), _F32)
    gate_width = MLA_HEADS * MLA_V
    q_latent = (q_norm, MLA_Q_RANK, 0)
    kv_latent = (kv_norm, MLA_KV_RANK, MLA_Q_RANK // MLA_KV_RANK)
    kpe = _rope_cols(zk, 0, cos, sin, seq)
    wq = w_qb.reshape(MLA_Q_RANK, MLA_HEADS, MLA_NOPE + MLA_ROPE)
    wq_pe = wq[:, :, MLA_NOPE:]
    wq = jnp.concatenate([wq[:, :, :MLA_NOPE], wq_pe, _rot_cols(wq_pe)], axis=-1)
    q = _matmul_rope(z, q_latent, wq.reshape(MLA_Q_RANK, MLA_HEADS * 2 * LANES).astype(_BF), cos, sin, seq)
    wkv = w_kvb.reshape(MLA_KV_RANK, MLA_HEADS, MLA_NOPE + MLA_V)
    wk = wkv[:, :, :MLA_NOPE].reshape(MLA_KV_RANK, -1).astype(_BF)
    wv_t = wkv[:, :, MLA_NOPE:].reshape(MLA_KV_RANK, -1).T.astype(_BF)
    kn = _matmul(z, wk, _BF, norm=kv_latent)
    vt = _matmul_nt(wv_t, z, _BF, norm=kv_latent)
    o_c = _mla_attention(q.reshape(bsz, seq, -1), kn.reshape(bsz, seq, -1), kpe.reshape(bsz, seq, LANES),
                         vt, bsz, seq)
    gate_block = (MLA_Q_RANK + MLA_KV_RANK) // gate_width
    return _out_odd(x2d, o_c.reshape(bsz * seq, -1), z, w_out.astype(_BF), nw_next, gate_block, h_dtype, keep_x)


def kernel(x, rel_bias, norm_w, final_norm_w, ev_w_in, nsa_cmp_pos, nsa_cmp_w1, nsa_cmp_w2, ev_w_out,
           od_w_in, mla_q_norm, mla_w_qb, mla_kv_norm, mla_w_kvb, od_w_out):
    bsz, seq, d = x.shape
    depth = norm_w.shape[0]
    assert seq % MLA_TQ == 0 and seq % (DIL_TQ * DIL_PAIRS[-1][1]) == 0 and seq // NSA_BLOCK <= LANES
    assert NSA_WINDOW == 2 * NSA_TQ
    bt, bc, dtab = _bias_tables(rel_bias, seq)
    blk_id = np.arange(seq)[:, None] // NSA_BLOCK
    onehot = jnp.asarray(blk_id == np.arange(LANES)[None, :], dtype=_BF)
    tables = (bt, bc, dtab, onehot)
    rope = _rope_tables(seq)
    x2d = x.reshape(bsz * seq, d).astype(_F32)
    h = _rmsnorm(x2d, norm_w[0], d, 0, _BF)
    for l in range(depth):
        last = l == depth - 1
        nw_next = final_norm_w if last else norm_w[l + 1]
        h_dtype = _F32 if last else _BF
        i = l // 2
        if l % 2 == 0:
            x2d, h = _even_layer(x2d, h, ev_w_in[i], nsa_cmp_pos[i], nsa_cmp_w1[i], nsa_cmp_w2[i],
                                 ev_w_out[i], nw_next, tables, bsz, seq, h_dtype)
        else:
            x2d, h = _odd_layer(x2d, h, od_w_in[i], mla_q_norm[i], mla_w_qb[i], mla_kv_norm[i],
                                mla_w_kvb[i], od_w_out[i], nw_next, rope, bsz, seq, h_dtype, keep_x=not last)
    return h.reshape(bsz, seq, d)
```

```python
import functools
import math

import numpy as np
import jax
import jax.numpy as jnp
from jax import lax
from jax.experimental import pallas as pl
from jax.experimental.pallas import tpu as pltpu

HEAD_DIM = 128
EPS = 1e-6
NEG = -1e30
NUM_BUCKETS = 32
MAX_DISTANCE = 2048
NSA_HEADS = 8
NSA_GROUPS = 2
NSA_HPG = 4
NSA_BLOCK = 64
NSA_TOPK = 16
NSA_WINDOW = 512
DIL_PAIRS = ((128, 1), (512, 4), (2048, 16))
DIL_GROUPS = 3
DIL_HPG = 4
DIL_HEADS = 12
MLA_HEADS = 16
MLA_Q_RANK = 1536
MLA_KV_RANK = 512
MLA_NOPE = 128
MLA_ROPE = 64
MLA_V = 128
ROPE_THETA = 10000.0

A_Q = NSA_HEADS * HEAD_DIM
A_KV = 6 * NSA_GROUPS * HEAD_DIM
A_G3 = 3 * NSA_HEADS
A_GATE = NSA_HEADS * HEAD_DIM
B_QKV = 3 * DIL_HEADS * HEAD_DIM
B_GATE = DIL_HPG * HEAD_DIM

LANES = 128
SUBLANES = 8
VMEM_LIMIT = 48 * 1024 * 1024

NSA_TQ = 256
NSA_CHAINS = 1
DIL_TQ = 256
DIL_PW = 128
DIL_UNROLL = 16
MLA_TQ = 1024
OUT_TM = 512
MLA_HPS = 2
LOG2E = math.log2(math.e)

_BF = jnp.bfloat16
_F32 = jnp.float32


def _cparams(sem):
    return pltpu.CompilerParams(dimension_semantics=sem, vmem_limit_bytes=VMEM_LIMIT)


def _dot(a, b):
    return jnp.dot(a, b, preferred_element_type=_F32)


def _dot_nt(a, b):
    return lax.dot_general(a, b, (((1,), (1,)), ((), ())), preferred_element_type=_F32)


def _silu(x):
    return x * (1.0 / (1.0 + jnp.exp(-x)))


def _sigmoid(x):
    return 1.0 / (1.0 + jnp.exp(-x))


def _pick_tile(n, candidates):
    for c in candidates:
        if n % c == 0:
            return c
    return n


def _rmsnorm_kernel(x_ref, w_ref, o_ref):
    x = x_ref[...]
    y = x * lax.rsqrt(jnp.mean(x * x, axis=-1, keepdims=True) + EPS)
    o_ref[...] = (y * w_ref[...]).astype(o_ref.dtype)


def _rmsnorm(x2d, w, width, col_block, out_dtype):
    m = x2d.shape[0]
    tm = _pick_tile(m, (512, 256, 128))
    return pl.pallas_call(
        _rmsnorm_kernel,
        grid=(m // tm,),
        in_specs=[pl.BlockSpec((tm, width), lambda i: (i, col_block)),
                  pl.BlockSpec((1, width), lambda i: (0, 0))],
        out_specs=pl.BlockSpec((tm, width), lambda i: (i, 0)),
        out_shape=jax.ShapeDtypeStruct((m, width), out_dtype),
        compiler_params=_cparams(("parallel",)),
        name="rmsnorm",
    )(x2d, w.reshape(1, width).astype(_F32))


def _matmul_kernel(a_ref, w_ref, o_ref):
    o_ref[...] = _dot(a_ref[...], w_ref[...]).astype(o_ref.dtype)


def _normed_lhs(a_ref, nw_ref, an_ref):
    @pl.when(pl.program_id(1) == 0)
    def _():
        x = a_ref[...]
        y = x * lax.rsqrt(jnp.mean(x * x, axis=-1, keepdims=True) + EPS)
        an_ref[...] = (y * nw_ref[...]).astype(an_ref.dtype)
    return an_ref[...]


def _norm_matmul_kernel(a_ref, nw_ref, w_ref, o_ref, an_ref):
    o_ref[...] = _dot(_normed_lhs(a_ref, nw_ref, an_ref), w_ref[...]).astype(o_ref.dtype)


def _lhs_specs(a, norm, tm):
    if norm is None:
        k = a.shape[1]
        return k, [pl.BlockSpec((tm, k), lambda i, j: (i, 0))], [], []
    nw, k, cb = norm
    specs = [pl.BlockSpec((tm, k), lambda i, j: (i, cb)), pl.BlockSpec((1, k), lambda i, j: (0, 0))]
    return k, specs, [nw.reshape(1, k).astype(_F32)], [pltpu.VMEM((tm, k), _BF)]


def _matmul(a, w, out_dtype, norm=None):
    m = a.shape[0]
    n = w.shape[1]
    tm = _pick_tile(m, (1024, 512, 256, 128))
    k, a_specs, a_args, scratch = _lhs_specs(a, norm, tm)
    wide = (2048,) if k * 2 <= 1024 else ()
    tn = _pick_tile(n, wide + (1536, 1024, 768, 512, 384, 256, 128))
    return pl.pallas_call(
        _matmul_kernel if norm is None else _norm_matmul_kernel,
        grid=(m // tm, n // tn),
        in_specs=a_specs + [pl.BlockSpec((k, tn), lambda i, j: (0, j))],
        out_specs=pl.BlockSpec((tm, tn), lambda i, j: (i, j)),
        out_shape=jax.ShapeDtypeStruct((m, n), out_dtype),
        scratch_shapes=scratch,
        compiler_params=_cparams(("parallel", "arbitrary")),
        name="matmul" if norm is None else "norm_matmul",
    )(a, *a_args, w)


def _matmul_nt_kernel(w_ref, a_ref, o_ref):
    o_ref[...] = _dot_nt(w_ref[...], a_ref[...]).astype(o_ref.dtype)


def _norm_matmul_nt_kernel(w_ref, a_ref, nw_ref, o_ref, an_ref):
    o_ref[...] = _dot_nt(w_ref[...], _normed_lhs(a_ref, nw_ref, an_ref)).astype(o_ref.dtype)


def _matmul_nt(w_t, a, out_dtype, norm=None):
    n = w_t.shape[0]
    m = a.shape[0]
    tm = _pick_tile(m, (1024, 512, 256, 128))
    tn = _pick_tile(n, (2048, 1024, 512, 256, 128))
    k, a_specs, a_args, scratch = _lhs_specs(a, norm, tm)
    return pl.pallas_call(
        _matmul_nt_kernel if norm is None else _norm_matmul_nt_kernel,
        grid=(m // tm, n // tn),
        in_specs=[pl.BlockSpec((tn, k), lambda i, j: (j, 0))] + a_specs,
        out_specs=pl.BlockSpec((tn, tm), lambda i, j: (j, i)),
        out_shape=jax.ShapeDtypeStruct((n, m), out_dtype),
        scratch_shapes=scratch,
        compiler_params=_cparams(("parallel", "arbitrary")),
        name="matmul_nt" if norm is None else "norm_matmul_nt",
    )(w_t, a, *a_args)


def _rope_chunk(chunk, cos, sin):
    return chunk * cos + pltpu.roll(chunk, 64, 1) * sin


def _matmul_rope_kernel(a_ref, nw_ref, w_ref, cos_ref, sin_ref, o_ref, an_ref, *, heads_per_tile):
    acc = _dot(_normed_lhs(a_ref, nw_ref, an_ref), w_ref[...])
    cos = cos_ref[...]
    sin = sin_ref[...]
    for h in range(heads_per_tile):
        base = h * 2 * LANES
        o_ref[:, base:base + LANES] = acc[:, base:base + LANES].astype(o_ref.dtype)
        o_ref[:, base + LANES:base + 2 * LANES] = _rope_chunk(
            acc[:, base + LANES:base + 2 * LANES], cos, sin).astype(o_ref.dtype)


def _matmul_rope(a, norm, w, cos, sin, seq):
    m = a.shape[0]
    n = w.shape[1]
    tm = _pick_tile(seq, (1024, 512, 256, 128))
    tn = 2048
    tpb = seq // tm
    k, a_specs, a_args, scratch = _lhs_specs(a, norm, tm)
    return pl.pallas_call(
        functools.partial(_matmul_rope_kernel, heads_per_tile=tn // (2 * LANES)),
        grid=(m // tm, n // tn),
        in_specs=a_specs + [pl.BlockSpec((k, tn), lambda i, j: (0, j)),
                            pl.BlockSpec((tm, LANES), lambda i, j: (i % tpb, 0)),
                            pl.BlockSpec((tm, LANES), lambda i, j: (i % tpb, 0))],
        out_specs=pl.BlockSpec((tm, tn), lambda i, j: (i, j)),
        out_shape=jax.ShapeDtypeStruct((m, n), _BF),
        scratch_shapes=scratch,
        compiler_params=_cparams(("parallel", "arbitrary")),
        name="matmul_rope",
    )(a, *a_args, w, cos, sin)


def _rope_cols_kernel(x_ref, cos_ref, sin_ref, o_ref):
    o_ref[...] = _rope_chunk(x_ref[...], cos_ref[...], sin_ref[...]).astype(o_ref.dtype)


def _rope_cols(z2d, col_block, cos, sin, seq):
    m = z2d.shape[0]
    tm = _pick_tile(seq, (512, 256, 128))
    tpb = seq // tm
    return pl.pallas_call(
        _rope_cols_kernel,
        grid=(m // tm,),
        in_specs=[pl.BlockSpec((tm, LANES), lambda i: (i, col_block)),
                  pl.BlockSpec((tm, LANES), lambda i: (i % tpb, 0)),
                  pl.BlockSpec((tm, LANES), lambda i: (i % tpb, 0))],
        out_specs=pl.BlockSpec((tm, LANES), lambda i: (i, 0)),
        out_shape=jax.ShapeDtypeStruct((m, LANES), _BF),
        compiler_params=_cparams(("parallel",)),
        name="rope_kpe",
    )(z2d, cos, sin)


def _compress_kernel(blk_ref, pos_ref, w1_ref, w2_ref, o_ref, acc_ref, *, lt):
    li = pl.program_id(1)

    @pl.when(li == 0)
    def _():
        acc_ref[...] = jnp.zeros_like(acc_ref)

    acc = acc_ref[...]
    for l in range(lt):
        a = (blk_ref[l] + pos_ref[l:l + 1, :]).astype(_BF)
        acc = acc + _dot(a, w1_ref[l])
    acc_ref[...] = acc

    @pl.when(li == pl.num_programs(1) - 1)
    def _():
        hid = _silu(acc_ref[...]).astype(_BF)
        o_ref[...] = _dot(hid, w2_ref[...])


def _compress(kcvc_t, pos, w1, w2):
    _, L, R, _ = kcvc_t.shape
    lt = 16
    return pl.pallas_call(
        functools.partial(_compress_kernel, lt=lt),
        grid=(4, L // lt),
        in_specs=[pl.BlockSpec((None, lt, R, HEAD_DIM), lambda c, l: (c, l, 0, 0)),
                  pl.BlockSpec((None, lt, HEAD_DIM), lambda c, l: (c // 2, l, 0)),
                  pl.BlockSpec((None, lt, HEAD_DIM, HEAD_DIM), lambda c, l: (c // 2, l, 0, 0)),
                  pl.BlockSpec((None, HEAD_DIM, HEAD_DIM), lambda c, l: (c // 2, 0, 0))],
        out_specs=pl.BlockSpec((None, R, HEAD_DIM), lambda c, l: (c, 0, 0)),
        out_shape=jax.ShapeDtypeStruct((4, R, HEAD_DIM), _F32),
        scratch_shapes=[pltpu.VMEM((R, HEAD_DIM), _F32)],
        compiler_params=_cparams(("parallel", "arbitrary")),
        name="nsa_compress",
    )(kcvc_t, pos, w1.reshape(2, L, HEAD_DIM, HEAD_DIM), w2)


def _flash_reset(m_ref, l_ref, acc_ref):
    m_ref[...] = jnp.full(m_ref.shape, -jnp.inf, _F32)
    l_ref[...] = jnp.zeros(l_ref.shape, _F32)
    acc_ref[...] = jnp.zeros(acc_ref.shape, _F32)


def _flash_update(s, vt, c, m_ref, l_ref, acc_ref, tile_max=None):
    m_prev = m_ref[...]
    if tile_max is None:
        tile_max = jnp.max(s, axis=0, keepdims=True)
    m_new = jnp.maximum(m_prev, tile_max)
    if c is None:
        alpha = jnp.exp2(m_prev - m_new)
        p = jnp.exp2(s - m_new)
    else:
        alpha = jnp.exp2((m_prev - m_new) * c)
        p = jnp.exp2((s - m_new) * c)
    l_ref[...] = alpha * l_ref[...] + jnp.sum(p, axis=0, keepdims=True)
    acc_ref[...] = alpha * acc_ref[...] + _dot(vt, p.astype(_BF))
    m_ref[...] = m_new


def _store_scores(buf, s):
    s_ref, mx_ref = buf
    s_ref[...] = s
    mx_ref[...] = jnp.max(s, axis=0, keepdims=True)


def _causal_flash(n_off, scores_into, consume, buf_a, buf_b, primed=False):
    if not primed:
        scores_into(buf_a, 0)

    def pair(p, carry):
        k0 = 2 * p
        scores_into(buf_b, k0 + 1)
        consume(buf_a, k0, False)
        scores_into(buf_a, k0 + 2)
        consume(buf_b, k0 + 1, False)
        return carry

    lax.fori_loop(0, n_off // 2, pair, 0)

    @pl.when(n_off % 2 == 1)
    def _():
        scores_into(buf_b, n_off)
        consume(buf_a, n_off - 1, False)
        consume(buf_b, n_off, True)

    @pl.when(n_off % 2 == 0)
    def _():
        consume(buf_a, n_off, True)


def _nsa_kernel(q_ref, ks_ref, kw_ref, vst_ref, vwt_ref, kc_ref, vc_ref, g3_ref, oh_ref, bt_ref, bc_ref,
                o_ref, *scratch, seq):
    tq = NSA_TQ
    hp = NSA_HPG
    n = hp * tq
    nc = n // NSA_CHAINS
    nblk = seq // NSA_BLOCK
    qi = pl.program_id(2)
    scale = HEAD_DIM ** -0.5
    c = scale * LOG2E
    t0 = qi * tq

    per_chain = len(scratch) // NSA_CHAINS
    chains = []
    for ch in range(NSA_CHAINS):
        (sa_ref, sb_ref, w0_ref, w1_ref, w2_ref, mxa_ref, mxb_ref, m_ref, l_ref, mw_ref, lw_ref,
         acc_s_ref, acc_w_ref) = scratch[ch * per_chain:(ch + 1) * per_chain]
        chains.append(dict(buf_a=(sa_ref, mxa_ref), buf_b=(sb_ref, mxb_ref), win=(w0_ref, w1_ref, w2_ref),
                           slc_state=(m_ref, l_ref, acc_s_ref), win_state=(mw_ref, lw_ref, acc_w_ref),
                           lanes=slice(ch * nc, (ch + 1) * nc)))

    qs = jnp.concatenate([q_ref[:, h * HEAD_DIM:(h + 1) * HEAD_DIM] for h in range(hp)], axis=0)

    def rows_of(kt):
        return pl.ds(pl.multiple_of(kt * tq, tq), tq)

    for delta in range(3):
        kw = kw_ref[rows_of(jnp.maximum(qi - delta, 0)), :]
        for ch in chains:
            ch["win"][delta][...] = _dot_nt(kw, qs[ch["lanes"], :]) * c + bt_ref[delta, :, ch["lanes"]]

    kc = kc_ref[...].astype(_BF)
    s_c = _dot_nt(kc, qs) * scale
    s_c = s_c + jnp.concatenate([bc_ref[h] for h in range(hp)], axis=1)
    n_col = lax.broadcasted_iota(jnp.int32, (nblk, tq), 0)
    t_row = t0 + lax.broadcasted_iota(jnp.int32, (nblk, tq), 1)
    n_all = lax.broadcasted_iota(jnp.int32, (nblk, n), 0)
    t_all = t0 + (lax.broadcasted_iota(jnp.int32, (nblk, n), 1) & (tq - 1))
    valid = t_all >= n_all * NSA_BLOCK + (NSA_BLOCK - 1)
    s_c = jnp.where(valid, s_c, NEG)
    m_c = jnp.max(s_c, axis=0, keepdims=True)
    p_c = jnp.where(valid, jnp.exp(s_c - m_c), 0.0)
    l_c = jnp.sum(p_c, axis=0, keepdims=True)
    p_c = p_c / jnp.where(l_c > 0.0, l_c, 1.0)
    imp = p_c[:, 0:tq]
    for h in range(1, hp):
        imp = imp + p_c[:, h * tq:(h + 1) * tq]

    pad_rows = LANES - nblk
    vc_pad = jnp.concatenate([vc_ref[...], jnp.zeros((pad_rows, HEAD_DIM), _F32)], axis=0)
    vc_t = jnp.transpose(vc_pad).astype(_BF)
    p_pad = jnp.concatenate([p_c, jnp.zeros((pad_rows, n), _F32)], axis=0).astype(_BF)
    o_cmp = _dot(vc_t, p_pad)

    cur = jnp.right_shift(t_row, int(math.log2(NSA_BLOCK)))
    forced = (n_col == 0) | (n_col == cur) | (n_col == cur - 1)
    impv = jnp.where(forced, jnp.inf, jnp.where(n_col > cur, -jnp.inf, imp))
    rank = jnp.zeros((nblk, tq), _F32)
    sub_row = lax.broadcasted_iota(jnp.int32, (SUBLANES, tq), 0)
    for i in range(nblk):
        row = impv[i:i + 1, :]
        lo = i // SUBLANES * SUBLANES
        parts = []
        if lo > 0:
            parts.append(jnp.where(row > impv[:lo], 1.0, 0.0))
        mid = impv[lo:lo + SUBLANES]
        parts.append(jnp.where(sub_row > i - lo,
                               jnp.where(row >= mid, 1.0, 0.0), jnp.where(row > mid, 1.0, 0.0)))
        if lo + SUBLANES < nblk:
            parts.append(jnp.where(row >= impv[lo + SUBLANES:], 1.0, 0.0))
        rank = rank + jnp.concatenate(parts, axis=0)
    sel = (rank < float(min(NSA_TOPK, nblk))) & (n_col <= cur)
    pen = jnp.where(sel, 0.0, NEG)
    pen = jnp.concatenate([pen, jnp.full((pad_rows, tq), NEG, _F32)], axis=0)
    pen_t = jnp.transpose(pen).astype(_BF)
    q_aug = jnp.concatenate([qs, jnp.concatenate([pen_t] * hp, axis=0)], axis=1)

    jj = lax.broadcasted_iota(jnp.int32, (tq, nc), 0)
    ii = lax.broadcasted_iota(jnp.int32, (tq, nc), 1) & (tq - 1)

    def slc_scores(side, ki):
        rows = rows_of(ki)
        k_aug = jnp.concatenate([ks_ref[rows, :], oh_ref[rows, :]], axis=1)
        for ch in chains:
            _store_scores(ch[side], _dot_nt(k_aug, q_aug[ch["lanes"], :]) * c + bt_ref[qi - ki, :, ch["lanes"]])

    def slc_consume(side, ki, diagonal):
        vt = vst_ref[:, rows_of(ki)]
        for ch in chains:
            s = ch[side][0][...]
            tile_max = ch[side][1][...]
            if diagonal:
                s = jnp.where(jj <= ii, s, NEG)
                tile_max = None
            _flash_update(s, vt, None, *ch["slc_state"], tile_max)

    slc_scores("buf_a", 0)

    def win_consume(delta, keep):
        vt = vwt_ref[:, rows_of(jnp.maximum(qi - delta, 0))]
        for ch in chains:
            _flash_update(jnp.where(keep, ch["win"][delta][...], NEG), vt, None, *ch["win_state"])

    for ch in chains:
        _flash_reset(*ch["win_state"])
        _flash_reset(*ch["slc_state"])
    win_consume(0, jj <= ii)
    win_consume(1, qi >= 1)
    win_consume(2, (jj > ii) & (qi >= 2))

    _causal_flash(qi, slc_scores, slc_consume, "buf_a", "buf_b", primed=True)

    g_t = jnp.transpose(_sigmoid(g3_ref[...]))
    for ch in chains:
        ch["o_win"] = ch["win_state"][2][...] / ch["win_state"][1][...]
        ch["o_slc"] = ch["slc_state"][2][...] / ch["slc_state"][1][...]
    for h in range(hp):
        ch = chains[h * tq // nc]
        cols = slice(h * tq, (h + 1) * tq)
        local = slice(h * tq % nc, h * tq % nc + tq)
        out_t = (g_t[0 * hp + h:0 * hp + h + 1, :] * o_cmp[:, cols]
                 + g_t[1 * hp + h:1 * hp + h + 1, :] * ch["o_slc"][:, local]
                 + g_t[2 * hp + h:2 * hp + h + 1, :] * ch["o_win"][:, local])
        o_ref[:, h * HEAD_DIM:(h + 1) * HEAD_DIM] = jnp.transpose(out_t).astype(o_ref.dtype)


def _nsa_attention(zb, vt, zf, kvc, onehot, bt, bc, bsz, seq):
    tq = NSA_TQ
    n = NSA_HPG * tq
    nblk = seq // NSA_BLOCK
    nd = seq // tq
    kcol = A_Q // HEAD_DIM
    g3col = ZF_COLS["g3"] // LANES
    return pl.pallas_call(
        functools.partial(_nsa_kernel, seq=seq),
        grid=(NSA_GROUPS, bsz, seq // tq),
        in_specs=[
            pl.BlockSpec((None, tq, NSA_HPG * HEAD_DIM), lambda g, b, i: (b, i, g)),
            pl.BlockSpec((None, seq, HEAD_DIM), lambda g, b, i: (b, 0, kcol + g)),
            pl.BlockSpec((None, seq, HEAD_DIM), lambda g, b, i: (b, 0, kcol + NSA_GROUPS + g)),
            pl.BlockSpec((HEAD_DIM, seq), lambda g, b, i: (g, b)),
            pl.BlockSpec((HEAD_DIM, seq), lambda g, b, i: (NSA_GROUPS + g, b)),
            pl.BlockSpec((None, None, nblk, HEAD_DIM), lambda g, b, i: (g, b, 0, 0)),
            pl.BlockSpec((None, None, nblk, HEAD_DIM), lambda g, b, i: (2 + g, b, 0, 0)),
            pl.BlockSpec((None, tq, LANES), lambda g, b, i: (b, i, g3col + g)),
            pl.BlockSpec((seq, LANES), lambda g, b, i: (0, 0)),
            pl.BlockSpec((None, nd, tq, n), lambda g, b, i: (g, 0, 0, 0), pipeline_mode=pl.Buffered(1)),
            pl.BlockSpec((None, NSA_HPG, nblk, tq), lambda g, b, i: (g, 0, 0, i)),
        ],
        out_specs=pl.BlockSpec((None, tq, NSA_HPG * HEAD_DIM), lambda g, b, i: (b, i, g)),
        out_shape=jax.ShapeDtypeStruct((bsz, seq, NSA_HEADS * HEAD_DIM), _BF),
        scratch_shapes=NSA_CHAINS * ([pltpu.VMEM((tq, n // NSA_CHAINS), _F32) for _ in range(5)]
                                     + [pltpu.VMEM((1, n // NSA_CHAINS), _F32) for _ in range(6)]
                                     + [pltpu.VMEM((HEAD_DIM, n // NSA_CHAINS), _F32) for _ in range(2)]),
        compiler_params=_cparams(("arbitrary", "arbitrary", "arbitrary")),
        name="nsa_attention",
    )(zb, zb, zb, vt, vt, kvc, kvc, zf, onehot, bt, bc)


def _dilated_kernel(*refs, seq):
    qkv_refs, (tab_ref, out_ref, qs_ref, ks_ref, vs_ref, lse_ref, o_ref) = refs[:9], refs[9:]
    tq, pw = DIL_TQ, DIL_PW
    c = HEAD_DIM ** -0.5 * LOG2E
    for g, (_, d) in enumerate(DIL_PAIRS):
        q_ref, k_ref, v_ref = qkv_refs[3 * g:3 * g + 3]
        qs_ref[...] = q_ref[...].astype(_F32)
        ks_ref[...] = k_ref[...].astype(_F32)
        vs_ref[...] = v_ref[...].astype(_F32)
        ntile = seq // d // tq
        tab_p = tab_ref[g, :pw, :]
        tab_d = tab_ref[g, pw:, :]

        def scores(idx, d=d, ntile=ntile, tab_d=tab_d, tab_p=tab_p):
            r = idx // ntile
            tile = idx % ntile
            u0 = tile * tq
            rows = pl.ds(u0 * d + r, tq, stride=d)
            rows_p = pl.ds(jnp.maximum(u0 - pw, 0) * d + r, pw, stride=d)
            q = qs_ref[rows, :].astype(_BF)
            s_d = _dot_nt(ks_ref[rows, :].astype(_BF), q) * c + tab_d
            s_p = jnp.where(tile > 0, _dot_nt(ks_ref[rows_p, :].astype(_BF), q) * c + tab_p, NEG)
            return rows, rows_p, s_d, s_p

        def softmax(rows, rows_p, s_d, s_p):
            m = jnp.maximum(jnp.max(s_d, axis=0, keepdims=True), jnp.max(s_p, axis=0, keepdims=True))
            p_d = jnp.exp2(s_d - m)
            p_p = jnp.exp2(s_p - m)
            l = jnp.sum(p_d, axis=0, keepdims=True) + jnp.sum(p_p, axis=0, keepdims=True)
            return rows, rows_p, p_d.astype(_BF), p_p.astype(_BF), l, m + jnp.log2(l)

        def values(rows, rows_p, p_d, p_p, l, lse):
            vt_d = jnp.transpose(vs_ref[rows, :]).astype(_BF)
            vt_p = jnp.transpose(vs_ref[rows_p, :]).astype(_BF)
            acc = _dot(vt_d, p_d) + _dot(vt_p, p_p)
            return rows, jnp.transpose(acc / l), jnp.transpose(jnp.broadcast_to(lse, (HEAD_DIM, tq)))

        def tile_body(it, carry, g=g, scores=scores):
            staged = [scores(it * DIL_UNROLL + j) for j in range(DIL_UNROLL)]
            staged = [softmax(*x) for x in staged]
            for rows, o, lse in [values(*x) for x in staged]:
                if g == 0:
                    o_ref[rows, :] = o
                    lse_ref[rows, :] = lse
                else:
                    lse_old = lse_ref[rows, :]
                    top = jnp.maximum(lse_old, lse)
                    e_old = jnp.exp2(lse_old - top)
                    e_new = jnp.exp2(lse - top)
                    o_ref[rows, :] = (e_old * o_ref[rows, :] + e_new * o) / (e_old + e_new)
                    lse_ref[rows, :] = top + jnp.log2(e_old + e_new)
            return carry

        assert (d * ntile) % DIL_UNROLL == 0
        lax.fori_loop(0, d * ntile // DIL_UNROLL, tile_body, 0)
    out_ref[...] = o_ref[...].astype(out_ref.dtype)


def _dilated_attention(zd3, dtab, bsz, seq):
    width = DIL_HPG * HEAD_DIM
    col = lambda part: pl.BlockSpec((None, seq, HEAD_DIM), lambda b, h, part=part: (b, 0, part * DIL_HPG + h))
    return pl.pallas_call(
        functools.partial(_dilated_kernel, seq=seq),
        grid=(bsz, DIL_HPG),
        in_specs=[col(part) for part in range(3 * DIL_GROUPS)]
        + [pl.BlockSpec((DIL_GROUPS, None, DIL_PW + DIL_TQ, DIL_TQ), lambda b, h: (0, h, 0, 0))],
        out_specs=pl.BlockSpec((None, seq, HEAD_DIM), lambda b, h: (b, 0, h)),
        out_shape=jax.ShapeDtypeStruct((bsz, seq, width), _BF),
        scratch_shapes=[pltpu.VMEM((seq, HEAD_DIM), _F32) for _ in range(5)],
        compiler_params=_cparams(("parallel", "arbitrary")),
        name="dilated_attention",
    )(*([zd3] * (3 * DIL_GROUPS)), dtab)


def _mla_kernel(qi_tab, ki_tab, q_ref, kn_ref, kp_ref, vt_ref, o_ref, *scratch, ntiles):
    tq = MLA_TQ
    c = (MLA_NOPE + MLA_ROPE) ** -0.5 * LOG2E
    per_head = len(scratch) // MLA_HPS
    heads = []
    for hd in range(MLA_HPS):
        sa_ref, sb_ref, mxa_ref, mxb_ref, m_ref, l_ref, acc_ref = scratch[hd * per_head:(hd + 1) * per_head]
        heads.append(dict(bufs=((sa_ref, mxa_ref), (sb_ref, mxb_ref)), state=(m_ref, l_ref, acc_ref),
                          qcols=slice(hd * 2 * LANES, (hd + 1) * 2 * LANES),
                          kcols=slice(hd * MLA_NOPE, (hd + 1) * MLA_NOPE), vrows=slice(hd * MLA_V, (hd + 1) * MLA_V)))
        _flash_reset(m_ref, l_ref, acc_ref)
    jj = lax.broadcasted_iota(jnp.int32, (tq, tq), 0)
    ii = lax.broadcasted_iota(jnp.int32, (tq, tq), 1)

    def rows_of(i):
        return pl.ds(pl.multiple_of(i * tq, tq), tq)

    def scores_into(slot, t):
        rows = rows_of(ki_tab[t])
        q_rows = rows_of(qi_tab[t])
        for hd in heads:
            k = jnp.concatenate([kn_ref[rows, hd["kcols"]], kp_ref[rows, :]], axis=1)
            _store_scores(hd["bufs"][slot], _dot_nt(k, q_ref[q_rows, hd["qcols"]]))

    def consume(slot, t, diagonal):
        rows = rows_of(ki_tab[t])
        for hd in heads:
            s_ref, mx_ref = hd["bufs"][slot]
            m_ref, l_ref, acc_ref = hd["state"]
            s = s_ref[...]
            tile_max = mx_ref[...]
            if diagonal:
                s = jnp.where(jj <= ii, s, NEG)
                tile_max = None
            _flash_update(s, vt_ref[hd["vrows"], rows], c, m_ref, l_ref, acc_ref, tile_max)
            if diagonal:
                o_ref[rows_of(qi_tab[t]), hd["vrows"]] = jnp.transpose(acc_ref[...] / l_ref[...]).astype(o_ref.dtype)
                _flash_reset(m_ref, l_ref, acc_ref)

    scores_into(0, 0)

    def pair(p, carry):
        t0 = 2 * p
        d0 = ki_tab[t0] == qi_tab[t0]
        d1 = ki_tab[t0 + 1] == qi_tab[t0 + 1]
        for x, y in ((False, False), (True, False), (False, True)):
            @pl.when(jnp.logical_and(d0 == x, d1 == y))
            def _(x=x, y=y):
                scores_into(1, t0 + 1)
                consume(0, t0, x)
                scores_into(0, t0 + 2)
                consume(1, t0 + 1, y)
        return carry

    lax.fori_loop(0, ntiles // 2, pair, 0)


def _mla_attention(q, kn, kpe, vt, bsz, seq):
    tq = MLA_TQ
    pairs = [(qi, ki) for qi in range(seq // tq) for ki in range(qi + 1)]
    ntiles = len(pairs)
    diag = [qi == ki for qi, ki in pairs]
    assert ntiles % 2 == 0 and not any(diag[t] and diag[t + 1] for t in range(0, ntiles, 2))
    pairs.append((0, 0))
    qi_tab = jnp.asarray([p[0] for p in pairs], jnp.int32)
    ki_tab = jnp.asarray([p[1] for p in pairs], jnp.int32)
    smem = pl.BlockSpec(memory_space=pltpu.SMEM)
    return pl.pallas_call(
        functools.partial(_mla_kernel, ntiles=ntiles),
        grid=(bsz, MLA_HEADS // MLA_HPS),
        in_specs=[
            smem, smem,
            pl.BlockSpec((None, seq, MLA_HPS * 2 * LANES), lambda b, h: (b, 0, h)),
            pl.BlockSpec((None, seq, MLA_HPS * MLA_NOPE), lambda b, h: (b, 0, h)),
            pl.BlockSpec((None, seq, LANES), lambda b, h: (b, 0, 0)),
            pl.BlockSpec((MLA_HPS * MLA_V, seq), lambda b, h: (h, b)),
        ],
        out_specs=pl.BlockSpec((None, seq, MLA_HPS * MLA_V), lambda b, h: (b, 0, h)),
        out_shape=jax.ShapeDtypeStruct((bsz, seq, MLA_HEADS * MLA_V), _BF),
        scratch_shapes=MLA_HPS * ([pltpu.VMEM((tq, tq), _F32) for _ in range(2)]
                                  + [pltpu.VMEM((1, tq), _F32) for _ in range(4)]
                                  + [pltpu.VMEM((MLA_V, tq), _F32)]),
        compiler_params=_cparams(("parallel", "arbitrary")),
        name="mla_attention",
    )(qi_tab, ki_tab, q, kn, kpe, vt)


def _residual_norm_store(x_new, nw_ref, x_out_ref, h_out_ref):
    x_out_ref[...] = x_new
    y = x_new * lax.rsqrt(jnp.mean(x_new * x_new, axis=-1, keepdims=True) + EPS)
    h_out_ref[...] = (y * nw_ref[...]).astype(h_out_ref.dtype)


def _out_even_kernel(x_ref, oa_ref, ga_ref, ob_ref, gb_ref, wa_ref, wb_ref, nw_ref, x_out_ref, h_out_ref):
    mixed_a = (oa_ref[...] * _silu(ga_ref[...])).astype(_BF)
    mixed_b = (ob_ref[...] * _silu(gb_ref[...])).astype(_BF)
    x_new = x_ref[...] + _dot(mixed_a, wa_ref[...]) + _dot(mixed_b, wb_ref[...])
    _residual_norm_store(x_new, nw_ref, x_out_ref, h_out_ref)


def _out_even(x2d, o_a, o_b, zf, w_out, nw_next, zf_cols, h_dtype):
    m, d = x2d.shape
    tm = OUT_TM
    wa = A_GATE
    wbd = B_GATE
    row = lambda width, cb=0: pl.BlockSpec((tm, width), lambda i, cb=cb: (i, cb))
    full = lambda r, c: pl.BlockSpec((r, c), lambda i: (0, 0), pipeline_mode=pl.Buffered(1))
    return pl.pallas_call(
        _out_even_kernel,
        grid=(m // tm,),
        in_specs=[row(d), row(wa), row(wa, zf_cols["gate_a"] // wa), row(wbd), row(wbd, zf_cols["gate_b"] // wbd),
                  full(wa, d), full(wbd, d), full(1, d)],
        out_specs=[row(d), row(d)],
        out_shape=[jax.ShapeDtypeStruct((m, d), _F32), jax.ShapeDtypeStruct((m, d), h_dtype)],
        compiler_params=_cparams(("parallel",)),
        name="out_proj_even",
    )(x2d, o_a, zf, o_b, zf, w_out[:wa], w_out[wa:], nw_next.reshape(1, d).astype(_F32))


def _out_odd_kernel(x_ref, oc_ref, gc_ref, w_ref, nw_ref, *out_refs):
    mixed = (oc_ref[...] * _silu(gc_ref[...])).astype(_BF)
    x_new = x_ref[...] + _dot(mixed, w_ref[...])
    if len(out_refs) == 2:
        _residual_norm_store(x_new, nw_ref, *out_refs)
    else:
        y = x_new * lax.rsqrt(jnp.mean(x_new * x_new, axis=-1, keepdims=True) + EPS)
        out_refs[0][...] = (y * nw_ref[...]).astype(out_refs[0].dtype)


def _out_odd(x2d, o_c, z, w_out, nw_next, gate_col_block, h_dtype, keep_x):
    m, d = x2d.shape
    tm = OUT_TM
    width = MLA_HEADS * MLA_V
    row = lambda w, cb=0: pl.BlockSpec((tm, w), lambda i, cb=cb: (i, cb))
    full = lambda r, c: pl.BlockSpec((r, c), lambda i: (0, 0), pipeline_mode=pl.Buffered(1))
    h_shape = jax.ShapeDtypeStruct((m, d), h_dtype)
    outs = pl.pallas_call(
        _out_odd_kernel,
        grid=(m // tm,),
        in_specs=[row(d), row(width), row(width, gate_col_block), full(width, d), full(1, d)],
        out_specs=[row(d), row(d)] if keep_x else [row(d)],
        out_shape=[jax.ShapeDtypeStruct((m, d), _F32), h_shape] if keep_x else [h_shape],
        compiler_params=_cparams(("parallel",)),
        name="out_proj_odd",
    )(x2d, o_c, z, w_out, nw_next.reshape(1, d).astype(_F32))
    return (outs[0], outs[1]) if keep_x else (None, outs[0])


def _bucket_of_distance(n):
    d = np.arange(n)
    max_exact = NUM_BUCKETS // 2
    df = np.maximum(d, 1).astype(np.float32)
    large = max_exact + (np.log(df / max_exact) / math.log(MAX_DISTANCE / max_exact)
                         * (NUM_BUCKETS - max_exact)).astype(np.int32)
    large = np.minimum(large, NUM_BUCKETS - 1)
    return np.where(d < max_exact, d, large).astype(np.int32)


def _toeplitz_tiles(v, n, nd):
    hh = v.shape[0]
    vp = jnp.concatenate([jnp.zeros((hh, n), v.dtype), v, jnp.zeros((hh, n), v.dtype)], axis=1)
    idx = (n - np.arange(2 * n)) % (2 * n)
    tiles = []
    for dl in range(nd):
        w = vp[:, dl * n:dl * n + 2 * n]
        c = w[:, idx]
        flat = jnp.tile(c, (1, n))[:, :n * (2 * n - 1)]
        tiles.append(flat.reshape(hh, n, 2 * n - 1)[:, :, :n])
    return jnp.stack(tiles, axis=0)


def _bias_tables(rel_bias, seq):
    bd = rel_bias.astype(_F32)[_bucket_of_distance(seq)].T
    bd_nsa = bd[:NSA_HEADS]
    nd = seq // NSA_TQ
    bt = _toeplitz_tiles(bd_nsa * LOG2E, NSA_TQ, nd)
    bt = bt.reshape(nd, NSA_GROUPS, NSA_HPG, NSA_TQ, NSA_TQ).transpose(1, 0, 4, 2, 3)
    bt = bt.reshape(NSA_GROUPS, nd, NSA_TQ, NSA_HPG * NSA_TQ)
    nblk = seq // NSA_BLOCK
    off = NSA_BLOCK * (nblk - 1) + NSA_BLOCK - 1
    bdp = jnp.concatenate([jnp.zeros((NSA_HEADS, off), _F32), bd_nsa], axis=1)
    bc = jnp.stack([bdp[:, off - (NSA_BLOCK * n + NSA_BLOCK - 1):off - (NSA_BLOCK * n + NSA_BLOCK - 1) + seq]
                    for n in range(nblk)], axis=1)
    bc = bc.reshape(NSA_GROUPS, NSA_HPG, nblk, seq)
    rows, cols = DIL_TQ, DIL_PW + DIL_TQ
    period = rows + cols
    dtabs = []
    for g, (w, d) in enumerate(DIL_PAIRS):
        heads = bd[NSA_HEADS + g * DIL_HPG:NSA_HEADS + (g + 1) * DIL_HPG] * LOG2E
        m = w // d + 1
        assert m == DIL_PW + 1
        by_offset = heads[:, 0:m * d:d]
        cyc = jnp.concatenate([by_offset[:, ::-1], jnp.full((DIL_HPG, period - m), NEG, _F32)], axis=1)
        flat = jnp.tile(cyc, (1, rows))[:, :rows * (period - 1)]
        by_query = flat.reshape(DIL_HPG, rows, period - 1)[:, :, :cols]
        dtabs.append(by_query.transpose(0, 2, 1))
    return bt, bc, jnp.stack(dtabs, axis=0)


def _rope_tables(seq):
    half = MLA_ROPE // 2
    freqs = 1.0 / (ROPE_THETA ** (jnp.arange(half, dtype=_F32) / half))
    ang = jnp.arange(seq).astype(_F32)[:, None] * freqs[None, :]
    zeros = jnp.zeros((seq, LANES - MLA_ROPE), _F32)
    cos = jnp.concatenate([jnp.cos(ang), jnp.cos(ang), zeros], axis=1)
    sin = jnp.concatenate([jnp.sin(ang), jnp.sin(ang), zeros], axis=1)
    return cos, sin


def _rot_cols(w):
    half = w.shape[-1] // 2
    return jnp.concatenate([-w[..., half:], w[..., :half]], axis=-1)


ZF_COLS = {"gate_a": 0, "gate_b": A_GATE, "kcvc": A_GATE + B_GATE, "g3": A_GATE + B_GATE + 4 * HEAD_DIM}


def _even_weights(w_in):
    w_in = w_in.astype(_BF)
    o = 0
    q_a = w_in[:, o:o + A_Q]; o += A_Q
    kv_a = w_in[:, o:o + A_KV]; o += A_KV
    g3 = w_in[:, o:o + A_G3]; o += A_G3
    gate_a = w_in[:, o:o + A_GATE]; o += A_GATE
    qkv_b = w_in[:, o:o + B_QKV]; o += B_QKV
    gate_b = w_in[:, o:o + B_GATE]
    gw = NSA_GROUPS * HEAD_DIM
    kc_vc, ks, vs, kw, vw = (kv_a[:, :2 * gw], kv_a[:, 2 * gw:3 * gw], kv_a[:, 3 * gw:4 * gw],
                             kv_a[:, 4 * gw:5 * gw], kv_a[:, 5 * gw:6 * gw])
    wb = jnp.concatenate([q_a, ks, kw], axis=1).astype(_BF)
    wvt = jnp.concatenate([vs, vw], axis=1).T.astype(_BF)
    dw = DIL_HPG * HEAD_DIM
    nq = DIL_HEADS * HEAD_DIM
    wd = jnp.concatenate([qkv_b[:, part * nq + g * dw:part * nq + (g + 1) * dw]
                          for g in range(DIL_GROUPS) for part in range(3)], axis=1).astype(_BF)
    g3_blocks = []
    for g in range(NSA_GROUPS):
        cols = [(g * NSA_HPG + h) * 3 + j for j in range(3) for h in range(NSA_HPG)]
        blk = g3[:, np.asarray(cols)]
        g3_blocks.append(jnp.pad(blk, ((0, 0), (0, LANES - len(cols)))))
    wf = jnp.concatenate([gate_a, gate_b, kc_vc] + g3_blocks, axis=1).astype(_BF)
    return wb, wvt, wd, wf


def _even_layer(x2d, h, w_in, cmp_pos, cmp_w1, cmp_w2, w_out, nw_next, tables, bsz, seq, h_dtype):
    bt, bc, dtab, onehot = tables
    wb, wvt, wd, wf = _even_weights(w_in)
    zb = _matmul(h, wb, _BF)
    vt = _matmul_nt(wvt, h, _BF)
    zd = _matmul(h, wd, _BF)
    zf = _matmul(h, wf, _F32)
    nblk = seq // NSA_BLOCK
    kcvc = zf[:, ZF_COLS["kcvc"]:ZF_COLS["kcvc"] + 4 * HEAD_DIM]
    kcvc_t = kcvc.reshape(bsz * nblk, NSA_BLOCK, 4, HEAD_DIM).transpose(2, 1, 0, 3)
    kvc = _compress(kcvc_t, cmp_pos.astype(_F32), cmp_w1.astype(_BF), cmp_w2.astype(_BF))
    kvc = kvc.reshape(4, bsz, nblk, HEAD_DIM)
    zf3 = zf.reshape(bsz, seq, zf.shape[-1])
    o_a = _nsa_attention(zb.reshape(bsz, seq, -1), vt, zf3, kvc, onehot, bt, bc, bsz, seq)
    o_b = _dilated_attention(zd.reshape(bsz, seq, -1), dtab, bsz, seq)
    return _out_even(x2d, o_a.reshape(bsz * seq, -1), o_b.reshape(bsz * seq, -1), zf, w_out.astype(_BF), nw_next,
                     ZF_COLS, h_dtype)


def _odd_layer(x2d, h, w_in, q_norm, w_qb, kv_norm, w_kvb, w_out, nw_next, rope, bsz, seq, h_dtype, keep_x):
    cos, sin = rope
    w_in, w_qb, w_kvb = w_in.astype(_BF), w_qb.astype(_BF), w_kvb.astype(_BF)
    o = 0
    w_cq = w_in[:, o:o + MLA_Q_RANK]; o += MLA_Q_RANK
    w_ckv = w_in[:, o:o + MLA_KV_RANK]; o += MLA_KV_RANK
    w_kpe = w_in[:, o:o + MLA_ROPE]; o += MLA_ROPE
    w_gate = w_in[:, o:]
    w1 = jnp.concatenate([w_cq, w_ckv, w_gate], axis=1).astype(_BF)
    z = _matmul(h, w1, _F32)
    zk = _matmul(h, jnp.concatenate([w_kpe, _rot_cols(w_kpe)], axis=1).astype(_BF), _F32)
    gate_width = MLA_HEADS * MLA_V
    q_latent = (q_norm, MLA_Q_RANK, 0)
    kv_latent = (kv_norm, MLA_KV_RANK, MLA_Q_RANK // MLA_KV_RANK)
    kpe = _rope_cols(zk, 0, cos, sin, seq)
    wq = w_qb.reshape(MLA_Q_RANK, MLA_HEADS, MLA_NOPE + MLA_ROPE)
    wq_pe = wq[:, :, MLA_NOPE:]
    wq = jnp.concatenate([wq[:, :, :MLA_NOPE], wq_pe, _rot_cols(wq_pe)], axis=-1)
    q = _matmul_rope(z, q_latent, wq.reshape(MLA_Q_RANK, MLA_HEADS * 2 * LANES).astype(_BF), cos, sin, seq)
    wkv = w_kvb.reshape(MLA_KV_RANK, MLA_HEADS, MLA_NOPE + MLA_V)
    wk = wkv[:, :, :MLA_NOPE].reshape(MLA_KV_RANK, -1).astype(_BF)
    wv_t = wkv[:, :, MLA_NOPE:].reshape(MLA_KV_RANK, -1).T.astype(_BF)
    kn = _matmul(z, wk, _BF, norm=kv_latent)
    vt = _matmul_nt(wv_t, z, _BF, norm=kv_latent)
    o_c = _mla_attention(q.reshape(bsz, seq, -1), kn.reshape(bsz, seq, -1), kpe.reshape(bsz, seq, LANES),
                         vt, bsz, seq)
    gate_block = (MLA_Q_RANK + MLA_KV_RANK) // gate_width
    return _out_odd(x2d, o_c.reshape(bsz * seq, -1), z, w_out.astype(_BF), nw_next, gate_block, h_dtype, keep_x)


def kernel(x, rel_bias, norm_w, final_norm_w, ev_w_in, nsa_cmp_pos, nsa_cmp_w1, nsa_cmp_w2, ev_w_out,
           od_w_in, mla_q_norm, mla_w_qb, mla_kv_norm, mla_w_kvb, od_w_out):
    bsz, seq, d = x.shape
    depth = norm_w.shape[0]
    assert seq % MLA_TQ == 0 and seq % (DIL_TQ * DIL_PAIRS[-1][1]) == 0 and seq // NSA_BLOCK <= LANES
    assert NSA_WINDOW == 2 * NSA_TQ
    bt, bc, dtab = _bias_tables(rel_bias, seq)
    blk_id = np.arange(seq)[:, None] // NSA_BLOCK
    onehot = jnp.asarray(blk_id == np.arange(LANES)[None, :], dtype=_BF)
    tables = (bt, bc, dtab, onehot)
    rope = _rope_tables(seq)
    x2d = x.reshape(bsz * seq, d).astype(_F32)
    h = _rmsnorm(x2d, norm_w[0], d, 0, _BF)
    for l in range(depth):
        last = l == depth - 1
        nw_next = final_norm_w if last else norm_w[l + 1]
        h_dtype = _F32 if last else _BF
        i = l // 2
        if l % 2 == 0:
            x2d, h = _even_layer(x2d, h, ev_w_in[i], nsa_cmp_pos[i], nsa_cmp_w1[i], nsa_cmp_w2[i],
                                 ev_w_out[i], nw_next, tables, bsz, seq, h_dtype)
        else:
            x2d, h = _odd_layer(x2d, h, od_w_in[i], mla_q_norm[i], mla_w_qb[i], mla_kv_norm[i],
                                mla_w_kvb[i], od_w_out[i], nw_next, rope, bsz, seq, h_dtype, keep_x=not last)
    return h.reshape(bsz, seq, d)
```

```python
import functools
import math

import numpy as np
import jax
import jax.numpy as jnp
from jax import lax
from jax.experimental import pallas as pl
from jax.experimental.pallas import tpu as pltpu

HEAD_DIM = 128
EPS = 1e-6
NEG = -1e30
NUM_BUCKETS = 32
MAX_DISTANCE = 2048
NSA_HEADS = 8
NSA_GROUPS = 2
NSA_HPG = 4
NSA_BLOCK = 64
NSA_TOPK = 16
NSA_WINDOW = 512
DIL_PAIRS = ((128, 1), (512, 4), (2048, 16))
DIL_GROUPS = 3
DIL_HPG = 4
DIL_HEADS = 12
MLA_HEADS = 16
MLA_Q_RANK = 1536
MLA_KV_RANK = 512
MLA_NOPE = 128
MLA_ROPE = 64
MLA_V = 128
ROPE_THETA = 10000.0

A_Q = NSA_HEADS * HEAD_DIM
A_KV = 6 * NSA_GROUPS * HEAD_DIM
A_G3 = 3 * NSA_HEADS
A_GATE = NSA_HEADS * HEAD_DIM
B_QKV = 3 * DIL_HEADS * HEAD_DIM
B_GATE = DIL_HPG * HEAD_DIM

LANES = 128
SUBLANES = 8
VMEM_LIMIT = 56 * 1024 * 1024

NSA_TQ = 256
NSA_CHAINS = 1
DIL_TQ = 256
DIL_PW = 128
DIL_UNROLL = 16
MLA_TQ = 512
OUT_TM = 512
MLA_HPS = 4
LOG2E = math.log2(math.e)

_BF = jnp.bfloat16
_F32 = jnp.float32


def _cparams(sem):
    return pltpu.CompilerParams(dimension_semantics=sem, vmem_limit_bytes=VMEM_LIMIT)


def _dot(a, b):
    return jnp.dot(a, b, preferred_element_type=_F32)


def _dot_nt(a, b):
    return lax.dot_general(a, b, (((1,), (1,)), ((), ())), preferred_element_type=_F32)


def _silu(x):
    return x * (1.0 / (1.0 + jnp.exp(-x)))


def _sigmoid(x):
    return 1.0 / (1.0 + jnp.exp(-x))


def _pick_tile(n, candidates):
    for c in candidates:
        if n % c == 0:
            return c
    return n


def _rmsnorm_kernel(x_ref, w_ref, o_ref):
    x = x_ref[...]
    y = x * lax.rsqrt(jnp.mean(x * x, axis=-1, keepdims=True) + EPS)
    o_ref[...] = (y * w_ref[...]).astype(o_ref.dtype)


def _rmsnorm(x2d, w, width, col_block, out_dtype):
    m = x2d.shape[0]
    tm = _pick_tile(m, (512, 256, 128))
    return pl.pallas_call(
        _rmsnorm_kernel,
        grid=(m // tm,),
        in_specs=[pl.BlockSpec((tm, width), lambda i: (i, col_block)),
                  pl.BlockSpec((1, width), lambda i: (0, 0))],
        out_specs=pl.BlockSpec((tm, width), lambda i: (i, 0)),
        out_shape=jax.ShapeDtypeStruct((m, width), out_dtype),
        compiler_params=_cparams(("parallel",)),
        name="rmsnorm",
    )(x2d, w.reshape(1, width).astype(_F32))


def _matmul_kernel(a_ref, w_ref, o_ref):
    o_ref[...] = _dot(a_ref[...], w_ref[...]).astype(o_ref.dtype)


def _normed_lhs(a_ref, nw_ref, an_ref):
    @pl.when(pl.program_id(1) == 0)
    def _():
        x = a_ref[...]
        y = x * lax.rsqrt(jnp.mean(x * x, axis=-1, keepdims=True) + EPS)
        an_ref[...] = (y * nw_ref[...]).astype(an_ref.dtype)
    return an_ref[...]


def _norm_matmul_kernel(a_ref, nw_ref, w_ref, o_ref, an_ref):
    o_ref[...] = _dot(_normed_lhs(a_ref, nw_ref, an_ref), w_ref[...]).astype(o_ref.dtype)


def _lhs_specs(a, norm, tm):
    if norm is None:
        k = a.shape[1]
        return k, [pl.BlockSpec((tm, k), lambda i, j: (i, 0))], [], []
    nw, k, cb = norm
    specs = [pl.BlockSpec((tm, k), lambda i, j: (i, cb)), pl.BlockSpec((1, k), lambda i, j: (0, 0))]
    return k, specs, [nw.reshape(1, k).astype(_F32)], [pltpu.VMEM((tm, k), _BF)]


def _matmul(a, w, out_dtype, norm=None):
    m = a.shape[0]
    n = w.shape[1]
    tm = _pick_tile(m, (1024, 512, 256, 128))
    k, a_specs, a_args, scratch = _lhs_specs(a, norm, tm)
    wide = (2048,) if k * 2 <= 1024 else ()
    tn = _pick_tile(n, wide + (1536, 1024, 768, 512, 384, 256, 128))
    return pl.pallas_call(
        _matmul_kernel if norm is None else _norm_matmul_kernel,
        grid=(m // tm, n // tn),
        in_specs=a_specs + [pl.BlockSpec((k, tn), lambda i, j: (0, j))],
        out_specs=pl.BlockSpec((tm, tn), lambda i, j: (i, j)),
        out_shape=jax.ShapeDtypeStruct((m, n), out_dtype),
        scratch_shapes=scratch,
        compiler_params=_cparams(("parallel", "arbitrary")),
        name="matmul" if norm is None else "norm_matmul",
    )(a, *a_args, w)


def _matmul_nt_kernel(w_ref, a_ref, o_ref):
    o_ref[...] = _dot_nt(w_ref[...], a_ref[...]).astype(o_ref.dtype)


def _norm_matmul_nt_kernel(w_ref, a_ref, nw_ref, o_ref, an_ref):
    o_ref[...] = _dot_nt(w_ref[...], _normed_lhs(a_ref, nw_ref, an_ref)).astype(o_ref.dtype)


def _matmul_nt(w_t, a, out_dtype, norm=None):
    n = w_t.shape[0]
    m = a.shape[0]
    tm = _pick_tile(m, (1024, 512, 256, 128))
    tn = _pick_tile(n, (2048, 1024, 512, 256, 128))
    k, a_specs, a_args, scratch = _lhs_specs(a, norm, tm)
    return pl.pallas_call(
        _matmul_nt_kernel if norm is None else _norm_matmul_nt_kernel,
        grid=(m // tm, n // tn),
        in_specs=[pl.BlockSpec((tn, k), lambda i, j: (j, 0))] + a_specs,
        out_specs=pl.BlockSpec((tn, tm), lambda i, j: (j, i)),
        out_shape=jax.ShapeDtypeStruct((n, m), out_dtype),
        scratch_shapes=scratch,
        compiler_params=_cparams(("parallel", "arbitrary")),
        name="matmul_nt" if norm is None else "norm_matmul_nt",
    )(w_t, a, *a_args)


def _rope_chunk(chunk, cos, sin):
    return chunk * cos + pltpu.roll(chunk, 64, 1) * sin


def _matmul_rope_kernel(a_ref, nw_ref, w_ref, cos_ref, sin_ref, o_ref, an_ref, *, heads_per_tile):
    acc = _dot(_normed_lhs(a_ref, nw_ref, an_ref), w_ref[...])
    cos = cos_ref[...]
    sin = sin_ref[...]
    for h in range(heads_per_tile):
        base = h * 2 * LANES
        o_ref[:, base:base + LANES] = acc[:, base:base + LANES].astype(o_ref.dtype)
        o_ref[:, base + LANES:base + 2 * LANES] = _rope_chunk(
            acc[:, base + LANES:base + 2 * LANES], cos, sin).astype(o_ref.dtype)


def _matmul_rope(a, norm, w, cos, sin, seq):
    m = a.shape[0]
    n = w.shape[1]
    tm = _pick_tile(seq, (1024, 512, 256, 128))
    tn = 2048
    tpb = seq // tm
    k, a_specs, a_args, scratch = _lhs_specs(a, norm, tm)
    return pl.pallas_call(
        functools.partial(_matmul_rope_kernel, heads_per_tile=tn // (2 * LANES)),
        grid=(m // tm, n // tn),
        in_specs=a_specs + [pl.BlockSpec((k, tn), lambda i, j: (0, j)),
                            pl.BlockSpec((tm, LANES), lambda i, j: (i % tpb, 0)),
                            pl.BlockSpec((tm, LANES), lambda i, j: (i % tpb, 0))],
        out_specs=pl.BlockSpec((tm, tn), lambda i, j: (i, j)),
        out_shape=jax.ShapeDtypeStruct((m, n), _BF),
        scratch_shapes=scratch,
        compiler_params=_cparams(("parallel", "arbitrary")),
        name="matmul_rope",
    )(a, *a_args, w, cos, sin)


def _rope_cols_kernel(x_ref, cos_ref, sin_ref, o_ref):
    o_ref[...] = _rope_chunk(x_ref[...], cos_ref[...], sin_ref[...]).astype(o_ref.dtype)


def _rope_cols(z2d, col_block, cos, sin, seq):
    m = z2d.shape[0]
    tm = _pick_tile(seq, (512, 256, 128))
    tpb = seq // tm
    return pl.pallas_call(
        _rope_cols_kernel,
        grid=(m // tm,),
        in_specs=[pl.BlockSpec((tm, LANES), lambda i: (i, col_block)),
                  pl.BlockSpec((tm, LANES), lambda i: (i % tpb, 0)),
                  pl.BlockSpec((tm, LANES), lambda i: (i % tpb, 0))],
        out_specs=pl.BlockSpec((tm, LANES), lambda i: (i, 0)),
        out_shape=jax.ShapeDtypeStruct((m, LANES), _BF),
        compiler_params=_cparams(("parallel",)),
        name="rope_kpe",
    )(z2d, cos, sin)


def _compress_kernel(blk_ref, pos_ref, w1_ref, w2_ref, o_ref, acc_ref, *, lt):
    li = pl.program_id(1)

    @pl.when(li == 0)
    def _():
        acc_ref[...] = jnp.zeros_like(acc_ref)

    acc = acc_ref[...]
    for l in range(lt):
        a = (blk_ref[l] + pos_ref[l:l + 1, :]).astype(_BF)
        acc = acc + _dot(a, w1_ref[l])
    acc_ref[...] = acc

    @pl.when(li == pl.num_programs(1) - 1)
    def _():
        hid = _silu(acc_ref[...]).astype(_BF)
        o_ref[...] = _dot(hid, w2_ref[...])


def _compress(kcvc_t, pos, w1, w2):
    _, L, R, _ = kcvc_t.shape
    lt = 16
    return pl.pallas_call(
        functools.partial(_compress_kernel, lt=lt),
        grid=(4, L // lt),
        in_specs=[pl.BlockSpec((None, lt, R, HEAD_DIM), lambda c, l: (c, l, 0, 0)),
                  pl.BlockSpec((None, lt, HEAD_DIM), lambda c, l: (c // 2, l, 0)),
                  pl.BlockSpec((None, lt, HEAD_DIM, HEAD_DIM), lambda c, l: (c // 2, l, 0, 0)),
                  pl.BlockSpec((None, HEAD_DIM, HEAD_DIM), lambda c, l: (c // 2, 0, 0))],
        out_specs=pl.BlockSpec((None, R, HEAD_DIM), lambda c, l: (c, 0, 0)),
        out_shape=jax.ShapeDtypeStruct((4, R, HEAD_DIM), _F32),
        scratch_shapes=[pltpu.VMEM((R, HEAD_DIM), _F32)],
        compiler_params=_cparams(("parallel", "arbitrary")),
        name="nsa_compress",
    )(kcvc_t, pos, w1.reshape(2, L, HEAD_DIM, HEAD_DIM), w2)


def _flash_reset(m_ref, l_ref, acc_ref):
    m_ref[...] = jnp.full(m_ref.shape, -jnp.inf, _F32)
    l_ref[...] = jnp.zeros(l_ref.shape, _F32)
    acc_ref[...] = jnp.zeros(acc_ref.shape, _F32)


def _flash_update(s, vt, c, m_ref, l_ref, acc_ref, tile_max=None):
    m_prev = m_ref[...]
    if tile_max is None:
        tile_max = jnp.max(s, axis=0, keepdims=True)
    m_new = jnp.maximum(m_prev, tile_max)
    if c is None:
        alpha = jnp.exp2(m_prev - m_new)
        p = jnp.exp2(s - m_new)
    else:
        alpha = jnp.exp2((m_prev - m_new) * c)
        p = jnp.exp2((s - m_new) * c)
    l_ref[...] = alpha * l_ref[...] + jnp.sum(p, axis=0, keepdims=True)
    acc_ref[...] = alpha * acc_ref[...] + _dot(vt, p.astype(_BF))
    m_ref[...] = m_new


def _store_scores(buf, s):
    s_ref, mx_ref = buf
    s_ref[...] = s
    mx_ref[...] = jnp.max(s, axis=0, keepdims=True)


def _causal_flash(n_off, scores_into, consume, buf_a, buf_b, primed=False):
    if not primed:
        scores_into(buf_a, 0)

    def pair(p, carry):
        k0 = 2 * p
        scores_into(buf_b, k0 + 1)
        consume(buf_a, k0, False)
        scores_into(buf_a, k0 + 2)
        consume(buf_b, k0 + 1, False)
        return carry

    lax.fori_loop(0, n_off // 2, pair, 0)

    @pl.when(n_off % 2 == 1)
    def _():
        scores_into(buf_b, n_off)
        consume(buf_a, n_off - 1, False)
        consume(buf_b, n_off, True)

    @pl.when(n_off % 2 == 0)
    def _():
        consume(buf_a, n_off, True)


def _nsa_kernel(q_ref, ks_ref, kw_ref, vst_ref, vwt_ref, kc_ref, vc_ref, g3_ref, oh_ref, bt_ref, bc_ref,
                o_ref, *scratch, seq):
    tq = NSA_TQ
    hp = NSA_HPG
    n = hp * tq
    nc = n // NSA_CHAINS
    nblk = seq // NSA_BLOCK
    qi = pl.program_id(2)
    scale = HEAD_DIM ** -0.5
    c = scale * LOG2E
    t0 = qi * tq

    per_chain = len(scratch) // NSA_CHAINS
    chains = []
    for ch in range(NSA_CHAINS):
        (sa_ref, sb_ref, w0_ref, w1_ref, w2_ref, mxa_ref, mxb_ref, m_ref, l_ref, mw_ref, lw_ref,
         acc_s_ref, acc_w_ref) = scratch[ch * per_chain:(ch + 1) * per_chain]
        chains.append(dict(buf_a=(sa_ref, mxa_ref), buf_b=(sb_ref, mxb_ref), win=(w0_ref, w1_ref, w2_ref),
                           slc_state=(m_ref, l_ref, acc_s_ref), win_state=(mw_ref, lw_ref, acc_w_ref),
                           lanes=slice(ch * nc, (ch + 1) * nc)))

    qs = jnp.concatenate([q_ref[:, h * HEAD_DIM:(h + 1) * HEAD_DIM] for h in range(hp)], axis=0)

    def rows_of(kt):
        return pl.ds(pl.multiple_of(kt * tq, tq), tq)

    for delta in range(3):
        kw = kw_ref[rows_of(jnp.maximum(qi - delta, 0)), :]
        for ch in chains:
            ch["win"][delta][...] = _dot_nt(kw, qs[ch["lanes"], :]) * c + bt_ref[delta, :, ch["lanes"]]

    kc = kc_ref[...].astype(_BF)
    s_c = _dot_nt(kc, qs) * scale
    s_c = s_c + jnp.concatenate([bc_ref[h] for h in range(hp)], axis=1)
    n_col = lax.broadcasted_iota(jnp.int32, (nblk, tq), 0)
    t_row = t0 + lax.broadcasted_iota(jnp.int32, (nblk, tq), 1)
    n_all = lax.broadcasted_iota(jnp.int32, (nblk, n), 0)
    t_all = t0 + (lax.broadcasted_iota(jnp.int32, (nblk, n), 1) & (tq - 1))
    valid = t_all >= n_all * NSA_BLOCK + (NSA_BLOCK - 1)
    s_c = jnp.where(valid, s_c, NEG)
    m_c = jnp.max(s_c, axis=0, keepdims=True)
    p_c = jnp.where(valid, jnp.exp(s_c - m_c), 0.0)
    l_c = jnp.sum(p_c, axis=0, keepdims=True)
    p_c = p_c / jnp.where(l_c > 0.0, l_c, 1.0)
    imp = p_c[:, 0:tq]
    for h in range(1, hp):
        imp = imp + p_c[:, h * tq:(h + 1) * tq]

    pad_rows = LANES - nblk
    vc_pad = jnp.concatenate([vc_ref[...], jnp.zeros((pad_rows, HEAD_DIM), _F32)], axis=0)
    vc_t = jnp.transpose(vc_pad).astype(_BF)
    p_pad = jnp.concatenate([p_c, jnp.zeros((pad_rows, n), _F32)], axis=0).astype(_BF)
    o_cmp = _dot(vc_t, p_pad)

    cur = jnp.right_shift(t_row, int(math.log2(NSA_BLOCK)))
    forced = (n_col == 0) | (n_col == cur) | (n_col == cur - 1)
    impv = jnp.where(forced, jnp.inf, jnp.where(n_col > cur, -jnp.inf, imp))
    rank = jnp.zeros((nblk, tq), _F32)
    sub_row = lax.broadcasted_iota(jnp.int32, (SUBLANES, tq), 0)
    for i in range(nblk):
        row = impv[i:i + 1, :]
        lo = i // SUBLANES * SUBLANES
        parts = []
        if lo > 0:
            parts.append(jnp.where(row > impv[:lo], 1.0, 0.0))
        mid = impv[lo:lo + SUBLANES]
        parts.append(jnp.where(sub_row > i - lo,
                               jnp.where(row >= mid, 1.0, 0.0), jnp.where(row > mid, 1.0, 0.0)))
        if lo + SUBLANES < nblk:
            parts.append(jnp.where(row >= impv[lo + SUBLANES:], 1.0, 0.0))
        rank = rank + jnp.concatenate(parts, axis=0)
    sel = (rank < float(min(NSA_TOPK, nblk))) & (n_col <= cur)
    pen = jnp.where(sel, 0.0, NEG)
    pen = jnp.concatenate([pen, jnp.full((pad_rows, tq), NEG, _F32)], axis=0)
    pen_t = jnp.transpose(pen).astype(_BF)
    q_aug = jnp.concatenate([qs, jnp.concatenate([pen_t] * hp, axis=0)], axis=1)

    jj = lax.broadcasted_iota(jnp.int32, (tq, nc), 0)
    ii = lax.broadcasted_iota(jnp.int32, (tq, nc), 1) & (tq - 1)

    def slc_scores(side, ki):
        rows = rows_of(ki)
        k_aug = jnp.concatenate([ks_ref[rows, :], oh_ref[rows, :]], axis=1)
        for ch in chains:
            _store_scores(ch[side], _dot_nt(k_aug, q_aug[ch["lanes"], :]) * c + bt_ref[qi - ki, :, ch["lanes"]])

    def slc_consume(side, ki, diagonal):
        vt = vst_ref[:, rows_of(ki)]
        for ch in chains:
            s = ch[side][0][...]
            tile_max = ch[side][1][...]
            if diagonal:
                s = jnp.where(jj <= ii, s, NEG)
                tile_max = None
            _flash_update(s, vt, None, *ch["slc_state"], tile_max)

    slc_scores("buf_a", 0)

    def win_consume(delta, keep):
        vt = vwt_ref[:, rows_of(jnp.maximum(qi - delta, 0))]
        for ch in chains:
            _flash_update(jnp.where(keep, ch["win"][delta][...], NEG), vt, None, *ch["win_state"])

    for ch in chains:
        _flash_reset(*ch["win_state"])
        _flash_reset(*ch["slc_state"])
    win_consume(0, jj <= ii)
    win_consume(1, qi >= 1)
    win_consume(2, (jj > ii) & (qi >= 2))

    _causal_flash(qi, slc_scores, slc_consume, "buf_a", "buf_b", primed=True)

    g_t = jnp.transpose(_sigmoid(g3_ref[...]))
    for ch in chains:
        ch["o_win"] = ch["win_state"][2][...] / ch["win_state"][1][...]
        ch["o_slc"] = ch["slc_state"][2][...] / ch["slc_state"][1][...]
    for h in range(hp):
        ch = chains[h * tq // nc]
        cols = slice(h * tq, (h + 1) * tq)
        local = slice(h * tq % nc, h * tq % nc + tq)
        out_t = (g_t[0 * hp + h:0 * hp + h + 1, :] * o_cmp[:, cols]
                 + g_t[1 * hp + h:1 * hp + h + 1, :] * ch["o_slc"][:, local]
                 + g_t[2 * hp + h:2 * hp + h + 1, :] * ch["o_win"][:, local])
        o_ref[:, h * HEAD_DIM:(h + 1) * HEAD_DIM] = jnp.transpose(out_t).astype(o_ref.dtype)


def _nsa_attention(zb, vt, zf, kvc, onehot, bt, bc, bsz, seq):
    tq = NSA_TQ
    n = NSA_HPG * tq
    nblk = seq // NSA_BLOCK
    nd = seq // tq
    kcol = A_Q // HEAD_DIM
    g3col = ZF_COLS["g3"] // LANES
    return pl.pallas_call(
        functools.partial(_nsa_kernel, seq=seq),
        grid=(NSA_GROUPS, bsz, seq // tq),
        in_specs=[
            pl.BlockSpec((None, tq, NSA_HPG * HEAD_DIM), lambda g, b, i: (b, i, g)),
            pl.BlockSpec((None, seq, HEAD_DIM), lambda g, b, i: (b, 0, kcol + g)),
            pl.BlockSpec((None, seq, HEAD_DIM), lambda g, b, i: (b, 0, kcol + NSA_GROUPS + g)),
            pl.BlockSpec((HEAD_DIM, seq), lambda g, b, i: (g, b)),
            pl.BlockSpec((HEAD_DIM, seq), lambda g, b, i: (NSA_GROUPS + g, b)),
            pl.BlockSpec((None, None, nblk, HEAD_DIM), lambda g, b, i: (g, b, 0, 0)),
            pl.BlockSpec((None, None, nblk, HEAD_DIM), lambda g, b, i: (2 + g, b, 0, 0)),
            pl.BlockSpec((None, tq, LANES), lambda g, b, i: (b, i, g3col + g)),
            pl.BlockSpec((seq, LANES), lambda g, b, i: (0, 0)),
            pl.BlockSpec((None, nd, tq, n), lambda g, b, i: (g, 0, 0, 0), pipeline_mode=pl.Buffered(1)),
            pl.BlockSpec((None, NSA_HPG, nblk, tq), lambda g, b, i: (g, 0, 0, i)),
        ],
        out_specs=pl.BlockSpec((None, tq, NSA_HPG * HEAD_DIM), lambda g, b, i: (b, i, g)),
        out_shape=jax.ShapeDtypeStruct((bsz, seq, NSA_HEADS * HEAD_DIM), _BF),
        scratch_shapes=NSA_CHAINS * ([pltpu.VMEM((tq, n // NSA_CHAINS), _F32) for _ in range(5)]
                                     + [pltpu.VMEM((1, n // NSA_CHAINS), _F32) for _ in range(6)]
                                     + [pltpu.VMEM((HEAD_DIM, n // NSA_CHAINS), _F32) for _ in range(2)]),
        compiler_params=_cparams(("arbitrary", "arbitrary", "arbitrary")),
        name="nsa_attention",
    )(zb, zb, zb, vt, vt, kvc, kvc, zf, onehot, bt, bc)


def _dilated_kernel(*refs, seq):
    qkv_refs, (tab_ref, out_ref, qs_ref, ks_ref, vs_ref, lse_ref, o_ref) = refs[:9], refs[9:]
    tq, pw = DIL_TQ, DIL_PW
    c = HEAD_DIM ** -0.5 * LOG2E
    for g, (_, d) in enumerate(DIL_PAIRS):
        q_ref, k_ref, v_ref = qkv_refs[3 * g:3 * g + 3]
        qs_ref[...] = q_ref[...].astype(_F32)
        ks_ref[...] = k_ref[...].astype(_F32)
        vs_ref[...] = v_ref[...].astype(_F32)
        ntile = seq // d // tq
        tab_p = tab_ref[g, :pw, :]
        tab_d = tab_ref[g, pw:, :]

        def scores(idx, d=d, ntile=ntile, tab_d=tab_d, tab_p=tab_p):
            r = idx // ntile
            tile = idx % ntile
            u0 = tile * tq
            rows = pl.ds(u0 * d + r, tq, stride=d)
            rows_p = pl.ds(jnp.maximum(u0 - pw, 0) * d + r, pw, stride=d)
            q = qs_ref[rows, :].astype(_BF)
            s_d = _dot_nt(ks_ref[rows, :].astype(_BF), q) * c + tab_d
            s_p = jnp.where(tile > 0, _dot_nt(ks_ref[rows_p, :].astype(_BF), q) * c + tab_p, NEG)
            return rows, rows_p, s_d, s_p

        def softmax(rows, rows_p, s_d, s_p):
            m = jnp.maximum(jnp.max(s_d, axis=0, keepdims=True), jnp.max(s_p, axis=0, keepdims=True))
            p_d = jnp.exp2(s_d - m)
            p_p = jnp.exp2(s_p - m)
            l = jnp.sum(p_d, axis=0, keepdims=True) + jnp.sum(p_p, axis=0, keepdims=True)
            return rows, rows_p, p_d.astype(_BF), p_p.astype(_BF), l, m + jnp.log2(l)

        def values(rows, rows_p, p_d, p_p, l, lse):
            vt_d = jnp.transpose(vs_ref[rows, :]).astype(_BF)
            vt_p = jnp.transpose(vs_ref[rows_p, :]).astype(_BF)
            acc = _dot(vt_d, p_d) + _dot(vt_p, p_p)
            return rows, jnp.transpose(acc / l), jnp.transpose(jnp.broadcast_to(lse, (HEAD_DIM, tq)))

        def tile_body(it, carry, g=g, scores=scores):
            staged = [scores(it * DIL_UNROLL + j) for j in range(DIL_UNROLL)]
            staged = [softmax(*x) for x in staged]
            for rows, o, lse in [values(*x) for x in staged]:
                if g == 0:
                    o_ref[rows, :] = o
                    lse_ref[rows, :] = lse
                else:
                    lse_old = lse_ref[rows, :]
                    top = jnp.maximum(lse_old, lse)
                    e_old = jnp.exp2(lse_old - top)
                    e_new = jnp.exp2(lse - top)
                    o_ref[rows, :] = (e_old * o_ref[rows, :] + e_new * o) / (e_old + e_new)
                    lse_ref[rows, :] = top + jnp.log2(e_old + e_new)
            return carry

        assert (d * ntile) % DIL_UNROLL == 0
        lax.fori_loop(0, d * ntile // DIL_UNROLL, tile_body, 0)
    out_ref[...] = o_ref[...].astype(out_ref.dtype)


def _dilated_attention(zd3, dtab, bsz, seq):
    width = DIL_HPG * HEAD_DIM
    col = lambda part: pl.BlockSpec((None, seq, HEAD_DIM), lambda b, h, part=part: (b, 0, part * DIL_HPG + h))
    return pl.pallas_call(
        functools.partial(_dilated_kernel, seq=seq),
        grid=(bsz, DIL_HPG),
        in_specs=[col(part) for part in range(3 * DIL_GROUPS)]
        + [pl.BlockSpec((DIL_GROUPS, None, DIL_PW + DIL_TQ, DIL_TQ), lambda b, h: (0, h, 0, 0))],
        out_specs=pl.BlockSpec((None, seq, HEAD_DIM), lambda b, h: (b, 0, h)),
        out_shape=jax.ShapeDtypeStruct((bsz, seq, width), _BF),
        scratch_shapes=[pltpu.VMEM((seq, HEAD_DIM), _F32) for _ in range(5)],
        compiler_params=_cparams(("parallel", "arbitrary")),
        name="dilated_attention",
    )(*([zd3] * (3 * DIL_GROUPS)), dtab)


def _mla_kernel(qi_tab, ki_tab, q_ref, kn_ref, kp_ref, vt_ref, o_ref, *scratch, ntiles):
    tq = MLA_TQ
    c = (MLA_NOPE + MLA_ROPE) ** -0.5 * LOG2E
    per_head = len(scratch) // MLA_HPS
    heads = []
    for hd in range(MLA_HPS):
        sa_ref, sb_ref, mxa_ref, mxb_ref, m_ref, l_ref, acc_ref = scratch[hd * per_head:(hd + 1) * per_head]
        heads.append(dict(bufs=((sa_ref, mxa_ref), (sb_ref, mxb_ref)), state=(m_ref, l_ref, acc_ref),
                          qcols=slice(hd * 2 * LANES, (hd + 1) * 2 * LANES),
                          kcols=slice(hd * MLA_NOPE, (hd + 1) * MLA_NOPE), vrows=slice(hd * MLA_V, (hd + 1) * MLA_V)))
        _flash_reset(m_ref, l_ref, acc_ref)
    jj = lax.broadcasted_iota(jnp.int32, (tq, tq), 0)
    ii = lax.broadcasted_iota(jnp.int32, (tq, tq), 1)

    def rows_of(i):
        return pl.ds(pl.multiple_of(i * tq, tq), tq)

    def scores_into(slot, t):
        rows = rows_of(ki_tab[t])
        q_rows = rows_of(qi_tab[t])
        for hd in heads:
            k = jnp.concatenate([kn_ref[rows, hd["kcols"]], kp_ref[rows, :]], axis=1)
            _store_scores(hd["bufs"][slot], _dot_nt(k, q_ref[q_rows, hd["qcols"]]))

    def consume(slot, t, diagonal):
        rows = rows_of(ki_tab[t])
        for hd in heads:
            s_ref, mx_ref = hd["bufs"][slot]
            m_ref, l_ref, acc_ref = hd["state"]
            s = s_ref[...]
            tile_max = mx_ref[...]
            if diagonal:
                s = jnp.where(jj <= ii, s, NEG)
                tile_max = None
            _flash_update(s, vt_ref[hd["vrows"], rows], c, m_ref, l_ref, acc_ref, tile_max)
            if diagonal:
                o_ref[rows_of(qi_tab[t]), hd["vrows"]] = jnp.transpose(acc_ref[...] / l_ref[...]).astype(o_ref.dtype)
                _flash_reset(m_ref, l_ref, acc_ref)

    scores_into(0, 0)

    def pair(p, carry):
        t0 = 2 * p
        d0 = ki_tab[t0] == qi_tab[t0]
        d1 = ki_tab[t0 + 1] == qi_tab[t0 + 1]
        for x, y in ((False, False), (True, False), (False, True)):
            @pl.when(jnp.logical_and(d0 == x, d1 == y))
            def _(x=x, y=y):
                scores_into(1, t0 + 1)
                consume(0, t0, x)
                scores_into(0, t0 + 2)
                consume(1, t0 + 1, y)
        return carry

    lax.fori_loop(0, ntiles // 2, pair, 0)


def _mla_attention(q, kn, kpe, vt, bsz, seq):
    tq = MLA_TQ
    pairs = [(qi, ki) for qi in range(seq // tq) for ki in range(qi + 1)]
    ntiles = len(pairs)
    diag = [qi == ki for qi, ki in pairs]
    assert ntiles % 2 == 0 and not any(diag[t] and diag[t + 1] for t in range(0, ntiles, 2))
    pairs.append((0, 0))
    qi_tab = jnp.asarray([p[0] for p in pairs], jnp.int32)
    ki_tab = jnp.asarray([p[1] for p in pairs], jnp.int32)
    smem = pl.BlockSpec(memory_space=pltpu.SMEM)
    return pl.pallas_call(
        functools.partial(_mla_kernel, ntiles=ntiles),
        grid=(bsz, MLA_HEADS // MLA_HPS),
        in_specs=[
            smem, smem,
            pl.BlockSpec((None, seq, MLA_HPS * 2 * LANES), lambda b, h: (b, 0, h)),
            pl.BlockSpec((None, seq, MLA_HPS * MLA_NOPE), lambda b, h: (b, 0, h)),
            pl.BlockSpec((None, seq, LANES), lambda b, h: (b, 0, 0)),
            pl.BlockSpec((MLA_HPS * MLA_V, seq), lambda b, h: (h, b)),
        ],
        out_specs=pl.BlockSpec((None, seq, MLA_HPS * MLA_V), lambda b, h: (b, 0, h)),
        out_shape=jax.ShapeDtypeStruct((bsz, seq, MLA_HEADS * MLA_V), _BF),
        scratch_shapes=MLA_HPS * ([pltpu.VMEM((tq, tq), _F32) for _ in range(2)]
                                  + [pltpu.VMEM((1, tq), _F32) for _ in range(4)]
                                  + [pltpu.VMEM((MLA_V, tq), _F32)]),
        compiler_params=_cparams(("parallel", "arbitrary")),
        name="mla_attention",
    )(qi_tab, ki_tab, q, kn, kpe, vt)


def _residual_norm_store(x_new, nw_ref, x_out_ref, h_out_ref):
    x_out_ref[...] = x_new
    y = x_new * lax.rsqrt(jnp.mean(x_new * x_new, axis=-1, keepdims=True) + EPS)
    h_out_ref[...] = (y * nw_ref[...]).astype(h_out_ref.dtype)


def _out_even_kernel(x_ref, oa_ref, ga_ref, ob_ref, gb_ref, wa_ref, wb_ref, nw_ref, x_out_ref, h_out_ref):
    mixed_a = (oa_ref[...] * _silu(ga_ref[...])).astype(_BF)
    mixed_b = (ob_ref[...] * _silu(gb_ref[...])).astype(_BF)
    x_new = x_ref[...] + _dot(mixed_a, wa_ref[...]) + _dot(mixed_b, wb_ref[...])
    _residual_norm_store(x_new, nw_ref, x_out_ref, h_out_ref)


def _out_even(x2d, o_a, o_b, zf, w_out, nw_next, zf_cols, h_dtype):
    m, d = x2d.shape
    tm = OUT_TM
    wa = A_GATE
    wbd = B_GATE
    row = lambda width, cb=0: pl.BlockSpec((tm, width), lambda i, cb=cb: (i, cb))
    full = lambda r, c: pl.BlockSpec((r, c), lambda i: (0, 0), pipeline_mode=pl.Buffered(1))
    return pl.pallas_call(
        _out_even_kernel,
        grid=(m // tm,),
        in_specs=[row(d), row(wa), row(wa, zf_cols["gate_a"] // wa), row(wbd), row(wbd, zf_cols["gate_b"] // wbd),
                  full(wa, d), full(wbd, d), full(1, d)],
        out_specs=[row(d), row(d)],
        out_shape=[jax.ShapeDtypeStruct((m, d), _F32), jax.ShapeDtypeStruct((m, d), h_dtype)],
        compiler_params=_cparams(("parallel",)),
        name="out_proj_even",
    )(x2d, o_a, zf, o_b, zf, w_out[:wa], w_out[wa:], nw_next.reshape(1, d).astype(_F32))


def _out_odd_kernel(x_ref, oc_ref, gc_ref, w_ref, nw_ref, *out_refs):
    mixed = (oc_ref[...] * _silu(gc_ref[...])).astype(_BF)
    x_new = x_ref[...] + _dot(mixed, w_ref[...])
    if len(out_refs) == 2:
        _residual_norm_store(x_new, nw_ref, *out_refs)
    else:
        y = x_new * lax.rsqrt(jnp.mean(x_new * x_new, axis=-1, keepdims=True) + EPS)
        out_refs[0][...] = (y * nw_ref[...]).astype(out_refs[0].dtype)


def _out_odd(x2d, o_c, z, w_out, nw_next, gate_col_block, h_dtype, keep_x):
    m, d = x2d.shape
    tm = OUT_TM
    width = MLA_HEADS * MLA_V
    row = lambda w, cb=0: pl.BlockSpec((tm, w), lambda i, cb=cb: (i, cb))
    full = lambda r, c: pl.BlockSpec((r, c), lambda i: (0, 0), pipeline_mode=pl.Buffered(1))
    h_shape = jax.ShapeDtypeStruct((m, d), h_dtype)
    outs = pl.pallas_call(
        _out_odd_kernel,
        grid=(m // tm,),
        in_specs=[row(d), row(width), row(width, gate_col_block), full(width, d), full(1, d)],
        out_specs=[row(d), row(d)] if keep_x else [row(d)],
        out_shape=[jax.ShapeDtypeStruct((m, d), _F32), h_shape] if keep_x else [h_shape],
        compiler_params=_cparams(("parallel",)),
        name="out_proj_odd",
    )(x2d, o_c, z, w_out, nw_next.reshape(1, d).astype(_F32))
    return (outs[0], outs[1]) if keep_x else (None, outs[0])


def _bucket_of_distance(n):
    d = np.arange(n)
    max_exact = NUM_BUCKETS // 2
    df = np.maximum(d, 1).astype(np.float32)
    large = max_exact + (np.log(df / max_exact) / math.log(MAX_DISTANCE / max_exact)
                         * (NUM_BUCKETS - max_exact)).astype(np.int32)
    large = np.minimum(large, NUM_BUCKETS - 1)
    return np.where(d < max_exact, d, large).astype(np.int32)


def _toeplitz_tiles(v, n, nd):
    hh = v.shape[0]
    vp = jnp.concatenate([jnp.zeros((hh, n), v.dtype), v, jnp.zeros((hh, n), v.dtype)], axis=1)
    idx = (n - np.arange(2 * n)) % (2 * n)
    tiles = []
    for dl in range(nd):
        w = vp[:, dl * n:dl * n + 2 * n]
        c = w[:, idx]
        flat = jnp.tile(c, (1, n))[:, :n * (2 * n - 1)]
        tiles.append(flat.reshape(hh, n, 2 * n - 1)[:, :, :n])
    return jnp.stack(tiles, axis=0)


def _bias_tables(rel_bias, seq):
    bd = rel_bias.astype(_F32)[_bucket_of_distance(seq)].T
    bd_nsa = bd[:NSA_HEADS]
    nd = seq // NSA_TQ
    bt = _toeplitz_tiles(bd_nsa * LOG2E, NSA_TQ, nd)
    bt = bt.reshape(nd, NSA_GROUPS, NSA_HPG, NSA_TQ, NSA_TQ).transpose(1, 0, 4, 2, 3)
    bt = bt.reshape(NSA_GROUPS, nd, NSA_TQ, NSA_HPG * NSA_TQ)
    nblk = seq // NSA_BLOCK
    off = NSA_BLOCK * (nblk - 1) + NSA_BLOCK - 1
    bdp = jnp.concatenate([jnp.zeros((NSA_HEADS, off), _F32), bd_nsa], axis=1)
    bc = jnp.stack([bdp[:, off - (NSA_BLOCK * n + NSA_BLOCK - 1):off - (NSA_BLOCK * n + NSA_BLOCK - 1) + seq]
                    for n in range(nblk)], axis=1)
    bc = bc.reshape(NSA_GROUPS, NSA_HPG, nblk, seq)
    rows, cols = DIL_TQ, DIL_PW + DIL_TQ
    period = rows + cols
    dtabs = []
    for g, (w, d) in enumerate(DIL_PAIRS):
        heads = bd[NSA_HEADS + g * DIL_HPG:NSA_HEADS + (g + 1) * DIL_HPG] * LOG2E
        m = w // d + 1
        assert m == DIL_PW + 1
        by_offset = heads[:, 0:m * d:d]
        cyc = jnp.concatenate([by_offset[:, ::-1], jnp.full((DIL_HPG, period - m), NEG, _F32)], axis=1)
        flat = jnp.tile(cyc, (1, rows))[:, :rows * (period - 1)]
        by_query = flat.reshape(DIL_HPG, rows, period - 1)[:, :, :cols]
        dtabs.append(by_query.transpose(0, 2, 1))
    return bt, bc, jnp.stack(dtabs, axis=0)


def _rope_tables(seq):
    half = MLA_ROPE // 2
    freqs = 1.0 / (ROPE_THETA ** (jnp.arange(half, dtype=_F32) / half))
    ang = jnp.arange(seq).astype(_F32)[:, None] * freqs[None, :]
    zeros = jnp.zeros((seq, LANES - MLA_ROPE), _F32)
    cos = jnp.concatenate([jnp.cos(ang), jnp.cos(ang), zeros], axis=1)
    sin = jnp.concatenate([jnp.sin(ang), jnp.sin(ang), zeros], axis=1)
    return cos, sin


def _rot_cols(w):
    half = w.shape[-1] // 2
    return jnp.concatenate([-w[..., half:], w[..., :half]], axis=-1)


ZF_COLS = {"gate_a": 0, "gate_b": A_GATE, "kcvc": A_GATE + B_GATE, "g3": A_GATE + B_GATE + 4 * HEAD_DIM}


def _even_weights(w_in):
    w_in = w_in.astype(_BF)
    o = 0
    q_a = w_in[:, o:o + A_Q]; o += A_Q
    kv_a = w_in[:, o:o + A_KV]; o += A_KV
    g3 = w_in[:, o:o + A_G3]; o += A_G3
    gate_a = w_in[:, o:o + A_GATE]; o += A_GATE
    qkv_b = w_in[:, o:o + B_QKV]; o += B_QKV
    gate_b = w_in[:, o:o + B_GATE]
    gw = NSA_GROUPS * HEAD_DIM
    kc_vc, ks, vs, kw, vw = (kv_a[:, :2 * gw], kv_a[:, 2 * gw:3 * gw], kv_a[:, 3 * gw:4 * gw],
                             kv_a[:, 4 * gw:5 * gw], kv_a[:, 5 * gw:6 * gw])
    wb = jnp.concatenate([q_a, ks, kw], axis=1).astype(_BF)
    wvt = jnp.concatenate([vs, vw], axis=1).T.astype(_BF)
    dw = DIL_HPG * HEAD_DIM
    nq = DIL_HEADS * HEAD_DIM
    wd = jnp.concatenate([qkv_b[:, part * nq + g * dw:part * nq + (g + 1) * dw]
                          for g in range(DIL_GROUPS) for part in range(3)], axis=1).astype(_BF)
    g3_blocks = []
    for g in range(NSA_GROUPS):
        cols = [(g * NSA_HPG + h) * 3 + j for j in range(3) for h in range(NSA_HPG)]
        blk = g3[:, np.asarray(cols)]
        g3_blocks.append(jnp.pad(blk, ((0, 0), (0, LANES - len(cols)))))
    wf = jnp.concatenate([gate_a, gate_b, kc_vc] + g3_blocks, axis=1).astype(_BF)
    return wb, wvt, wd, wf


def _even_layer(x2d, h, w_in, cmp_pos, cmp_w1, cmp_w2, w_out, nw_next, tables, bsz, seq, h_dtype):
    bt, bc, dtab, onehot = tables
    wb, wvt, wd, wf = _even_weights(w_in)
    zb = _matmul(h, wb, _BF)
    vt = _matmul_nt(wvt, h, _BF)
    zd = _matmul(h, wd, _BF)
    zf = _matmul(h, wf, _F32)
    nblk = seq // NSA_BLOCK
    kcvc = zf[:, ZF_COLS["kcvc"]:ZF_COLS["kcvc"] + 4 * HEAD_DIM]
    kcvc_t = kcvc.reshape(bsz * nblk, NSA_BLOCK, 4, HEAD_DIM).transpose(2, 1, 0, 3)
    kvc = _compress(kcvc_t, cmp_pos.astype(_F32), cmp_w1.astype(_BF), cmp_w2.astype(_BF))
    kvc = kvc.reshape(4, bsz, nblk, HEAD_DIM)
    zf3 = zf.reshape(bsz, seq, zf.shape[-1])
    o_a = _nsa_attention(zb.reshape(bsz, seq, -1), vt, zf3, kvc, onehot, bt, bc, bsz, seq)
    o_b = _dilated_attention(zd.reshape(bsz, seq, -1), dtab, bsz, seq)
    return _out_even(x2d, o_a.reshape(bsz * seq, -1), o_b.reshape(bsz * seq, -1), zf, w_out.astype(_BF), nw_next,
                     ZF_COLS, h_dtype)


def _odd_layer(x2d, h, w_in, q_norm, w_qb, kv_norm, w_kvb, w_out, nw_next, rope, bsz, seq, h_dtype, keep_x):
    cos, sin = rope
    w_in, w_qb, w_kvb = w_in.astype(_BF), w_qb.astype(_BF), w_kvb.astype(_BF)
    o = 0
    w_cq = w_in[:, o:o + MLA_Q_RANK]; o += MLA_Q_RANK
    w_ckv = w_in[:, o:o + MLA_KV_RANK]; o += MLA_KV_RANK
    w_kpe = w_in[:, o:o + MLA_ROPE]; o += MLA_ROPE
    w_gate = w_in[:, o:]
    w1 = jnp.concatenate([w_cq, w_ckv, w_gate], axis=1).astype(_BF)
    z = _matmul(h, w1, _F32)
    zk = _matmul(h, jnp.concatenate([w_kpe, _rot_cols(w_kpe)], axis=1).astype(_BF), _F32)
    gate_width = MLA_HEADS * MLA_V
    q_latent = (q_norm, MLA_Q_RANK, 0)
    kv_latent = (kv_norm, MLA_KV_RANK, MLA_Q_RANK // MLA_KV_RANK)
    kpe = _rope_cols(zk, 0, cos, sin, seq)
    wq = w_qb.reshape(MLA_Q_RANK, MLA_HEADS, MLA_NOPE + MLA_ROPE)
    wq_pe = wq[:, :, MLA_NOPE:]
    wq = jnp.concatenate([wq[:, :, :MLA_NOPE], wq_pe, _rot_cols(wq_pe)], axis=-1)
    q = _matmul_rope(z, q_latent, wq.reshape(MLA_Q_RANK, MLA_HEADS * 2 * LANES).astype(_BF), cos, sin, seq)
    wkv = w_kvb.reshape(MLA_KV_RANK, MLA_HEADS, MLA_NOPE + MLA_V)
    wk = wkv[:, :, :MLA_NOPE].reshape(MLA_KV_RANK, -1).astype(_BF)
    wv_t = wkv[:, :, MLA_NOPE:].reshape(MLA_KV_RANK, -1).T.astype(_BF)
    kn = _matmul(z, wk, _BF, norm=kv_latent)
    vt = _matmul_nt(wv_t, z, _BF, norm=kv_latent)
    o_c = _mla_attention(q.reshape(bsz, seq, -1), kn.reshape(bsz, seq, -1), kpe.reshape(bsz, seq, LANES),
                         vt, bsz, seq)
    gate_block = (MLA_Q_RANK + MLA_KV_RANK) // gate_width
    return _out_odd(x2d, o_c.reshape(bsz * seq, -1), z, w_out.astype(_BF), nw_next, gate_block, h_dtype, keep_x)


def kernel(x, rel_bias, norm_w, final_norm_w, ev_w_in, nsa_cmp_pos, nsa_cmp_w1, nsa_cmp_w2, ev_w_out,
           od_w_in, mla_q_norm, mla_w_qb, mla_kv_norm, mla_w_kvb, od_w_out):
    bsz, seq, d = x.shape
    depth = norm_w.shape[0]
    assert seq % MLA_TQ == 0 and seq % (DIL_TQ * DIL_PAIRS[-1][1]) == 0 and seq // NSA_BLOCK <= LANES
    assert NSA_WINDOW == 2 * NSA_TQ
    bt, bc, dtab = _bias_tables(rel_bias, seq)
    blk_id = np.arange(seq)[:, None] // NSA_BLOCK
    onehot = jnp.asarray(blk_id == np.arange(LANES)[None, :], dtype=_BF)
    tables = (bt, bc, dtab, onehot)
    rope = _rope_tables(seq)
    x2d = x.reshape(bsz * seq, d).astype(_F32)
    h = _rmsnorm(x2d, norm_w[0], d, 0, _BF)
    for l in range(depth):
        last = l == depth - 1
        nw_next = final_norm_w if last else norm_w[l + 1]
        h_dtype = _F32 if last else _BF
        i = l // 2
        if l % 2 == 0:
            x2d, h = _even_layer(x2d, h, ev_w_in[i], nsa_cmp_pos[i], nsa_cmp_w1[i], nsa_cmp_w2[i],
                                 ev_w_out[i], nw_next, tables, bsz, seq, h_dtype)
        else:
            x2d, h = _odd_layer(x2d, h, od_w_in[i], mla_q_norm[i], mla_w_qb[i], mla_kv_norm[i],
                                mla_w_kvb[i], od_w_out[i], nw_next, rope, bsz, seq, h_dtype, keep_x=not last)
    return h.reshape(bsz, seq, d)
```
